```python
import jax, jax.numpy as jnp
from jax import lax
import numpy as np

D_MODEL = 2048
BATCH = 4
SEQ = 8192
DEPTH = 2

N_EVEN = (DEPTH + 1) // 2
N_ODD = DEPTH // 2
HEAD_DIM = 64
EPS = 1e-6
LN_EPS = 1e-5

ATT_HEADS = 16
ATT_KV_HEADS = 4
ATT_GROUP = ATT_HEADS // ATT_KV_HEADS
ATT_Q = ATT_HEADS * HEAD_DIM
ATT_KV = ATT_KV_HEADS * HEAD_DIM
WINDOW = 128
ATT_BLOCK = 128
ROPE_THETA = 10000.0
CONV_CH = 1024
CONV_WIDTH = 31
AB_IN = ATT_Q + 2 * ATT_KV + 2 * CONV_CH
AB_OUT = ATT_Q + CONV_CH

RWKV_HEADS = 16
RWKV_DIM = RWKV_HEADS * HEAD_DIM
DECAY_LORA = 96
AAA_LORA = 96
GATE_LORA = 256
RWKV_LN_EPS = 64e-5
CD_SHIFT = 3 * RWKV_DIM + DECAY_LORA + AAA_LORA + GATE_LORA
LRU_WIDTH = 1024
LRU_BLOCKS = 16
LRU_BLOCK_DIM = LRU_WIDTH // LRU_BLOCKS
LRU_CONV_WIDTH = 4
LRU_C = 8.0
CD_IN = CD_SHIFT + 2 * LRU_WIDTH
CD_OUT = RWKV_DIM + LRU_WIDTH

N_EXPERTS = 16
N_GROUPS = 4
EXPERTS_PER_GROUP = N_EXPERTS // N_GROUPS
TOP_K = 2
D_EXPERT = 1024
MOE_BLOCK = 256

kernel_name = 'hybrid_swa_conformer_rwkv7_rglru_moe'


def rms_norm(x, g):
    xf = x.astype(jnp.float32)
    y = xf * lax.rsqrt(jnp.mean(xf * xf, axis=-1, keepdims=True) + EPS)
    return (y * g.astype(jnp.float32)).astype(x.dtype)


def layer_norm(x, g, b):
    xf = x.astype(jnp.float32)
    mu = jnp.mean(xf, axis=-1, keepdims=True)
    var = jnp.mean(jnp.square(xf - mu), axis=-1, keepdims=True)
    y = (xf - mu) * lax.rsqrt(var + LN_EPS)
    return (y * g.astype(jnp.float32) + b.astype(jnp.float32)).astype(x.dtype)


def modulate(h, shift, scale):
    return h * (1.0 + scale[:, None, :]) + shift[:, None, :]


def rope(x, positions):
    half = HEAD_DIM // 2
    inv_freq = ROPE_THETA ** (-jnp.arange(half, dtype=jnp.float32) / half)
    ang = positions.astype(jnp.float32)[..., None] * inv_freq
    cos = jnp.cos(ang)[:, :, None, :]
    sin = jnp.sin(ang)[:, :, None, :]
    xf = x.astype(jnp.float32)
    x1, x2 = xf[..., :half], xf[..., half:]
    return jnp.concatenate([x1 * cos - x2 * sin, x2 * cos + x1 * sin], axis=-1).astype(x.dtype)


def token_shift(z):
    return jnp.pad(z[:, :-1], ((0, 0), (1, 0), (0, 0)))


def causal_depthwise_conv(x, w, b):
    width = w.shape[0]
    y = lax.conv_general_dilated(x, w.astype(x.dtype), window_strides=(1,), padding=[(width - 1, 0)],
                                 dimension_numbers=('NWC', 'WIO', 'NWC'), feature_group_count=x.shape[-1])
    return y + b


def sliding_window_attention(q, k, v, sinks):
    B, S = q.shape[0], q.shape[1]
    nb = S // ATT_BLOCK
    qb = q.reshape(B, nb, ATT_BLOCK, ATT_KV_HEADS, ATT_GROUP, HEAD_DIM)
    kb = k.reshape(B, nb, ATT_BLOCK, ATT_KV_HEADS, HEAD_DIM)
    vb = v.reshape(B, nb, ATT_BLOCK, ATT_KV_HEADS, HEAD_DIM)

    def with_prev(t):
        prev = jnp.pad(t[:, :-1], ((0, 0), (1, 0), (0, 0), (0, 0), (0, 0)))
        return jnp.concatenate([prev, t], axis=2)

    kk, vv = with_prev(kb), with_prev(vb)
    s = jnp.einsum('bnqhgd,bnkhd->bnhgqk', qb, kk).astype(jnp.float32) * (HEAD_DIM ** -0.5)
    qi = jnp.arange(ATT_BLOCK)[:, None]
    kj = jnp.arange(2 * ATT_BLOCK)[None, :]
    dist = ATT_BLOCK + qi - kj
    kpos = (jnp.arange(nb)[:, None, None] - 1) * ATT_BLOCK + kj[None]
    valid = (dist >= 0) & (dist < WINDOW) & (kpos >= 0)
    s = jnp.where(valid[None, :, None, None], s, -1e30)
    sink = sinks.astype(jnp.float32).reshape(ATT_KV_HEADS, ATT_GROUP)[None, None, :, :, None, None]
    m = jnp.maximum(jnp.max(s, axis=-1, keepdims=True), sink)
    p = jnp.exp(s - m)
    p = p / (jnp.sum(p, axis=-1, keepdims=True) + jnp.exp(sink - m))
    o = jnp.einsum('bnhgqk,bnkhd->bnqhgd', p.astype(v.dtype), vv)
    return o.reshape(B, S, ATT_Q)


def conformer_conv(u, conv_w, conv_b, ln_g, ln_b):
    val, gate = u[..., :CONV_CH], u[..., CONV_CH:]
    y = val * jax.nn.sigmoid(gate)
    y = causal_depthwise_conv(y, conv_w, conv_b)
    return jax.nn.silu(layer_norm(y, ln_g, ln_b))


def mixer_ab(h, positions, w_in, sinks, conv_w, conv_b, ln_g, ln_b, w_out):
    B, S, _ = h.shape
    z = h @ w_in
    q, k, v, u = jnp.split(z, [ATT_Q, ATT_Q + ATT_KV, ATT_Q + 2 * ATT_KV], axis=-1)
    q = rope(q.reshape(B, S, ATT_HEADS, HEAD_DIM), positions)
    k = rope(k.reshape(B, S, ATT_KV_HEADS, HEAD_DIM), positions)
    v = v.reshape(B, S, ATT_KV_HEADS, HEAD_DIM)
    att = sliding_window_attention(q, k, v, sinks)
    conv = conformer_conv(u, conv_w, conv_b, ln_g, ln_b)
    return jnp.concatenate([att, conv], axis=-1) @ w_out


def _rwkv7_step(state, inp):
    r_t, w_t, k_t, v_t, kk_t, b_t = inp
    sa = jnp.einsum('bhvk,bhk->bhv', state, -kk_t)
    state = (state * w_t[:, :, None, :]
             + sa[..., None] * b_t[:, :, None, :]
             + v_t[..., None] * k_t[:, :, None, :])
    return state, jnp.einsum('bhvk,bhk->bhv', state, r_t)


def rwkv7_time_mix(r, k, v, zw, za, zg, w0, w2, a0, a2, g2, k_k, k_a, r_k, ln_g, ln_b):
    B, S, _ = r.shape
    f32 = jnp.float32
    heads = lambda t: t.reshape(B, S, RWKV_HEADS, HEAD_DIM)
    w = -jax.nn.softplus(-(w0 + jnp.tanh(zw) @ w2).astype(f32)) - 0.5
    decay = jnp.exp(-jnp.exp(w))
    a = jax.nn.sigmoid((a0 + za @ a2).astype(f32))
    g = jax.nn.sigmoid(zg) @ g2
    kk = heads((k * k_k).astype(f32))
    kk = kk * lax.rsqrt(jnp.maximum(jnp.sum(kk * kk, axis=-1, keepdims=True), 1e-24))
    k = k.astype(f32) * (1.0 + (a - 1.0) * k_a)
    r_h, k_h, v_h, w_h, a_h = heads(r.astype(f32)), heads(k), heads(v.astype(f32)), heads(decay), heads(a)
    xs = tuple(jnp.moveaxis(t, 1, 0) for t in (r_h, w_h, k_h, v_h, kk, kk * a_h))
    state0 = jnp.zeros((B, RWKV_HEADS, HEAD_DIM, HEAD_DIM), f32)
    _, y = lax.scan(_rwkv7_step, state0, xs)
    y = jnp.moveaxis(y, 0, 1)
    mu = jnp.mean(y, axis=-1, keepdims=True)
    var = jnp.mean(jnp.square(y - mu), axis=-1, keepdims=True)
    y = ((y - mu) * lax.rsqrt(var + RWKV_LN_EPS)).reshape(B, S, RWKV_DIM) * ln_g + ln_b
    bonus = jnp.sum(r_h * k_h * r_k, axis=-1, keepdims=True) * v_h
    y = (y + bonus.reshape(B, S, RWKV_DIM)) * g
    return y.astype(r.dtype)


def _linear_recurrence_combine(left, right):
    a_l, b_l = left
    a_r, b_r = right
    return a_l * a_r, a_r * b_l + b_r


def rglru_branch(zd, conv_w, conv_b, wa, ba, wx, bx, lam):
    B, S, _ = zd.shape
    f32 = jnp.float32
    xb, gb = zd[..., :LRU_WIDTH], zd[..., LRU_WIDTH:]
    xb = causal_depthwise_conv(xb, conv_w, conv_b)
    xh = xb.reshape(B, S, LRU_BLOCKS, LRU_BLOCK_DIM)
    gate_r = jnp.einsum('bsgi,gij->bsgj', xh, wa).reshape(B, S, LRU_WIDTH) + ba
    gate_i = jnp.einsum('bsgi,gij->bsgj', xh, wx).reshape(B, S, LRU_WIDTH) + bx
    r = jax.nn.sigmoid(gate_r.astype(f32))
    i = jax.nn.sigmoid(gate_i.astype(f32))
    log_a = -LRU_C * r * jax.nn.softplus(-lam.astype(f32))
    a = jnp.exp(log_a)
    u = jnp.sqrt(-jnp.expm1(2.0 * log_a)) * (i * xb.astype(f32))
    _, hseq = lax.associative_scan(_linear_recurrence_combine, (a, u), axis=1)
    return (hseq * jax.nn.gelu(gb.astype(f32))).astype(zd.dtype)


def mixer_cd(h, w_in, shift_mu, w0, w2, a0, a2, g2, k_k, k_a, r_k, ln_g, ln_b,
             conv_w, conv_b, wa, ba, wx, bx, lam, w_out):
    z = h @ w_in
    zc, zd = z[..., :CD_SHIFT], z[..., CD_SHIFT:]
    zc = zc + shift_mu * (token_shift(zc) - zc)
    r, k, v, zw, za, zg = jnp.split(zc, [RWKV_DIM, 2 * RWKV_DIM, 3 * RWKV_DIM,
                                         3 * RWKV_DIM + DECAY_LORA,
                                         3 * RWKV_DIM + DECAY_LORA + AAA_LORA], axis=-1)
    y_c = rwkv7_time_mix(r, k, v, zw, za, zg, w0, w2, a0, a2, g2, k_k, k_a, r_k, ln_g, ln_b)
    y_d = rglru_branch(zd, conv_w, conv_b, wa, ba, wx, bx, lam)
    return jnp.concatenate([y_c, y_d], axis=-1) @ w_out


def route(xt, router_w, router_bias):
    probs = jax.nn.softmax((xt @ router_w).astype(jnp.float32), axis=-1)
    sel = (probs + router_bias.astype(jnp.float32)).reshape(-1, N_GROUPS, EXPERTS_PER_GROUP)
    group_score = jnp.sum(lax.top_k(sel, TOP_K)[0], axis=-1)
    g_idx = jnp.argmax(group_score, axis=-1)
    in_group = jnp.take_along_axis(sel, g_idx[:, None, None], axis=1)[:, 0]
    _, local = lax.top_k(in_group, TOP_K)
    expert_idx = (g_idx[:, None] * EXPERTS_PER_GROUP + local).astype(jnp.int32)
    gate = jnp.take_along_axis(probs, expert_idx, axis=1)
    return expert_idx, gate / jnp.sum(gate, axis=-1, keepdims=True)


def moe_ffn(h, router_w, router_bias, w1, w3, w2):
    B, S, D = h.shape
    xt = h.reshape(-1, D)
    n = xt.shape[0]
    nk = n * TOP_K
    expert_idx, gate = route(xt, router_w, router_bias)
    flat_e = expert_idx.reshape(-1)
    flat_tok = jnp.repeat(jnp.arange(n, dtype=jnp.int32), TOP_K)
    flat_w = gate.reshape(-1)
    order = jnp.argsort(flat_e)
    se = flat_e[order]
    counts = jnp.zeros((N_EXPERTS,), jnp.int32).at[flat_e].add(1)
    padded = (counts + MOE_BLOCK - 1) // MOE_BLOCK * MOE_BLOCK
    start = jnp.cumsum(counts) - counts
    pstart = jnp.cumsum(padded) - padded
    pend = pstart + padded
    dest = pstart[se] + (jnp.arange(nk, dtype=jnp.int32) - start[se])
    n_blocks = -(-nk // MOE_BLOCK) + N_EXPERTS
    rows = n_blocks * MOE_BLOCK
    row_tok = jnp.full((rows,), n, jnp.int32).at[dest].set(flat_tok[order])
    row_w = jnp.zeros((rows,), h.dtype).at[dest].set(flat_w[order].astype(h.dtype))
    block_start = jnp.arange(n_blocks, dtype=jnp.int32) * MOE_BLOCK
    block_e = jnp.minimum(jnp.sum(block_start[:, None] >= pend[None, :], axis=1), N_EXPERTS - 1)
    x_pad = jnp.concatenate([xt, jnp.zeros((1, D), xt.dtype)], axis=0)
    xb = x_pad[row_tok].reshape(n_blocks, MOE_BLOCK, D)

    def expert_block(args):
        xblk, e = args
        hid = jax.nn.silu(xblk @ w1[e]) * (xblk @ w3[e])
        return hid @ w2[e]

    yb = lax.map(expert_block, (xb, block_e))
    y = jnp.zeros((n + 1, D), h.dtype).at[row_tok].add(yb.reshape(rows, D) * row_w[:, None])
    return y[:n].reshape(B, S, D)


def setup_inputs(seed: int = 0) -> dict:
    key = jax.random.key(seed)
    ks = iter(jax.random.split(key, 48))
    f32 = jnp.float32

    def nrm(shape, scale):
        return scale * jax.random.normal(next(ks), shape, f32)

    def gain(shape):
        return 1.0 + nrm(shape, 0.02)

    def unif(shape, lo, hi):
        return jax.random.uniform(next(ks), shape, f32, lo, hi)

    D = D_MODEL
    x = nrm((BATCH, SEQ, D), 1.0)
    c = nrm((BATCH, D), 1.0)
    positions = (jax.random.randint(next(ks), (BATCH, 1), 0, 1024, jnp.int32)
                 + jnp.arange(SEQ, dtype=jnp.int32)[None, :])
    router_w = nrm((D, N_EXPERTS), D ** -0.5)
    router_bias = nrm((N_EXPERTS,), 0.01)
    ada_w = nrm((DEPTH, D, 6 * D), 0.5 * D ** -0.5)
    ada_b = nrm((DEPTH, 6 * D), 0.02)
    norm_mix = gain((DEPTH, D))
    norm_ffn = gain((DEPTH, D))
    moe_w1 = nrm((DEPTH, N_EXPERTS, D, D_EXPERT), D ** -0.5)
    moe_w3 = nrm((DEPTH, N_EXPERTS, D, D_EXPERT), D ** -0.5)
    moe_w2 = nrm((DEPTH, N_EXPERTS, D_EXPERT, D), D_EXPERT ** -0.5)
    ab_w_in = nrm((N_EVEN, D, AB_IN), D ** -0.5)
    ab_sinks = nrm((N_EVEN, ATT_HEADS), 0.5)
    ab_conv_w = nrm((N_EVEN, CONV_WIDTH, 1, CONV_CH), CONV_WIDTH ** -0.5)
    ab_conv_b = nrm((N_EVEN, CONV_CH), 0.02)
    ab_conv_ln_g = gain((N_EVEN, CONV_CH))
    ab_conv_ln_b = nrm((N_EVEN, CONV_CH), 0.02)
    ab_w_out = nrm((N_EVEN, AB_OUT, D), AB_OUT ** -0.5)
    cd_w_in = nrm((N_ODD, D, CD_IN), D ** -0.5)
    cd_shift_mu = unif((N_ODD, CD_SHIFT), 0.0, 1.0)
    cd_w0 = unif((N_ODD, RWKV_DIM), -3.0, 1.0)
    cd_w2 = nrm((N_ODD, DECAY_LORA, RWKV_DIM), 0.1 * DECAY_LORA ** -0.5)
    cd_a0 = nrm((N_ODD, RWKV_DIM), 0.5)
    cd_a2 = nrm((N_ODD, AAA_LORA, RWKV_DIM), 0.1 * AAA_LORA ** -0.5)
    cd_g2 = nrm((N_ODD, GATE_LORA, RWKV_DIM), GATE_LORA ** -0.5)
    cd_k_k = 0.85 + nrm((N_ODD, RWKV_DIM), 0.02)
    cd_k_a = gain((N_ODD, RWKV_DIM))
    cd_r_k = nrm((N_ODD, RWKV_HEADS, HEAD_DIM), 0.1)
    cd_ln_x_g = gain((N_ODD, RWKV_DIM))
    cd_ln_x_b = nrm((N_ODD, RWKV_DIM), 0.02)
    cd_lru_conv_w = nrm((N_ODD, LRU_CONV_WIDTH, 1, LRU_WIDTH), 0.5)
    cd_lru_conv_b = nrm((N_ODD, LRU_WIDTH), 0.02)
    cd_lru_wa = nrm((N_ODD, LRU_BLOCKS, LRU_BLOCK_DIM, LRU_BLOCK_DIM), LRU_BLOCK_DIM ** -0.5)
    cd_lru_ba = nrm((N_ODD, LRU_WIDTH), 0.02)
    cd_lru_wx = nrm((N_ODD, LRU_BLOCKS, LRU_BLOCK_DIM, LRU_BLOCK_DIM), LRU_BLOCK_DIM ** -0.5)
    cd_lru_bx = nrm((N_ODD, LRU_WIDTH), 0.02)
    a_c = unif((N_ODD, LRU_WIDTH), 0.9, 0.999)
    sig = a_c ** (1.0 / LRU_C)
    cd_lru_lambda = jnp.log(sig) - jnp.log1p(-sig)
    cd_w_out = nrm((N_ODD, CD_OUT, D), CD_OUT ** -0.5)
    final_norm = gain((D,))
    return {'x': x, 'c': c, 'positions': positions, 'router_w': router_w, 'router_bias': router_bias,
            'ada_w': ada_w, 'ada_b': ada_b, 'norm_mix': norm_mix, 'norm_ffn': norm_ffn,
            'moe_w1': moe_w1, 'moe_w3': moe_w3, 'moe_w2': moe_w2,
            'ab_w_in': ab_w_in, 'ab_sinks': ab_sinks, 'ab_conv_w': ab_conv_w, 'ab_conv_b': ab_conv_b,
            'ab_conv_ln_g': ab_conv_ln_g, 'ab_conv_ln_b': ab_conv_ln_b, 'ab_w_out': ab_w_out,
            'cd_w_in': cd_w_in, 'cd_shift_mu': cd_shift_mu, 'cd_w0': cd_w0, 'cd_w2': cd_w2,
            'cd_a0': cd_a0, 'cd_a2': cd_a2, 'cd_g2': cd_g2, 'cd_k_k': cd_k_k, 'cd_k_a': cd_k_a,
            'cd_r_k': cd_r_k, 'cd_ln_x_g': cd_ln_x_g, 'cd_ln_x_b': cd_ln_x_b,
            'cd_lru_conv_w': cd_lru_conv_w, 'cd_lru_conv_b': cd_lru_conv_b,
            'cd_lru_wa': cd_lru_wa, 'cd_lru_ba': cd_lru_ba, 'cd_lru_wx': cd_lru_wx, 'cd_lru_bx': cd_lru_bx,
            'cd_lru_lambda': cd_lru_lambda, 'cd_w_out': cd_w_out, 'final_norm': final_norm}


def reference(x, c, positions, router_w, router_bias, ada_w, ada_b, norm_mix, norm_ffn,
              moe_w1, moe_w3, moe_w2, ab_w_in, ab_sinks, ab_conv_w, ab_conv_b, ab_conv_ln_g,
              ab_conv_ln_b, ab_w_out, cd_w_in, cd_shift_mu, cd_w0, cd_w2, cd_a0, cd_a2, cd_g2,
              cd_k_k, cd_k_a, cd_r_k, cd_ln_x_g, cd_ln_x_b, cd_lru_conv_w, cd_lru_conv_b,
              cd_lru_wa, cd_lru_ba, cd_lru_wx, cd_lru_bx, cd_lru_lambda, cd_w_out, final_norm):
    c_act = jax.nn.silu(c)
    for layer in range(DEPTH):
        j = layer // 2
        mod = c_act @ ada_w[layer] + ada_b[layer]
        shift_m, scale_m, gate_m, shift_f, scale_f, gate_f = jnp.split(mod, 6, axis=-1)
        h = modulate(rms_norm(x, norm_mix[layer]), shift_m, scale_m)
        if layer % 2 == 0:
            y = mixer_ab(h, positions, ab_w_in[j], ab_sinks[j], ab_conv_w[j], ab_conv_b[j],
                         ab_conv_ln_g[j], ab_conv_ln_b[j], ab_w_out[j])
        else:
            y = mixer_cd(h, cd_w_in[j], cd_shift_mu[j], cd_w0[j], cd_w2[j], cd_a0[j], cd_a2[j],
                         cd_g2[j], cd_k_k[j], cd_k_a[j], cd_r_k[j], cd_ln_x_g[j], cd_ln_x_b[j],
                         cd_lru_conv_w[j], cd_lru_conv_b[j], cd_lru_wa[j], cd_lru_ba[j],
                         cd_lru_wx[j], cd_lru_bx[j], cd_lru_lambda[j], cd_w_out[j])
        x = x + gate_m[:, None, :] * y
        h = modulate(rms_norm(x, norm_ffn[layer]), shift_f, scale_f)
        x = x + gate_f[:, None, :] * moe_ffn(h, router_w, router_bias, moe_w1[layer], moe_w3[layer], moe_w2[layer])
    return rms_norm(x, final_norm)
```

```python
import functools

import jax
import jax.numpy as jnp
from jax import lax
from jax.experimental import pallas as pl
from jax.experimental.pallas import tpu as pltpu

F32 = jnp.float32
BF16 = jnp.bfloat16

EPS = 1e-6
LN_EPS = 1e-5
RWKV_LN_EPS = 64e-5
HEAD_DIM = 64
LANES = 128
ATT_BLOCK = 128
WINDOW = 128
ATT_HEADS = 16
ATT_KV_HEADS = 4
CONV_WIDTH = 31
CONV_HALO = 32
LRU_CONV_WIDTH = 4
LRU_C = 8.0
N_EXPERTS = 16
N_GROUPS = 4
EXPERTS_PER_GROUP = 4
MOE_BLOCK = 256
RWKV_CHUNK = 64
ROPE_THETA = 10000.0
VMEM_LIMIT = 56 * 1024 * 1024


def _cparams(sem):
    return pltpu.CompilerParams(dimension_semantics=sem, vmem_limit_bytes=VMEM_LIMIT)


def _bdot(a, b):
    return jnp.dot(a.astype(BF16), b.astype(BF16), preferred_element_type=F32)


def _split3(x):
    hi = x.astype(BF16)
    r1 = x - hi.astype(F32)
    mid = r1.astype(BF16)
    lo = (r1 - mid.astype(F32)).astype(BF16)
    return hi, mid, lo


def _dot_left_exact(m_bf16, x):
    hi, mid, lo = _split3(x)
    d = functools.partial(jnp.dot, preferred_element_type=F32)
    return d(m_bf16, hi) + d(m_bf16, mid) + d(m_bf16, lo)


def _dot_right_exact(x, m_bf16):
    hi, mid, lo = _split3(x)
    d = functools.partial(jnp.dot, preferred_element_type=F32)
    return d(hi, m_bf16) + d(mid, m_bf16) + d(lo, m_bf16)


def _dot_3pass(a, b):
    a_hi = a.astype(BF16)
    a_lo = (a - a_hi.astype(F32)).astype(BF16)
    b_hi = b.astype(BF16)
    b_lo = (b - b_hi.astype(F32)).astype(BF16)
    d = functools.partial(jnp.dot, preferred_element_type=F32)
    return d(a_hi, b_hi) + d(a_hi, b_lo) + d(a_lo, b_hi)


def _sigmoid(x):
    return 1.0 / (1.0 + jnp.exp(-x))


def _softplus(x):
    return jnp.maximum(x, 0.0) + jnp.log(1.0 + jnp.exp(-jnp.abs(x)))


def _rms_mod(x, g, shift, scale):
    ms = jnp.mean(x * x, axis=-1, keepdims=True)
    return (x * lax.rsqrt(ms + EPS) * g) * (1.0 + scale) + shift


def _ada_kernel(c_ref, w_ref, b_ref, o_ref):
    c = c_ref[...]
    o_ref[0] = _dot_3pass(c * _sigmoid(c), w_ref[0]) + b_ref[0]


def _ada_mod(c, ada_w, ada_b):
    depth, d, n6 = ada_w.shape
    b = c.shape[0]
    bp = 8
    tn = 768
    c_pad = jnp.zeros((bp, d), F32).at[:b].set(c)
    out = pl.pallas_call(
        _ada_kernel,
        grid=(depth, n6 // tn),
        in_specs=[pl.BlockSpec((bp, d), lambda l, j: (0, 0)),
                  pl.BlockSpec((1, d, tn), lambda l, j: (l, 0, j)),
                  pl.BlockSpec((1, 1, tn), lambda l, j: (l, 0, j))],
        out_specs=pl.BlockSpec((1, bp, tn), lambda l, j: (l, 0, j)),
        out_shape=jax.ShapeDtypeStruct((depth, bp, n6), F32),
        compiler_params=_cparams(("parallel", "parallel")),
        name="ada_mod",
    )(c_pad, ada_w, ada_b.reshape(depth, 1, n6))
    return out[:, :b]


def _rope_kernel(pos_ref, f_ref, cos_ref, sin_ref):
    ang = pos_ref[...].astype(F32) * f_ref[...]
    lane = lax.broadcasted_iota(jnp.int32, ang.shape, 1)
    s = jnp.sin(ang)
    cos_ref[...] = jnp.cos(ang)
    sin_ref[...] = jnp.where((lane & 32) == 0, -s, s)


def _rope_tables(positions):
    n = positions.size
    half = HEAD_DIM // 2
    inv_freq = ROPE_THETA ** (-jnp.arange(half, dtype=F32) / half)
    f_row = jnp.tile(inv_freq, LANES // half).reshape(1, LANES)
    tm = 512
    return pl.pallas_call(
        _rope_kernel,
        grid=(n // tm,),
        in_specs=[pl.BlockSpec((tm, 1), lambda i: (i, 0)),
                  pl.BlockSpec((1, LANES), lambda i: (0, 0))],
        out_specs=[pl.BlockSpec((tm, LANES), lambda i: (i, 0))] * 2,
        out_shape=[jax.ShapeDtypeStruct((n, LANES), F32)] * 2,
        compiler_params=_cparams(("parallel",)),
        name="rope_tables",
    )(positions.reshape(n, 1), f_row)


def _in_proj_kernel(x_ref, g_ref, sh_ref, sc_ref, w_ref, o_ref, h_ref):
    @pl.when(pl.program_id(1) == 0)
    def _():
        h_ref[...] = _rms_mod(x_ref[...], g_ref[...], sh_ref[0], sc_ref[0]).astype(BF16)

    o_ref[...] = jnp.dot(h_ref[...], w_ref[...], preferred_element_type=F32)


def _in_proj(x, g, shift, scale, w_bf16, seq, tm, tn):
    n, d = x.shape
    nout = w_bf16.shape[1]
    per_seq = seq // tm
    return pl.pallas_call(
        _in_proj_kernel,
        grid=(n // tm, nout // tn),
        in_specs=[pl.BlockSpec((tm, d), lambda i, j: (i, 0)),
                  pl.BlockSpec((1, d), lambda i, j: (0, 0)),
                  pl.BlockSpec((1, 1, d), lambda i, j: (i // per_seq, 0, 0)),
                  pl.BlockSpec((1, 1, d), lambda i, j: (i // per_seq, 0, 0)),
                  pl.BlockSpec((d, tn), lambda i, j: (0, j))],
        out_specs=pl.BlockSpec((tm, tn), lambda i, j: (i, j)),
        out_shape=jax.ShapeDtypeStruct((n, nout), F32),
        scratch_shapes=[pltpu.VMEM((tm, d), BF16)],
        compiler_params=_cparams(("parallel", "arbitrary")),
        name="in_proj",
    )(x, g.reshape(1, d), shift, scale, w_bf16)


def _route(logits_t, bias_col):
    m = jnp.max(logits_t, axis=0, keepdims=True)
    e = jnp.exp(logits_t - m)
    probs = e / jnp.sum(e, axis=0, keepdims=True)
    sel = probs + bias_col
    s = [sel[i:i + 1, :] for i in range(N_EXPERTS)]
    pr = [probs[i:i + 1, :] for i in range(N_EXPERTS)]

    def top2_sum(v):
        best = v[0] + v[1]
        for i in range(len(v)):
            for j in range(i + 1, len(v)):
                if (i, j) != (0, 1):
                    best = jnp.maximum(best, v[i] + v[j])
        return best

    gs = [top2_sum(s[4 * g:4 * g + 4]) for g in range(N_GROUPS)]
    best = gs[0]
    gi = jnp.zeros_like(best, dtype=jnp.int32)
    for g in range(1, N_GROUPS):
        upd = gs[g] > best
        gi = jnp.where(upd, g, gi)
        best = jnp.where(upd, gs[g], best)

    def pick_group(rows):
        out = []
        for l in range(EXPERTS_PER_GROUP):
            v = rows[l]
            for g in range(1, N_GROUPS):
                v = jnp.where(gi == g, rows[4 * g + l], v)
            out.append(v)
        return out

    ig = pick_group(s)
    pg = pick_group(pr)

    def argmax4(v):
        bv = v[0]
        bi = jnp.zeros_like(gi)
        for l in range(1, EXPERTS_PER_GROUP):
            upd = v[l] > bv
            bi = jnp.where(upd, l, bi)
            bv = jnp.where(upd, v[l], bv)
        return bi

    i1 = argmax4(ig)
    i2 = argmax4([jnp.where(i1 == l, -jnp.inf, ig[l]) for l in range(EXPERTS_PER_GROUP)])

    def pick(v, idx):
        o = v[0]
        for l in range(1, EXPERTS_PER_GROUP):
            o = jnp.where(idx == l, v[l], o)
        return o

    p1 = pick(pg, i1)
    p2 = pick(pg, i2)
    tot = p1 + p2
    return gi * EXPERTS_PER_GROUP + i1, gi * EXPERTS_PER_GROUP + i2, p1 / tot, p2 / tot


def _out_proj_kernel(a1_ref, a2_ref, w1_ref, w2_ref, x_ref, gate_ref, g_ref, sh_ref, sc_ref,
                     rw_ref, rb_ref, xo_ref, h_ref, ridx_ref, rgate_ref):
    y = (jnp.dot(a1_ref[...], w1_ref[...], preferred_element_type=F32)
         + jnp.dot(a2_ref[...], w2_ref[...], preferred_element_type=F32))
    x1 = x_ref[...] + gate_ref[0] * y
    xo_ref[...] = x1
    h = _rms_mod(x1, g_ref[...], sh_ref[0], sc_ref[0])
    h_ref[...] = h.astype(BF16)
    logits = _dot_3pass(h, rw_ref[...])
    logits_t = logits.T[:N_EXPERTS, :]
    e1, e2, g1, g2 = _route(logits_t, rb_ref[...])
    row = lax.broadcasted_iota(jnp.int32, ridx_ref.shape, 0)
    ridx_ref[...] = jnp.where(row == 0, e1, jnp.where(row == 1, e2, 0))
    rgate_ref[...] = jnp.where(row == 0, g1, jnp.where(row == 1, g2, 0.0))


def _out_proj(a1, a2, w_out, x, gate, g, shift, scale, router_w, router_bias, seq, tm):
    n, d = x.shape
    ka = a1.shape[1]
    per_seq = seq // tm
    w1 = w_out[:ka].astype(BF16)
    w2 = w_out[ka:].astype(BF16)
    rw = jnp.zeros((d, LANES), F32).at[:, :N_EXPERTS].set(router_w)
    rb = router_bias.astype(F32).reshape(N_EXPERTS, 1)
    mod_spec = pl.BlockSpec((1, 1, d), lambda i: (i // per_seq, 0, 0))
    return pl.pallas_call(
        _out_proj_kernel,
        grid=(n // tm,),
        in_specs=[pl.BlockSpec((tm, ka), lambda i: (i, 0)),
                  pl.BlockSpec((tm, ka), lambda i: (i, 0)),
                  pl.BlockSpec((ka, d), lambda i: (0, 0)),
                  pl.BlockSpec((ka, d), lambda i: (0, 0)),
                  pl.BlockSpec((tm, d), lambda i: (i, 0)),
                  mod_spec,
                  pl.BlockSpec((1, d), lambda i: (0, 0)),
                  mod_spec, mod_spec,
                  pl.BlockSpec((d, LANES), lambda i: (0, 0)),
                  pl.BlockSpec((N_EXPERTS, 1), lambda i: (0, 0))],
        out_specs=[pl.BlockSpec((tm, d), lambda i: (i, 0)),
                   pl.BlockSpec((tm, d), lambda i: (i, 0)),
                   pl.BlockSpec((8, tm), lambda i: (0, i)),
                   pl.BlockSpec((8, tm), lambda i: (0, i))],
        out_shape=[jax.ShapeDtypeStruct((n, d), F32),
                   jax.ShapeDtypeStruct((n, d), BF16),
                   jax.ShapeDtypeStruct((8, n), jnp.int32),
                   jax.ShapeDtypeStruct((8, n), F32)],
        compiler_params=_cparams(("parallel",)),
        name="out_proj_route",
    )(a1, a2, w1, w2, x, gate, g.reshape(1, d), shift, scale, rw, rb)


def _rope_apply(x, cos, sin_signed):
    outs = []
    lane = lax.broadcasted_iota(jnp.int32, cos.shape, 1)
    first_half = (lane & 32) == 0
    for c in range(x.shape[1] // LANES):
        xc = x[:, c * LANES:(c + 1) * LANES]
        partner = jnp.where(first_half, pltpu.roll(xc, LANES - 32, axis=1), pltpu.roll(xc, 32, axis=1))
        outs.append(xc * cos + partner * sin_signed)
    return outs


def _attn_kernel(sink_ref, q_ref, kp_ref, kc_ref, vp_ref, vc_ref, cq_ref, sq_ref, cp_ref, sp_ref, o_ref):
    has_prev = pl.program_id(1) > 0
    cq, sq = cq_ref[...], sq_ref[...]
    q_cols = _rope_apply(q_ref[...] * (HEAD_DIM ** -0.5), cq, sq)
    k_cols = [jnp.concatenate([kp, kc], axis=0) for kp, kc in
              zip(_rope_apply(kp_ref[...], cp_ref[...], sp_ref[...]), _rope_apply(kc_ref[...], cq, sq))]
    v_all = jnp.concatenate([vp_ref[...], vc_ref[...]], axis=0).astype(BF16)
    qi = lax.broadcasted_iota(jnp.int32, (ATT_BLOCK, 2 * ATT_BLOCK), 0)
    kj = lax.broadcasted_iota(jnp.int32, (ATT_BLOCK, 2 * ATT_BLOCK), 1)
    valid = ((kj > qi) & (kj < ATT_BLOCK) & has_prev) | ((kj >= ATT_BLOCK) & (kj - ATT_BLOCK <= qi))
    group = ATT_HEADS // ATT_KV_HEADS
    out_cols = []
    for pair in range(ATT_HEADS // 2):
        halves = []
        for sub in range(2):
            hq = 2 * pair + sub
            hk = hq // group
            qh = q_cols[pair][:, sub * HEAD_DIM:(sub + 1) * HEAD_DIM].astype(BF16)
            kh = k_cols[hk // 2][:, (hk % 2) * HEAD_DIM:(hk % 2 + 1) * HEAD_DIM].astype(BF16)
            vh = v_all[:, hk * HEAD_DIM:(hk + 1) * HEAD_DIM]
            s = lax.dot_general(qh, kh, (((1,), (1,)), ((), ())), preferred_element_type=F32)
            s = jnp.where(valid, s, -1e30)
            sink = sink_ref[hq]
            m = jnp.maximum(jnp.max(s, axis=-1, keepdims=True), sink)
            p = jnp.exp(s - m)
            den = jnp.sum(p, axis=-1, keepdims=True) + jnp.exp(sink - m)
            halves.append(jnp.dot(p.astype(BF16), vh, preferred_element_type=F32) / den)
        out_cols.append(jnp.concatenate(halves, axis=1))
    o_ref[...] = jnp.concatenate(out_cols, axis=1).astype(o_ref.dtype)


def _attention(z, cos_t, sin_t, sinks, batch, seq, k_col, v_col):
    n = z.shape[0]
    nb = seq // ATT_BLOCK
    att_q = ATT_HEADS * HEAD_DIM
    att_kv = ATT_KV_HEADS * HEAD_DIM
    cur = lambda b, i: b * nb + i
    prev = lambda b, i: b * nb + jnp.maximum(i - 1, 0)
    return pl.pallas_call(
        _attn_kernel,
        grid=(batch, nb),
        in_specs=[pl.BlockSpec(memory_space=pltpu.SMEM),
                  pl.BlockSpec((ATT_BLOCK, att_q), lambda b, i: (cur(b, i), 0)),
                  pl.BlockSpec((ATT_BLOCK, att_kv), lambda b, i: (prev(b, i), k_col)),
                  pl.BlockSpec((ATT_BLOCK, att_kv), lambda b, i: (cur(b, i), k_col)),
                  pl.BlockSpec((ATT_BLOCK, att_kv), lambda b, i: (prev(b, i), v_col)),
                  pl.BlockSpec((ATT_BLOCK, att_kv), lambda b, i: (cur(b, i), v_col)),
                  pl.BlockSpec((ATT_BLOCK, LANES), lambda b, i: (cur(b, i), 0)),
                  pl.BlockSpec((ATT_BLOCK, LANES), lambda b, i: (cur(b, i), 0)),
                  pl.BlockSpec((ATT_BLOCK, LANES), lambda b, i: (prev(b, i), 0)),
                  pl.BlockSpec((ATT_BLOCK, LANES), lambda b, i: (prev(b, i), 0))],
        out_specs=pl.BlockSpec((ATT_BLOCK, att_q), lambda b, i: (cur(b, i), 0)),
        out_shape=jax.ShapeDtypeStruct((n, att_q), BF16),
        compiler_params=_cparams(("parallel", "parallel")),
        name="swa_attention",
    )(sinks.astype(F32), z, z, z, z, z, cos_t, sin_t, cos_t, sin_t)


def _conv_kernel(val_ref, gate_ref, hval_ref, hgate_ref, w_ref, b_ref, g_ref, beta_ref, o_ref, ybuf, acc):
    t = val_ref.shape[0]
    ch = val_ref.shape[1]
    first = pl.program_id(1) == 0
    yh = hval_ref[...] * _sigmoid(hgate_ref[...])
    ybuf[0:CONV_HALO, :] = jnp.where(first, 0.0, yh)
    ybuf[CONV_HALO:, :] = val_ref[...] * _sigmoid(gate_ref[...])
    rt = 64
    off = CONV_HALO - (CONV_WIDTH - 1)
    for cb in range(ch // LANES):
        cs = slice(cb * LANES, (cb + 1) * LANES)
        for r0 in range(0, t, rt):
            a = jnp.broadcast_to(b_ref[:, cs], (rt, LANES))
            for j in range(CONV_WIDTH):
                a = a + w_ref[j:j + 1, cs] * ybuf[r0 + off + j:r0 + off + j + rt, cs]
            acc[r0:r0 + rt, cs] = a
    y = acc[...]
    mu = jnp.mean(y, axis=-1, keepdims=True)
    yc = y - mu
    var = jnp.mean(yc * yc, axis=-1, keepdims=True)
    ln = yc * lax.rsqrt(var + LN_EPS) * g_ref[...] + beta_ref[...]
    o_ref[...] = (ln * _sigmoid(ln)).astype(o_ref.dtype)


def _conformer_conv(z, conv_w, conv_b, ln_g, ln_b, batch, seq, val_col, gate_col, t):
    n = z.shape[0]
    ch = conv_w.shape[-1]
    nt = seq // t
    hpt = t // CONV_HALO
    w = jnp.zeros((CONV_HALO, ch), F32).at[:CONV_WIDTH].set(conv_w.reshape(CONV_WIDTH, ch))
    cur = lambda b, i: b * nt + i
    halo = lambda b, i: jnp.maximum((b * nt + i) * hpt - 1, 0)
    vec = pl.BlockSpec((1, ch), lambda b, i: (0, 0))
    return pl.pallas_call(
        _conv_kernel,
        grid=(batch, nt),
        in_specs=[pl.BlockSpec((t, ch), lambda b, i: (cur(b, i), val_col)),
                  pl.BlockSpec((t, ch), lambda b, i: (cur(b, i), gate_col)),
                  pl.BlockSpec((CONV_HALO, ch), lambda b, i: (halo(b, i), val_col)),
                  pl.BlockSpec((CONV_HALO, ch), lambda b, i: (halo(b, i), gate_col)),
                  pl.BlockSpec((CONV_HALO, ch), lambda b, i: (0, 0)),
                  vec, vec, vec],
        out_specs=pl.BlockSpec((t, ch), lambda b, i: (cur(b, i), 0)),
        out_shape=jax.ShapeDtypeStruct((n, ch), BF16),
        scratch_shapes=[pltpu.VMEM((t + CONV_HALO, ch), F32), pltpu.VMEM((t, ch), F32)],
        compiler_params=_cparams(("parallel", "parallel")),
        name="conformer_conv",
    )(z, z, z, z, w, conv_b.reshape(1, ch), ln_g.reshape(1, ch), ln_b.reshape(1, ch))


def _shift_lerp(z, halo, mu, first):
    prev = pltpu.roll(z, 1, axis=0)
    row = lax.broadcasted_iota(jnp.int32, z.shape, 0)
    hrow = jnp.where(first, 0.0, halo[7:8, :])
    prev = jnp.where(row == 0, hrow, prev)
    return z + mu * (prev - z)


def _rwkv_kernel(r_ref, k_ref, v_ref, lo_ref, rh_ref, kh_ref, vh_ref, loh_ref,
                 mur_ref, muk_ref, muv_ref, mulo_ref, w0_ref, w2_ref, a0_ref, a2_ref, g2_ref,
                 kk_ref, ka_ref, rk_ref, lng_ref, lnb_ref, seg_ref, segt_ref, tril_ref,
                 o_ref, hs_ref):
    first = pl.program_id(1) == 0

    @pl.when(first)
    def _():
        hs_ref[...] = jnp.zeros_like(hs_ref)

    r = _shift_lerp(r_ref[...], rh_ref[...], mur_ref[...], first)
    k = _shift_lerp(k_ref[...], kh_ref[...], muk_ref[...], first)
    v = _shift_lerp(v_ref[...], vh_ref[...], muv_ref[...], first)
    lo = _shift_lerp(lo_ref[...], loh_ref[...], mulo_ref[...], first)
    zw, za, zg = lo[:, 0:LANES], lo[:, LANES:2 * LANES], lo[:, 2 * LANES:]

    seg = seg_ref[...]
    segt = segt_ref[...]

    def head_sum(x):
        return _dot_right_exact(_dot_right_exact(x, seg), segt)

    w_raw = w0_ref[...] + _bdot(jnp.tanh(zw), w2_ref[...])
    log_decay = -jnp.exp(-_softplus(-w_raw) - 0.5)
    a_gate = _sigmoid(a0_ref[...] + _bdot(za, a2_ref[...]))
    g = _bdot(_sigmoid(zg), g2_ref[...])
    kk = k * kk_ref[...]
    kk = kk * lax.rsqrt(jnp.maximum(head_sum(kk * kk), 1e-24))
    k_mod = k * (1.0 + (a_gate - 1.0) * ka_ref[...])
    b_vec = kk * a_gate

    cum = _dot_left_exact(tril_ref[...], log_decay)
    cum_last = cum[RWKV_CHUNK - 1:RWKV_CHUNK, :]
    w_last = jnp.exp(cum_last)
    inv_w = jnp.exp(-cum)
    to_end = jnp.exp(cum_last - cum)
    a_til = -kk * jnp.exp(cum - log_decay)
    r_til = r * jnp.exp(cum)
    b_til = b_vec * inv_w
    k_til = k_mod * inv_w
    b_hat = b_vec * to_end
    k_hat = k_mod * to_end

    lane = lax.broadcasted_iota(jnp.int32, (RWKV_CHUNK, LANES), 1)
    m0 = lane < HEAD_DIM
    ri = lax.broadcasted_iota(jnp.int32, (LANES, LANES), 0)
    ci = lax.broadcasted_iota(jnp.int32, (LANES, LANES), 1)
    same = (ri < HEAD_DIM) == (ci < HEAD_DIM)
    mask_sl = same & (ci < ri)
    mask_li = same & (ci <= ri)
    eye = ri == ci

    def stack(x):
        return jnp.concatenate([jnp.where(m0, x, 0.0), jnp.where(m0, 0.0, x)], axis=0)

    nt_dims = (((1,), (1,)), ((), ()))
    y_cols = []
    for p in range(r.shape[1] // LANES):
        cs = slice(p * LANES, (p + 1) * LANES)
        xa_f = stack(a_til[:, cs])
        xr_f = stack(r_til[:, cs])
        xa, xr = xa_f.astype(BF16), xr_f.astype(BF16)
        vs = stack(v[:, cs]).astype(BF16)
        bs_t = stack(b_hat[:, cs]).T.astype(BF16)
        ks_t = stack(k_hat[:, cs]).T.astype(BF16)
        bt, kt = b_til[:, cs], k_til[:, cs]
        ycat = jnp.concatenate([bt, bt, kt, kt], axis=0).astype(BF16)
        m = lax.dot_general(jnp.concatenate([xa, xr], axis=0), ycat, nt_dims, preferred_element_type=F32)
        n1 = jnp.where(mask_sl, m[:LANES, :LANES], 0.0).astype(BF16)
        a_ak = jnp.where(mask_sl, m[:LANES, LANES:], 0.0).astype(BF16)
        m_rb = jnp.where(mask_li, m[LANES:, :LANES], 0.0).astype(BF16)
        m_rk = jnp.where(mask_li, m[LANES:, LANES:], 0.0).astype(BF16)
        z = jnp.concatenate([xa_f, jnp.dot(a_ak, vs, preferred_element_type=F32)], axis=1)
        powers = [n1]
        for _ in range(5):
            powers.append(jnp.dot(powers[-1], powers[-1], preferred_element_type=F32).astype(BF16))
        for nk in reversed(powers):
            z = z + jnp.dot(nk, z.astype(BF16), preferred_element_type=F32)
        zb = z.astype(BF16)
        q = (jnp.concatenate([xr_f, jnp.dot(m_rk, vs, preferred_element_type=F32)], axis=1)
             + jnp.dot(m_rb, zb, preferred_element_type=F32))
        ge = jnp.dot(bs_t, zb, preferred_element_type=F32)
        g_mat = ge[:, :LANES] + jnp.where(eye, w_last[:, cs], 0.0)
        e_mat = ge[:, LANES:] + jnp.dot(ks_t, vs, preferred_element_type=F32)
        h0 = hs_ref[p].astype(BF16)
        qg = jnp.concatenate([q[:, :LANES], g_mat], axis=0).astype(BF16)
        yh = jnp.dot(qg, h0, preferred_element_type=F32)
        y2 = yh[:LANES] + q[:, LANES:]
        hs_ref[p] = yh[LANES:] + e_mat
        y_cols.append(y2[:RWKV_CHUNK] + y2[RWKV_CHUNK:])
    y = jnp.concatenate(y_cols, axis=1)

    inv_hd = 1.0 / HEAD_DIM
    mu = head_sum(y) * inv_hd
    yc = y - mu
    var = head_sum(yc * yc) * inv_hd
    yn = yc * lax.rsqrt(var + RWKV_LN_EPS) * lng_ref[...] + lnb_ref[...]
    bonus = head_sum(r * k_mod * rk_ref[...]) * v
    o_ref[...] = ((yn + bonus) * g).astype(o_ref.dtype)


def _rwkv(z, p, batch, seq, lo_col):
    n = z.shape[0]
    dim = p["w0"].shape[-1]
    t = RWKV_CHUNK
    nt = seq // t
    hpt = t // 8
    lo_w = 4 * LANES
    cur = lambda b, i: b * nt + i
    halo = lambda b, i: jnp.maximum((b * nt + i) * hpt - 1, 0)
    heads = dim // HEAD_DIM
    seg = (jnp.arange(dim)[:, None] // HEAD_DIM == jnp.arange(heads)[None, :]).astype(BF16)
    tril = (jnp.arange(t)[:, None] >= jnp.arange(t)[None, :]).astype(BF16)

    def full(a):
        return pl.BlockSpec(a.shape, lambda b, i: (0,) * a.ndim)

    consts = [p["mu_r"], p["mu_k"], p["mu_v"], p["mu_lo"], p["w0"], p["w2"], p["a0"], p["a2"], p["g2"],
              p["k_k"], p["k_a"], p["r_k"], p["ln_g"], p["ln_b"], seg, seg.T, tril]
    col = lambda c: pl.BlockSpec((t, dim), lambda b, i: (cur(b, i), c))
    hcol = lambda c: pl.BlockSpec((8, dim), lambda b, i: (halo(b, i), c))
    return pl.pallas_call(
        _rwkv_kernel,
        grid=(batch, nt),
        in_specs=[col(0), col(1), col(2), pl.BlockSpec((t, lo_w), lambda b, i: (cur(b, i), lo_col)),
                  hcol(0), hcol(1), hcol(2), pl.BlockSpec((8, lo_w), lambda b, i: (halo(b, i), lo_col))]
                 + [full(a) for a in consts],
        out_specs=pl.BlockSpec((t, dim), lambda b, i: (cur(b, i), 0)),
        out_shape=jax.ShapeDtypeStruct((n, dim), BF16),
        scratch_shapes=[pltpu.VMEM((dim // LANES, LANES, LANES), F32)],
        compiler_params=_cparams(("parallel", "arbitrary")),
        name="rwkv7_chunked",
    )(z, z, z, z, z, z, z, z, *consts)


def _lru_kernel(x_ref, gb_ref, xh_ref, cw_ref, cb_ref, wg_ref, ba_ref, bx_ref, lam_ref, o_ref,
                xbuf, carry):
    t, ch = x_ref.shape
    first = pl.program_id(1) == 0

    @pl.when(first)
    def _():
        carry[...] = jnp.zeros_like(carry)

    xbuf[0:8, :] = jnp.where(first, 0.0, xh_ref[...])
    xbuf[8:, :] = x_ref[...]
    off = 8 - (LRU_CONV_WIDTH - 1)
    xc = jnp.broadcast_to(cb_ref[...], (t, ch))
    for j in range(LRU_CONV_WIDTH):
        xc = xc + cw_ref[j:j + 1, :] * xbuf[off + j:off + j + t, :]
    gr, gi = [], []
    for p in range(ch // LANES):
        gp = _bdot(xc[:, p * LANES:(p + 1) * LANES], wg_ref[p])
        gr.append(gp[:, :LANES])
        gi.append(gp[:, LANES:])
    r = _sigmoid(jnp.concatenate(gr, axis=1) + ba_ref[...])
    i = _sigmoid(jnp.concatenate(gi, axis=1) + bx_ref[...])
    log_a = -LRU_C * r * _softplus(-lam_ref[...])
    a = jnp.exp(log_a)
    u = jnp.sqrt(1.0 - jnp.exp(2.0 * log_a)) * (i * xc)
    row = lax.broadcasted_iota(jnp.int32, (t, ch), 0)
    d = 1
    while d < t:
        keep = row >= d
        a_sh = jnp.where(keep, pltpu.roll(a, d, axis=0), 1.0)
        u_sh = jnp.where(keep, pltpu.roll(u, d, axis=0), 0.0)
        u = a * u_sh + u
        a = a * a_sh
        d *= 2
    h = u + a * carry[...]
    carry[...] = h[t - 1:t, :]
    gb = gb_ref[...]
    gelu = 0.5 * gb * (1.0 + jnp.tanh(0.7978845608028654 * (gb + 0.044715 * (gb * gb * gb))))
    o_ref[...] = (h * gelu).astype(o_ref.dtype)


def _rglru(z, conv_w, conv_b, wa, ba, wx, bx, lam, batch, seq, x_col, g_col, t):
    n = z.shape[0]
    ch = lam.shape[-1]
    nt = seq // t
    hpt = t // 8
    nblk, bd = wa.shape[0], wa.shape[1]
    per = LANES // bd
    def blockdiag(w):
        w = w.reshape(nblk // per, per, bd, bd)
        out = jnp.zeros((nblk // per, LANES, LANES), F32)
        for q in range(per):
            out = out.at[:, q * bd:(q + 1) * bd, q * bd:(q + 1) * bd].set(w[:, q])
        return out
    wg = jnp.concatenate([blockdiag(wa), blockdiag(wx)], axis=-1).astype(BF16)
    cw = jnp.zeros((8, ch), F32).at[:LRU_CONV_WIDTH].set(conv_w.reshape(LRU_CONV_WIDTH, ch))
    cur = lambda b, i: b * nt + i
    halo = lambda b, i: jnp.maximum((b * nt + i) * hpt - 1, 0)
    vec = pl.BlockSpec((1, ch), lambda b, i: (0, 0))
    return pl.pallas_call(
        _lru_kernel,
        grid=(batch, nt),
        in_specs=[pl.BlockSpec((t, ch), lambda b, i: (cur(b, i), x_col)),
                  pl.BlockSpec((t, ch), lambda b, i: (cur(b, i), g_col)),
                  pl.BlockSpec((8, ch), lambda b, i: (halo(b, i), x_col)),
                  pl.BlockSpec((8, ch), lambda b, i: (0, 0)),
                  vec,
                  pl.BlockSpec(wg.shape, lambda b, i: (0, 0, 0)),
                  vec, vec, vec],
        out_specs=pl.BlockSpec((t, ch), lambda b, i: (cur(b, i), 0)),
        out_shape=jax.ShapeDtypeStruct((n, ch), BF16),
        scratch_shapes=[pltpu.VMEM((t + 8, ch), F32), pltpu.VMEM((1, ch), F32)],
        compiler_params=_cparams(("parallel", "arbitrary")),
        name="rglru",
    )(z, z, z, cw, conv_b.reshape(1, ch), wg, ba.reshape(1, ch), bx.reshape(1, ch), lam.reshape(1, ch))


def _moe_kernel(be_ref, nb_ref, x_ref, w1_ref, w3_ref, w2_ref, o_ref):
    i = pl.program_id(0)

    @pl.when(i < nb_ref[0])
    def _():
        x = x_ref[...]
        h1 = jnp.dot(x, w1_ref[0], preferred_element_type=F32)
        h3 = jnp.dot(x, w3_ref[0], preferred_element_type=F32)
        hid = (h1 * _sigmoid(h1) * h3).astype(BF16)
        o_ref[...] = jnp.dot(hid, w2_ref[0], preferred_element_type=F32)

    @pl.when(i >= nb_ref[0])
    def _():
        o_ref[...] = jnp.zeros_like(o_ref)


def _moe_experts(xb, block_e, n_used, w1, w3, w2):
    rows, d = xb.shape
    f = w1.shape[-1]
    n_blocks = rows // MOE_BLOCK
    grid_spec = pltpu.PrefetchScalarGridSpec(
        num_scalar_prefetch=2,
        grid=(n_blocks,),
        in_specs=[pl.BlockSpec((MOE_BLOCK, d), lambda i, be, nb: (i, 0)),
                  pl.BlockSpec((1, d, f), lambda i, be, nb: (be[i], 0, 0)),
                  pl.BlockSpec((1, d, f), lambda i, be, nb: (be[i], 0, 0)),
                  pl.BlockSpec((1, f, d), lambda i, be, nb: (be[i], 0, 0))],
        out_specs=pl.BlockSpec((MOE_BLOCK, d), lambda i, be, nb: (i, 0)),
    )
    return pl.pallas_call(
        _moe_kernel,
        grid_spec=grid_spec,
        out_shape=jax.ShapeDtypeStruct((rows, d), F32),
        compiler_params=_cparams(("arbitrary",)),
        name="moe_experts",
    )(block_e, n_used, xb, w1, w3, w2)


def _moe_ffn(h_bf16, ridx, rgate, w1, w3, w2):
    n, d = h_bf16.shape
    nk = 2 * n
    flat_e = ridx[:2].T.reshape(-1)
    onehot = (flat_e[:, None] == jnp.arange(N_EXPERTS, dtype=jnp.int32)[None, :]).astype(jnp.int32)
    csum = jnp.cumsum(onehot, axis=0)
    rank = jnp.take_along_axis(csum, flat_e[:, None], axis=1)[:, 0] - 1
    counts = csum[-1]
    padded = (counts + MOE_BLOCK - 1) // MOE_BLOCK * MOE_BLOCK
    pend = jnp.cumsum(padded)
    pstart = pend - padded
    dest = pstart[flat_e] + rank
    n_blocks = nk // MOE_BLOCK + N_EXPERTS
    rows = n_blocks * MOE_BLOCK
    flat_tok = jnp.repeat(jnp.arange(n, dtype=jnp.int32), 2)
    row_tok = jnp.zeros((rows,), jnp.int32).at[dest].set(flat_tok)
    block_start = jnp.arange(n_blocks, dtype=jnp.int32) * MOE_BLOCK
    block_e = jnp.minimum(jnp.sum(block_start[:, None] >= pend[None, :], axis=1), N_EXPERTS - 1).astype(jnp.int32)
    n_used = (pend[-1] // MOE_BLOCK).astype(jnp.int32).reshape(1)
    xb = h_bf16[row_tok]
    yb = _moe_experts(xb, block_e, n_used, w1, w3, w2)
    dest2 = dest.reshape(n, 2)
    g = rgate[:2].T
    return yb[dest2[:, 0]] * g[:, 0:1] + yb[dest2[:, 1]] * g[:, 1:2]


def _residual_kernel(x_ref, y_ref, gate_ref, g_ref, o_ref, *, final):
    x1 = x_ref[...] + gate_ref[0] * y_ref[...]
    if final:
        ms = jnp.mean(x1 * x1, axis=-1, keepdims=True)
        x1 = x1 * lax.rsqrt(ms + EPS) * g_ref[...]
    o_ref[...] = x1


def _residual(x, y, gate, g, seq, tm, final):
    n, d = x.shape
    per_seq = seq // tm
    return pl.pallas_call(
        functools.partial(_residual_kernel, final=final),
        grid=(n // tm,),
        in_specs=[pl.BlockSpec((tm, d), lambda i: (i, 0)),
                  pl.BlockSpec((tm, d), lambda i: (i, 0)),
                  pl.BlockSpec((1, 1, d), lambda i: (i // per_seq, 0, 0)),
                  pl.BlockSpec((1, d), lambda i: (0, 0))],
        out_specs=pl.BlockSpec((tm, d), lambda i: (i, 0)),
        out_shape=jax.ShapeDtypeStruct((n, d), F32),
        compiler_params=_cparams(("parallel",)),
        name="ffn_residual",
    )(x, y, gate, g.reshape(1, d))


def _pad_cols(w, width):
    return jnp.pad(w, ((0, 0), (0, width - w.shape[1])))


def _pad_rows(w, height):
    return jnp.pad(w, ((0, height - w.shape[0]), (0, 0)))


def kernel(x, c, positions, router_w, router_bias, ada_w, ada_b, norm_mix, norm_ffn, moe_w1, moe_w3, moe_w2, ab_w_in, ab_sinks, ab_conv_w, ab_conv_b, ab_conv_ln_g, ab_conv_ln_b, ab_w_out, cd_w_in, cd_shift_mu, cd_w0, cd_w2, cd_a0, cd_a2, cd_g2, cd_k_k, cd_k_a, cd_r_k, cd_ln_x_g, cd_ln_x_b, cd_lru_conv_w, cd_lru_conv_b, cd_lru_wa, cd_lru_ba, cd_lru_wx, cd_lru_bx, cd_lru_lambda, cd_w_out, final_norm):
    batch, seq, d = x.shape
    n = batch * seq
    depth = ada_w.shape[0]
    xf = x.reshape(n, d)
    mod = _ada_mod(c, ada_w, ada_b).reshape(depth, batch, 6, 1, d)
    cos_t, sin_t = _rope_tables(positions)
    att_q = ATT_HEADS * HEAD_DIM
    att_kv = ATT_KV_HEADS * HEAD_DIM
    tm = 256

    for layer in range(depth):
        j = layer // 2
        shift_m, scale_m, gate_m, shift_f, scale_f, gate_f = (mod[layer, :, q] for q in range(6))
        if layer % 2 == 0:
            w_in = ab_w_in[j]
            w_perm = jnp.concatenate([w_in[:, :att_q], w_in[:, att_q + 2 * att_kv:],
                                      w_in[:, att_q:att_q + 2 * att_kv]], axis=1).astype(BF16)
            z = _in_proj(xf, norm_mix[layer], shift_m, scale_m, w_perm, seq, 512, 512)
            conv_ch = ab_conv_w.shape[-1]
            kv0 = (att_q + 2 * conv_ch) // att_kv
            a1 = _attention(z, cos_t, sin_t, ab_sinks[j], batch, seq, kv0, kv0 + 1)
            a2 = _conformer_conv(z, ab_conv_w[j], ab_conv_b[j], ab_conv_ln_g[j], ab_conv_ln_b[j],
                                 batch, seq, att_q // conv_ch, att_q // conv_ch + 1, 128)
            w_out = ab_w_out[j]
        else:
            w_in = cd_w_in[j]
            dim = cd_w0.shape[-1]
            dl = cd_w2.shape[1]
            al = cd_a2.shape[1]
            gl = cd_g2.shape[1]
            s0 = 3 * dim
            seg_zw = w_in[:, s0:s0 + dl]
            seg_za = w_in[:, s0 + dl:s0 + dl + al]
            seg_zg = w_in[:, s0 + dl + al:s0 + dl + al + gl]
            s1 = s0 + dl + al + gl
            w_perm = jnp.concatenate([w_in[:, :s0], w_in[:, s1:], _pad_cols(seg_zw, LANES),
                                      _pad_cols(seg_za, LANES), seg_zg], axis=1).astype(BF16)
            mu = cd_shift_mu[j]
            mu_lo = jnp.concatenate([jnp.pad(mu[s0:s0 + dl], (0, LANES - dl)),
                                     jnp.pad(mu[s0 + dl:s0 + dl + al], (0, LANES - al)),
                                     mu[s0 + dl + al:s1]]).reshape(1, -1)
            row = lambda a: a.reshape(1, -1)
            p = dict(mu_r=row(mu[:dim]), mu_k=row(mu[dim:2 * dim]), mu_v=row(mu[2 * dim:s0]), mu_lo=mu_lo,
                     w0=row(cd_w0[j]), w2=_pad_rows(cd_w2[j], LANES).astype(BF16),
                     a0=row(cd_a0[j]), a2=_pad_rows(cd_a2[j], LANES).astype(BF16), g2=cd_g2[j].astype(BF16),
                     k_k=row(cd_k_k[j]), k_a=row(cd_k_a[j]), r_k=row(cd_r_k[j]),
                     ln_g=row(cd_ln_x_g[j]), ln_b=row(cd_ln_x_b[j]))
            z = _in_proj(xf, norm_mix[layer], shift_m, scale_m, w_perm, seq, 512, 512)
            lo_col = (5 * dim) // (4 * LANES)
            a1 = _rwkv(z, p, batch, seq, lo_col)
            a2 = _rglru(z, cd_lru_conv_w[j], cd_lru_conv_b[j], cd_lru_wa[j], cd_lru_ba[j],
                        cd_lru_wx[j], cd_lru_bx[j], cd_lru_lambda[j], batch, seq, 3, 4, 256)
            w_out = cd_w_out[j]
        x1, h, ridx, rgate = _out_proj(a1, a2, w_out, xf, gate_m, norm_ffn[layer], shift_f, scale_f,
                                       router_w, router_bias, seq, tm)
        y = _moe_ffn(h, ridx, rgate, moe_w1[layer].astype(BF16), moe_w3[layer].astype(BF16),
                     moe_w2[layer].astype(BF16))
        xf = _residual(x1, y, gate_f, final_norm, seq, 512, layer == depth - 1)
    return xf.reshape(batch, seq, d)
```

```python
import functools

import jax
import jax.numpy as jnp
from jax import lax
from jax.experimental import pallas as pl
from jax.experimental.pallas import tpu as pltpu

F32 = jnp.float32
BF16 = jnp.bfloat16

EPS = 1e-6
LN_EPS = 1e-5
RWKV_LN_EPS = 64e-5
HEAD_DIM = 64
LANES = 128
ATT_BLOCK = 128
WINDOW = 128
ATT_HEADS = 16
ATT_KV_HEADS = 4
CONV_WIDTH = 31
CONV_HALO = 32
LRU_CONV_WIDTH = 4
LRU_C = 8.0
N_EXPERTS = 16
N_GROUPS = 4
EXPERTS_PER_GROUP = 4
MOE_BLOCK = 256
RWKV_CHUNK = 64
ROPE_THETA = 10000.0
VMEM_LIMIT = 56 * 1024 * 1024


def _cparams(sem):
    return pltpu.CompilerParams(dimension_semantics=sem, vmem_limit_bytes=VMEM_LIMIT)


def _bdot(a, b):
    return jnp.dot(a.astype(BF16), b.astype(BF16), preferred_element_type=F32)


def _split3(x):
    hi = x.astype(BF16)
    r1 = x - hi.astype(F32)
    mid = r1.astype(BF16)
    lo = (r1 - mid.astype(F32)).astype(BF16)
    return hi, mid, lo


def _dot_left_exact(m_bf16, x):
    hi, mid, lo = _split3(x)
    d = functools.partial(jnp.dot, preferred_element_type=F32)
    return d(m_bf16, hi) + d(m_bf16, mid) + d(m_bf16, lo)


def _dot_right_exact(x, m_bf16):
    hi, mid, lo = _split3(x)
    d = functools.partial(jnp.dot, preferred_element_type=F32)
    return d(hi, m_bf16) + d(mid, m_bf16) + d(lo, m_bf16)


def _dot_3pass(a, b):
    a_hi = a.astype(BF16)
    a_lo = (a - a_hi.astype(F32)).astype(BF16)
    b_hi = b.astype(BF16)
    b_lo = (b - b_hi.astype(F32)).astype(BF16)
    d = functools.partial(jnp.dot, preferred_element_type=F32)
    return d(a_hi, b_hi) + d(a_hi, b_lo) + d(a_lo, b_hi)


def _sigmoid(x):
    return 1.0 / (1.0 + jnp.exp(-x))


def _softplus(x):
    return jnp.maximum(x, 0.0) + jnp.log(1.0 + jnp.exp(-jnp.abs(x)))


def _rms_mod(x, g, shift, scale):
    ms = jnp.mean(x * x, axis=-1, keepdims=True)
    return (x * lax.rsqrt(ms + EPS) * g) * (1.0 + scale) + shift


def _ada_kernel(c_ref, w_ref, b_ref, o_ref):
    c = c_ref[...]
    o_ref[0] = _dot_3pass(c * _sigmoid(c), w_ref[0]) + b_ref[0]


def _ada_mod(c, ada_w, ada_b):
    depth, d, n6 = ada_w.shape
    b = c.shape[0]
    bp = 8
    tn = 768
    c_pad = jnp.zeros((bp, d), F32).at[:b].set(c)
    out = pl.pallas_call(
        _ada_kernel,
        grid=(depth, n6 // tn),
        in_specs=[pl.BlockSpec((bp, d), lambda l, j: (0, 0)),
                  pl.BlockSpec((1, d, tn), lambda l, j: (l, 0, j)),
                  pl.BlockSpec((1, 1, tn), lambda l, j: (l, 0, j))],
        out_specs=pl.BlockSpec((1, bp, tn), lambda l, j: (l, 0, j)),
        out_shape=jax.ShapeDtypeStruct((depth, bp, n6), F32),
        compiler_params=_cparams(("parallel", "parallel")),
        name="ada_mod",
    )(c_pad, ada_w, ada_b.reshape(depth, 1, n6))
    return out[:, :b]


def _rope_kernel(pos_ref, f_ref, cos_ref, sin_ref):
    ang = pos_ref[...].astype(F32) * f_ref[...]
    lane = lax.broadcasted_iota(jnp.int32, ang.shape, 1)
    s = jnp.sin(ang)
    cos_ref[...] = jnp.cos(ang)
    sin_ref[...] = jnp.where((lane & 32) == 0, -s, s)


def _rope_tables(positions):
    n = positions.size
    half = HEAD_DIM // 2
    inv_freq = ROPE_THETA ** (-jnp.arange(half, dtype=F32) / half)
    f_row = jnp.tile(inv_freq, LANES // half).reshape(1, LANES)
    tm = 512
    return pl.pallas_call(
        _rope_kernel,
        grid=(n // tm,),
        in_specs=[pl.BlockSpec((tm, 1), lambda i: (i, 0)),
                  pl.BlockSpec((1, LANES), lambda i: (0, 0))],
        out_specs=[pl.BlockSpec((tm, LANES), lambda i: (i, 0))] * 2,
        out_shape=[jax.ShapeDtypeStruct((n, LANES), F32)] * 2,
        compiler_params=_cparams(("parallel",)),
        name="rope_tables",
    )(positions.reshape(n, 1), f_row)


def _in_proj_kernel(x_ref, g_ref, sh_ref, sc_ref, w_ref, o_ref, *, tn):
    h = _rms_mod(x_ref[...], g_ref[...], sh_ref[0], sc_ref[0]).astype(BF16)
    for c in range(o_ref.shape[1] // tn):
        o_ref[:, c * tn:(c + 1) * tn] = jnp.dot(h, w_ref[:, c * tn:(c + 1) * tn], preferred_element_type=F32)


def _in_proj(x, g, shift, scale, w_bf16, seq, tm, tn):
    n, d = x.shape
    nout = w_bf16.shape[1]
    per_seq = seq // tm
    return pl.pallas_call(
        functools.partial(_in_proj_kernel, tn=tn),
        grid=(n // tm,),
        in_specs=[pl.BlockSpec((tm, d), lambda i: (i, 0)),
                  pl.BlockSpec((1, d), lambda i: (0, 0)),
                  pl.BlockSpec((1, 1, d), lambda i: (i // per_seq, 0, 0)),
                  pl.BlockSpec((1, 1, d), lambda i: (i // per_seq, 0, 0)),
                  pl.BlockSpec((d, nout), lambda i: (0, 0), pipeline_mode=pl.Buffered(1))],
        out_specs=pl.BlockSpec((tm, nout), lambda i: (i, 0)),
        out_shape=jax.ShapeDtypeStruct((n, nout), F32),
        compiler_params=_cparams(("parallel",)),
        name="in_proj",
    )(x, g.reshape(1, d), shift, scale, w_bf16)


def _route(logits_t, bias_col):
    m = jnp.max(logits_t, axis=0, keepdims=True)
    e = jnp.exp(logits_t - m)
    probs = e / jnp.sum(e, axis=0, keepdims=True)
    sel = probs + bias_col
    s = [sel[i:i + 1, :] for i in range(N_EXPERTS)]
    pr = [probs[i:i + 1, :] for i in range(N_EXPERTS)]

    def top2_sum(v):
        best = v[0] + v[1]
        for i in range(len(v)):
            for j in range(i + 1, len(v)):
                if (i, j) != (0, 1):
                    best = jnp.maximum(best, v[i] + v[j])
        return best

    gs = [top2_sum(s[4 * g:4 * g + 4]) for g in range(N_GROUPS)]
    best = gs[0]
    gi = jnp.zeros_like(best, dtype=jnp.int32)
    for g in range(1, N_GROUPS):
        upd = gs[g] > best
        gi = jnp.where(upd, g, gi)
        best = jnp.where(upd, gs[g], best)

    def pick_group(rows):
        out = []
        for l in range(EXPERTS_PER_GROUP):
            v = rows[l]
            for g in range(1, N_GROUPS):
                v = jnp.where(gi == g, rows[4 * g + l], v)
            out.append(v)
        return out

    ig = pick_group(s)
    pg = pick_group(pr)

    def argmax4(v):
        bv = v[0]
        bi = jnp.zeros_like(gi)
        for l in range(1, EXPERTS_PER_GROUP):
            upd = v[l] > bv
            bi = jnp.where(upd, l, bi)
            bv = jnp.where(upd, v[l], bv)
        return bi

    i1 = argmax4(ig)
    i2 = argmax4([jnp.where(i1 == l, -jnp.inf, ig[l]) for l in range(EXPERTS_PER_GROUP)])

    def pick(v, idx):
        o = v[0]
        for l in range(1, EXPERTS_PER_GROUP):
            o = jnp.where(idx == l, v[l], o)
        return o

    p1 = pick(pg, i1)
    p2 = pick(pg, i2)
    tot = p1 + p2
    return gi * EXPERTS_PER_GROUP + i1, gi * EXPERTS_PER_GROUP + i2, p1 / tot, p2 / tot


def _out_proj_kernel(a1_ref, a2_ref, w1_ref, w2_ref, x_ref, gate_ref, g_ref, sh_ref, sc_ref,
                     rw_ref, rb_ref, xo_ref, h_ref, ridx_ref, rgate_ref):
    y = (jnp.dot(a1_ref[...], w1_ref[...], preferred_element_type=F32)
         + jnp.dot(a2_ref[...], w2_ref[...], preferred_element_type=F32))
    x1 = x_ref[...] + gate_ref[0] * y
    xo_ref[...] = x1
    h = _rms_mod(x1, g_ref[...], sh_ref[0], sc_ref[0])
    h_ref[...] = h.astype(BF16)
    logits = _dot_3pass(h, rw_ref[...])
    logits_t = logits.T[:N_EXPERTS, :]
    e1, e2, g1, g2 = _route(logits_t, rb_ref[...])
    row = lax.broadcasted_iota(jnp.int32, ridx_ref.shape, 0)
    ridx_ref[...] = jnp.where(row == 0, e1, jnp.where(row == 1, e2, 0))
    rgate_ref[...] = jnp.where(row == 0, g1, jnp.where(row == 1, g2, 0.0))


def _out_proj(a1, a2, w_out, x, gate, g, shift, scale, router_w, router_bias, seq, tm):
    n, d = x.shape
    ka = a1.shape[1]
    per_seq = seq // tm
    w1 = w_out[:ka].astype(BF16)
    w2 = w_out[ka:].astype(BF16)
    rw = jnp.zeros((d, LANES), F32).at[:, :N_EXPERTS].set(router_w)
    rb = router_bias.astype(F32).reshape(N_EXPERTS, 1)
    mod_spec = pl.BlockSpec((1, 1, d), lambda i: (i // per_seq, 0, 0))
    return pl.pallas_call(
        _out_proj_kernel,
        grid=(n // tm,),
        in_specs=[pl.BlockSpec((tm, ka), lambda i: (i, 0)),
                  pl.BlockSpec((tm, ka), lambda i: (i, 0)),
                  pl.BlockSpec((ka, d), lambda i: (0, 0), pipeline_mode=pl.Buffered(1)),
                  pl.BlockSpec((ka, d), lambda i: (0, 0), pipeline_mode=pl.Buffered(1)),
                  pl.BlockSpec((tm, d), lambda i: (i, 0)),
                  mod_spec,
                  pl.BlockSpec((1, d), lambda i: (0, 0)),
                  mod_spec, mod_spec,
                  pl.BlockSpec((d, LANES), lambda i: (0, 0)),
                  pl.BlockSpec((N_EXPERTS, 1), lambda i: (0, 0))],
        out_specs=[pl.BlockSpec((tm, d), lambda i: (i, 0)),
                   pl.BlockSpec((tm, d), lambda i: (i, 0)),
                   pl.BlockSpec((8, tm), lambda i: (0, i)),
                   pl.BlockSpec((8, tm), lambda i: (0, i))],
        out_shape=[jax.ShapeDtypeStruct((n, d), F32),
                   jax.ShapeDtypeStruct((n, d), BF16),
                   jax.ShapeDtypeStruct((8, n), jnp.int32),
                   jax.ShapeDtypeStruct((8, n), F32)],
        compiler_params=_cparams(("parallel",)),
        name="out_proj_route",
    )(a1, a2, w1, w2, x, gate, g.reshape(1, d), shift, scale, rw, rb)


def _rope_apply(x, cos, sin_signed):
    outs = []
    lane = lax.broadcasted_iota(jnp.int32, cos.shape, 1)
    first_half = (lane & 32) == 0
    for c in range(x.shape[1] // LANES):
        xc = x[:, c * LANES:(c + 1) * LANES]
        partner = jnp.where(first_half, pltpu.roll(xc, LANES - 32, axis=1), pltpu.roll(xc, 32, axis=1))
        outs.append(xc * cos + partner * sin_signed)
    return outs


def _attn_kernel(sink_ref, q_ref, kp_ref, kc_ref, vp_ref, vc_ref, cq_ref, sq_ref, cp_ref, sp_ref, o_ref):
    has_prev = pl.program_id(1) > 0
    cq, sq = cq_ref[...], sq_ref[...]
    q_cols = _rope_apply(q_ref[...] * (HEAD_DIM ** -0.5), cq, sq)
    k_cols = [jnp.concatenate([kp, kc], axis=0) for kp, kc in
              zip(_rope_apply(kp_ref[...], cp_ref[...], sp_ref[...]), _rope_apply(kc_ref[...], cq, sq))]
    v_all = jnp.concatenate([vp_ref[...], vc_ref[...]], axis=0).astype(BF16)
    qi = lax.broadcasted_iota(jnp.int32, (ATT_BLOCK, 2 * ATT_BLOCK), 0)
    kj = lax.broadcasted_iota(jnp.int32, (ATT_BLOCK, 2 * ATT_BLOCK), 1)
    valid = ((kj > qi) & (kj < ATT_BLOCK) & has_prev) | ((kj >= ATT_BLOCK) & (kj - ATT_BLOCK <= qi))
    group = ATT_HEADS // ATT_KV_HEADS
    out_cols = []
    for pair in range(ATT_HEADS // 2):
        halves = []
        for sub in range(2):
            hq = 2 * pair + sub
            hk = hq // group
            qh = q_cols[pair][:, sub * HEAD_DIM:(sub + 1) * HEAD_DIM].astype(BF16)
            kh = k_cols[hk // 2][:, (hk % 2) * HEAD_DIM:(hk % 2 + 1) * HEAD_DIM].astype(BF16)
            vh = v_all[:, hk * HEAD_DIM:(hk + 1) * HEAD_DIM]
            s = lax.dot_general(qh, kh, (((1,), (1,)), ((), ())), preferred_element_type=F32)
            s = jnp.where(valid, s, -1e30)
            sink = sink_ref[hq]
            m = jnp.maximum(jnp.max(s, axis=-1, keepdims=True), sink)
            p = jnp.exp(s - m)
            den = jnp.sum(p, axis=-1, keepdims=True) + jnp.exp(sink - m)
            halves.append(jnp.dot(p.astype(BF16), vh, preferred_element_type=F32) / den)
        out_cols.append(jnp.concatenate(halves, axis=1))
    o_ref[...] = jnp.concatenate(out_cols, axis=1).astype(o_ref.dtype)


def _attention(z, cos_t, sin_t, sinks, batch, seq, k_col, v_col):
    n = z.shape[0]
    nb = seq // ATT_BLOCK
    att_q = ATT_HEADS * HEAD_DIM
    att_kv = ATT_KV_HEADS * HEAD_DIM
    cur = lambda b, i: b * nb + i
    prev = lambda b, i: b * nb + jnp.maximum(i - 1, 0)
    return pl.pallas_call(
        _attn_kernel,
        grid=(batch, nb),
        in_specs=[pl.BlockSpec(memory_space=pltpu.SMEM),
                  pl.BlockSpec((ATT_BLOCK, att_q), lambda b, i: (cur(b, i), 0)),
                  pl.BlockSpec((ATT_BLOCK, att_kv), lambda b, i: (prev(b, i), k_col)),
                  pl.BlockSpec((ATT_BLOCK, att_kv), lambda b, i: (cur(b, i), k_col)),
                  pl.BlockSpec((ATT_BLOCK, att_kv), lambda b, i: (prev(b, i), v_col)),
                  pl.BlockSpec((ATT_BLOCK, att_kv), lambda b, i: (cur(b, i), v_col)),
                  pl.BlockSpec((ATT_BLOCK, LANES), lambda b, i: (cur(b, i), 0)),
                  pl.BlockSpec((ATT_BLOCK, LANES), lambda b, i: (cur(b, i), 0)),
                  pl.BlockSpec((ATT_BLOCK, LANES), lambda b, i: (prev(b, i), 0)),
                  pl.BlockSpec((ATT_BLOCK, LANES), lambda b, i: (prev(b, i), 0))],
        out_specs=pl.BlockSpec((ATT_BLOCK, att_q), lambda b, i: (cur(b, i), 0)),
        out_shape=jax.ShapeDtypeStruct((n, att_q), BF16),
        compiler_params=_cparams(("parallel", "parallel")),
        name="swa_attention",
    )(sinks.astype(F32), z, z, z, z, z, cos_t, sin_t, cos_t, sin_t)


def _conv_kernel(val_ref, gate_ref, hval_ref, hgate_ref, w_ref, b_ref, g_ref, beta_ref, o_ref, ybuf, acc):
    t = val_ref.shape[0]
    ch = val_ref.shape[1]
    first = pl.program_id(1) == 0
    yh = hval_ref[...] * _sigmoid(hgate_ref[...])
    ybuf[0:CONV_HALO, :] = jnp.where(first, 0.0, yh)
    ybuf[CONV_HALO:, :] = val_ref[...] * _sigmoid(gate_ref[...])
    rt = 64
    off = CONV_HALO - (CONV_WIDTH - 1)
    for cb in range(ch // LANES):
        cs = slice(cb * LANES, (cb + 1) * LANES)
        for r0 in range(0, t, rt):
            a = jnp.broadcast_to(b_ref[:, cs], (rt, LANES))
            for j in range(CONV_WIDTH):
                a = a + w_ref[j:j + 1, cs] * ybuf[r0 + off + j:r0 + off + j + rt, cs]
            acc[r0:r0 + rt, cs] = a
    y = acc[...]
    mu = jnp.mean(y, axis=-1, keepdims=True)
    yc = y - mu
    var = jnp.mean(yc * yc, axis=-1, keepdims=True)
    ln = yc * lax.rsqrt(var + LN_EPS) * g_ref[...] + beta_ref[...]
    o_ref[...] = (ln * _sigmoid(ln)).astype(o_ref.dtype)


def _conformer_conv(z, conv_w, conv_b, ln_g, ln_b, batch, seq, val_col, gate_col, t):
    n = z.shape[0]
    ch = conv_w.shape[-1]
    nt = seq // t
    hpt = t // CONV_HALO
    w = jnp.zeros((CONV_HALO, ch), F32).at[:CONV_WIDTH].set(conv_w.reshape(CONV_WIDTH, ch))
    cur = lambda b, i: b * nt + i
    halo = lambda b, i: jnp.maximum((b * nt + i) * hpt - 1, 0)
    vec = pl.BlockSpec((1, ch), lambda b, i: (0, 0))
    return pl.pallas_call(
        _conv_kernel,
        grid=(batch, nt),
        in_specs=[pl.BlockSpec((t, ch), lambda b, i: (cur(b, i), val_col)),
                  pl.BlockSpec((t, ch), lambda b, i: (cur(b, i), gate_col)),
                  pl.BlockSpec((CONV_HALO, ch), lambda b, i: (halo(b, i), val_col)),
                  pl.BlockSpec((CONV_HALO, ch), lambda b, i: (halo(b, i), gate_col)),
                  pl.BlockSpec((CONV_HALO, ch), lambda b, i: (0, 0)),
                  vec, vec, vec],
        out_specs=pl.BlockSpec((t, ch), lambda b, i: (cur(b, i), 0)),
        out_shape=jax.ShapeDtypeStruct((n, ch), BF16),
        scratch_shapes=[pltpu.VMEM((t + CONV_HALO, ch), F32), pltpu.VMEM((t, ch), F32)],
        compiler_params=_cparams(("parallel", "parallel")),
        name="conformer_conv",
    )(z, z, z, z, w, conv_b.reshape(1, ch), ln_g.reshape(1, ch), ln_b.reshape(1, ch))


def _shift_lerp(z, halo, mu, first):
    prev = pltpu.roll(z, 1, axis=0)
    row = lax.broadcasted_iota(jnp.int32, z.shape, 0)
    hrow = jnp.where(first, 0.0, halo[7:8, :])
    prev = jnp.where(row == 0, hrow, prev)
    return z + mu * (prev - z)


def _rwkv_kernel(r_ref, k_ref, v_ref, lo_ref, rh_ref, kh_ref, vh_ref, loh_ref,
                 mur_ref, muk_ref, muv_ref, mulo_ref, w0_ref, w2_ref, a0_ref, a2_ref, g2_ref,
                 kk_ref, ka_ref, rk_ref, lng_ref, lnb_ref, seg_ref, segt_ref, tril_ref,
                 o_ref, hs_ref):
    first = pl.program_id(1) == 0

    @pl.when(first)
    def _():
        hs_ref[...] = jnp.zeros_like(hs_ref)

    r = _shift_lerp(r_ref[...], rh_ref[...], mur_ref[...], first)
    k = _shift_lerp(k_ref[...], kh_ref[...], muk_ref[...], first)
    v = _shift_lerp(v_ref[...], vh_ref[...], muv_ref[...], first)
    lo = _shift_lerp(lo_ref[...], loh_ref[...], mulo_ref[...], first)
    zw, za, zg = lo[:, 0:LANES], lo[:, LANES:2 * LANES], lo[:, 2 * LANES:]

    seg = seg_ref[...]
    segt = segt_ref[...]

    def head_sum(x):
        return _dot_right_exact(_dot_right_exact(x, seg), segt)

    w_raw = w0_ref[...] + _bdot(jnp.tanh(zw), w2_ref[...])
    log_decay = -jnp.exp(-_softplus(-w_raw) - 0.5)
    a_gate = _sigmoid(a0_ref[...] + _bdot(za, a2_ref[...]))
    g = _bdot(_sigmoid(zg), g2_ref[...])
    kk = k * kk_ref[...]
    kk = kk * lax.rsqrt(jnp.maximum(head_sum(kk * kk), 1e-24))
    k_mod = k * (1.0 + (a_gate - 1.0) * ka_ref[...])
    b_vec = kk * a_gate

    cum = _dot_left_exact(tril_ref[...], log_decay)
    cum_last = cum[RWKV_CHUNK - 1:RWKV_CHUNK, :]
    w_last = jnp.exp(cum_last)
    inv_w = jnp.exp(-cum)
    to_end = jnp.exp(cum_last - cum)
    a_til = -kk * jnp.exp(cum - log_decay)
    r_til = r * jnp.exp(cum)
    b_til = b_vec * inv_w
    k_til = k_mod * inv_w
    b_hat = b_vec * to_end
    k_hat = k_mod * to_end

    lane = lax.broadcasted_iota(jnp.int32, (RWKV_CHUNK, LANES), 1)
    m0 = lane < HEAD_DIM
    ri = lax.broadcasted_iota(jnp.int32, (LANES, LANES), 0)
    ci = lax.broadcasted_iota(jnp.int32, (LANES, LANES), 1)
    same = (ri < HEAD_DIM) == (ci < HEAD_DIM)
    mask_sl = same & (ci < ri)
    mask_li = same & (ci <= ri)
    eye = ri == ci

    def stack(x):
        return jnp.concatenate([jnp.where(m0, x, 0.0), jnp.where(m0, 0.0, x)], axis=0)

    nt_dims = (((1,), (1,)), ((), ()))
    dot = functools.partial(jnp.dot, preferred_element_type=F32)
    pairs = range(r.shape[1] // LANES)
    cols = [slice(p * LANES, (p + 1) * LANES) for p in pairs]
    xa_f = [stack(a_til[:, cs]) for cs in cols]
    xr_f = [stack(r_til[:, cs]) for cs in cols]
    vs = [stack(v[:, cs]).astype(BF16) for cs in cols]
    m = [lax.dot_general(jnp.concatenate([xa_f[p], xr_f[p]], axis=0).astype(BF16),
                         jnp.concatenate([b_til[:, cols[p]]] * 2 + [k_til[:, cols[p]]] * 2, axis=0).astype(BF16),
                         nt_dims, preferred_element_type=F32) for p in pairs]
    power = [jnp.where(mask_sl, m[p][:LANES, :LANES], 0.0).astype(BF16) for p in pairs]
    a_ak = [jnp.where(mask_sl, m[p][:LANES, LANES:], 0.0).astype(BF16) for p in pairs]
    z = [jnp.concatenate([xa_f[p], dot(a_ak[p], vs[p])], axis=1) for p in pairs]
    for level in range(6):
        z = [z[p] + dot(power[p], z[p].astype(BF16)) for p in pairs]
        if level < 5:
            power = [dot(power[p], power[p]).astype(BF16) for p in pairs]
    zb = [z[p].astype(BF16) for p in pairs]
    m_rb = [jnp.where(mask_li, m[p][LANES:, :LANES], 0.0).astype(BF16) for p in pairs]
    m_rk = [jnp.where(mask_li, m[p][LANES:, LANES:], 0.0).astype(BF16) for p in pairs]
    q = [jnp.concatenate([xr_f[p], dot(m_rk[p], vs[p])], axis=1) + dot(m_rb[p], zb[p]) for p in pairs]
    ge = [dot(stack(b_hat[:, cols[p]]).T.astype(BF16), zb[p]) for p in pairs]
    e_mat = [ge[p][:, LANES:] + dot(stack(k_hat[:, cols[p]]).T.astype(BF16), vs[p]) for p in pairs]
    qg = [jnp.concatenate([q[p][:, :LANES], ge[p][:, :LANES] + jnp.where(eye, w_last[:, cols[p]], 0.0)],
                          axis=0).astype(BF16) for p in pairs]
    yh = [dot(qg[p], hs_ref[p].astype(BF16)) for p in pairs]
    y_cols = []
    for p in pairs:
        hs_ref[p] = yh[p][LANES:] + e_mat[p]
        y2 = yh[p][:LANES] + q[p][:, LANES:]
        y_cols.append(y2[:RWKV_CHUNK] + y2[RWKV_CHUNK:])
    y = jnp.concatenate(y_cols, axis=1)

    inv_hd = 1.0 / HEAD_DIM
    mu = head_sum(y) * inv_hd
    yc = y - mu
    var = head_sum(yc * yc) * inv_hd
    yn = yc * lax.rsqrt(var + RWKV_LN_EPS) * lng_ref[...] + lnb_ref[...]
    bonus = head_sum(r * k_mod * rk_ref[...]) * v
    o_ref[...] = ((yn + bonus) * g).astype(o_ref.dtype)


def _rwkv(z, p, batch, seq, lo_col):
    n = z.shape[0]
    dim = p["w0"].shape[-1]
    t = RWKV_CHUNK
    nt = seq // t
    hpt = t // 8
    lo_w = 4 * LANES
    cur = lambda b, i: b * nt + i
    halo = lambda b, i: jnp.maximum((b * nt + i) * hpt - 1, 0)
    heads = dim // HEAD_DIM
    seg = (jnp.arange(dim)[:, None] // HEAD_DIM == jnp.arange(heads)[None, :]).astype(BF16)
    tril = (jnp.arange(t)[:, None] >= jnp.arange(t)[None, :]).astype(BF16)

    def full(a):
        return pl.BlockSpec(a.shape, lambda b, i: (0,) * a.ndim)

    consts = [p["mu_r"], p["mu_k"], p["mu_v"], p["mu_lo"], p["w0"], p["w2"], p["a0"], p["a2"], p["g2"],
              p["k_k"], p["k_a"], p["r_k"], p["ln_g"], p["ln_b"], seg, seg.T, tril]
    col = lambda c: pl.BlockSpec((t, dim), lambda b, i: (cur(b, i), c))
    hcol = lambda c: pl.BlockSpec((8, dim), lambda b, i: (halo(b, i), c))
    return pl.pallas_call(
        _rwkv_kernel,
        grid=(batch, nt),
        in_specs=[col(0), col(1), col(2), pl.BlockSpec((t, lo_w), lambda b, i: (cur(b, i), lo_col)),
                  hcol(0), hcol(1), hcol(2), pl.BlockSpec((8, lo_w), lambda b, i: (halo(b, i), lo_col))]
                 + [full(a) for a in consts],
        out_specs=pl.BlockSpec((t, dim), lambda b, i: (cur(b, i), 0)),
        out_shape=jax.ShapeDtypeStruct((n, dim), BF16),
        scratch_shapes=[pltpu.VMEM((dim // LANES, LANES, LANES), F32)],
        compiler_params=_cparams(("parallel", "arbitrary")),
        name="rwkv7_chunked",
    )(z, z, z, z, z, z, z, z, *consts)


def _lru_kernel(x_ref, gb_ref, xh_ref, cw_ref, cb_ref, wg_ref, ba_ref, bx_ref, lam_ref, o_ref,
                xbuf, carry):
    t, ch = x_ref.shape
    first = pl.program_id(1) == 0

    @pl.when(first)
    def _():
        carry[...] = jnp.zeros_like(carry)

    xbuf[0:8, :] = jnp.where(first, 0.0, xh_ref[...])
    xbuf[8:, :] = x_ref[...]
    off = 8 - (LRU_CONV_WIDTH - 1)
    xc = jnp.broadcast_to(cb_ref[...], (t, ch))
    for j in range(LRU_CONV_WIDTH):
        xc = xc + cw_ref[j:j + 1, :] * xbuf[off + j:off + j + t, :]
    gr, gi = [], []
    for p in range(ch // LANES):
        gp = _bdot(xc[:, p * LANES:(p + 1) * LANES], wg_ref[p])
        gr.append(gp[:, :LANES])
        gi.append(gp[:, LANES:])
    r = _sigmoid(jnp.concatenate(gr, axis=1) + ba_ref[...])
    i = _sigmoid(jnp.concatenate(gi, axis=1) + bx_ref[...])
    log_a = -LRU_C * r * _softplus(-lam_ref[...])
    a = jnp.exp(log_a)
    u = jnp.sqrt(1.0 - jnp.exp(2.0 * log_a)) * (i * xc)
    row = lax.broadcasted_iota(jnp.int32, (t, ch), 0)
    d = 1
    while d < t:
        keep = row >= d
        a_sh = jnp.where(keep, pltpu.roll(a, d, axis=0), 1.0)
        u_sh = jnp.where(keep, pltpu.roll(u, d, axis=0), 0.0)
        u = a * u_sh + u
        a = a * a_sh
        d *= 2
    h = u + a * carry[...]
    carry[...] = h[t - 1:t, :]
    gb = gb_ref[...]
    gelu = 0.5 * gb * (1.0 + jnp.tanh(0.7978845608028654 * (gb + 0.044715 * (gb * gb * gb))))
    o_ref[...] = (h * gelu).astype(o_ref.dtype)


def _rglru(z, conv_w, conv_b, wa, ba, wx, bx, lam, batch, seq, x_col, g_col, t):
    n = z.shape[0]
    ch = lam.shape[-1]
    nt = seq // t
    hpt = t // 8
    nblk, bd = wa.shape[0], wa.shape[1]
    per = LANES // bd
    def blockdiag(w):
        w = w.reshape(nblk // per, per, bd, bd)
        out = jnp.zeros((nblk // per, LANES, LANES), F32)
        for q in range(per):
            out = out.at[:, q * bd:(q + 1) * bd, q * bd:(q + 1) * bd].set(w[:, q])
        return out
    wg = jnp.concatenate([blockdiag(wa), blockdiag(wx)], axis=-1).astype(BF16)
    cw = jnp.zeros((8, ch), F32).at[:LRU_CONV_WIDTH].set(conv_w.reshape(LRU_CONV_WIDTH, ch))
    cur = lambda b, i: b * nt + i
    halo = lambda b, i: jnp.maximum((b * nt + i) * hpt - 1, 0)
    vec = pl.BlockSpec((1, ch), lambda b, i: (0, 0))
    return pl.pallas_call(
        _lru_kernel,
        grid=(batch, nt),
        in_specs=[pl.BlockSpec((t, ch), lambda b, i: (cur(b, i), x_col)),
                  pl.BlockSpec((t, ch), lambda b, i: (cur(b, i), g_col)),
                  pl.BlockSpec((8, ch), lambda b, i: (halo(b, i), x_col)),
                  pl.BlockSpec((8, ch), lambda b, i: (0, 0)),
                  vec,
                  pl.BlockSpec(wg.shape, lambda b, i: (0, 0, 0)),
                  vec, vec, vec],
        out_specs=pl.BlockSpec((t, ch), lambda b, i: (cur(b, i), 0)),
        out_shape=jax.ShapeDtypeStruct((n, ch), BF16),
        scratch_shapes=[pltpu.VMEM((t + 8, ch), F32), pltpu.VMEM((1, ch), F32)],
        compiler_params=_cparams(("parallel", "arbitrary")),
        name="rglru",
    )(z, z, z, cw, conv_b.reshape(1, ch), wg, ba.reshape(1, ch), bx.reshape(1, ch), lam.reshape(1, ch))


def _moe_kernel(be_ref, nb_ref, x_ref, w1_ref, w3_ref, w2_ref, o_ref):
    i = pl.program_id(0)

    @pl.when(i < nb_ref[0])
    def _():
        x = x_ref[...]
        h1 = jnp.dot(x, w1_ref[0], preferred_element_type=F32)
        h3 = jnp.dot(x, w3_ref[0], preferred_element_type=F32)
        hid = (h1 * _sigmoid(h1) * h3).astype(BF16)
        o_ref[...] = jnp.dot(hid, w2_ref[0], preferred_element_type=F32)

    @pl.when(i >= nb_ref[0])
    def _():
        o_ref[...] = jnp.zeros_like(o_ref)


def _moe_experts(xb, block_e, n_used, w1, w3, w2):
    rows, d = xb.shape
    f = w1.shape[-1]
    n_blocks = rows // MOE_BLOCK
    grid_spec = pltpu.PrefetchScalarGridSpec(
        num_scalar_prefetch=2,
        grid=(n_blocks,),
        in_specs=[pl.BlockSpec((MOE_BLOCK, d), lambda i, be, nb: (i, 0)),
                  pl.BlockSpec((1, d, f), lambda i, be, nb: (be[i], 0, 0)),
                  pl.BlockSpec((1, d, f), lambda i, be, nb: (be[i], 0, 0)),
                  pl.BlockSpec((1, f, d), lambda i, be, nb: (be[i], 0, 0))],
        out_specs=pl.BlockSpec((MOE_BLOCK, d), lambda i, be, nb: (i, 0)),
    )
    return pl.pallas_call(
        _moe_kernel,
        grid_spec=grid_spec,
        out_shape=jax.ShapeDtypeStruct((rows, d), F32),
        compiler_params=_cparams(("arbitrary",)),
        name="moe_experts",
    )(block_e, n_used, xb, w1, w3, w2)


def _moe_ffn(h_bf16, ridx, rgate, w1, w3, w2):
    n, d = h_bf16.shape
    nk = 2 * n
    flat_e = ridx[:2].T.reshape(-1)
    onehot = (flat_e[:, None] == jnp.arange(N_EXPERTS, dtype=jnp.int32)[None, :]).astype(jnp.int32)
    csum = jnp.cumsum(onehot, axis=0)
    rank = jnp.take_along_axis(csum, flat_e[:, None], axis=1)[:, 0] - 1
    counts = csum[-1]
    padded = (counts + MOE_BLOCK - 1) // MOE_BLOCK * MOE_BLOCK
    pend = jnp.cumsum(padded)
    pstart = pend - padded
    dest = pstart[flat_e] + rank
    n_blocks = nk // MOE_BLOCK + N_EXPERTS
    rows = n_blocks * MOE_BLOCK
    flat_tok = jnp.repeat(jnp.arange(n, dtype=jnp.int32), 2)
    row_tok = jnp.zeros((rows,), jnp.int32).at[dest].set(flat_tok)
    block_start = jnp.arange(n_blocks, dtype=jnp.int32) * MOE_BLOCK
    block_e = jnp.minimum(jnp.sum(block_start[:, None] >= pend[None, :], axis=1), N_EXPERTS - 1).astype(jnp.int32)
    n_used = (pend[-1] // MOE_BLOCK).astype(jnp.int32).reshape(1)
    xb = h_bf16[row_tok]
    yb = _moe_experts(xb, block_e, n_used, w1, w3, w2)
    dest2 = dest.reshape(n, 2)
    g = rgate[:2].T
    return yb[dest2[:, 0]] * g[:, 0:1] + yb[dest2[:, 1]] * g[:, 1:2]


def _residual_kernel(x_ref, y_ref, gate_ref, g_ref, o_ref, *, final):
    x1 = x_ref[...] + gate_ref[0] * y_ref[...]
    if final:
        ms = jnp.mean(x1 * x1, axis=-1, keepdims=True)
        x1 = x1 * lax.rsqrt(ms + EPS) * g_ref[...]
    o_ref[...] = x1


def _residual(x, y, gate, g, seq, tm, final):
    n, d = x.shape
    per_seq = seq // tm
    return pl.pallas_call(
        functools.partial(_residual_kernel, final=final),
        grid=(n // tm,),
        in_specs=[pl.BlockSpec((tm, d), lambda i: (i, 0)),
                  pl.BlockSpec((tm, d), lambda i: (i, 0)),
                  pl.BlockSpec((1, 1, d), lambda i: (i // per_seq, 0, 0)),
                  pl.BlockSpec((1, d), lambda i: (0, 0))],
        out_specs=pl.BlockSpec((tm, d), lambda i: (i, 0)),
        out_shape=jax.ShapeDtypeStruct((n, d), F32),
        compiler_params=_cparams(("parallel",)),
        name="ffn_residual",
    )(x, y, gate, g.reshape(1, d))


def _pad_cols(w, width):
    return jnp.pad(w, ((0, 0), (0, width - w.shape[1])))


def _pad_rows(w, height):
    return jnp.pad(w, ((0, height - w.shape[0]), (0, 0)))


def kernel(x, c, positions, router_w, router_bias, ada_w, ada_b, norm_mix, norm_ffn, moe_w1, moe_w3, moe_w2, ab_w_in, ab_sinks, ab_conv_w, ab_conv_b, ab_conv_ln_g, ab_conv_ln_b, ab_w_out, cd_w_in, cd_shift_mu, cd_w0, cd_w2, cd_a0, cd_a2, cd_g2, cd_k_k, cd_k_a, cd_r_k, cd_ln_x_g, cd_ln_x_b, cd_lru_conv_w, cd_lru_conv_b, cd_lru_wa, cd_lru_ba, cd_lru_wx, cd_lru_bx, cd_lru_lambda, cd_w_out, final_norm):
    batch, seq, d = x.shape
    n = batch * seq
    depth = ada_w.shape[0]
    xf = x.reshape(n, d)
    mod = _ada_mod(c, ada_w, ada_b).reshape(depth, batch, 6, 1, d)
    cos_t, sin_t = _rope_tables(positions)
    att_q = ATT_HEADS * HEAD_DIM
    att_kv = ATT_KV_HEADS * HEAD_DIM
    tm = 256

    for layer in range(depth):
        j = layer // 2
        shift_m, scale_m, gate_m, shift_f, scale_f, gate_f = (mod[layer, :, q] for q in range(6))
        if layer % 2 == 0:
            w_in = ab_w_in[j]
            w_perm = jnp.concatenate([w_in[:, :att_q], w_in[:, att_q + 2 * att_kv:],
                                      w_in[:, att_q:att_q + 2 * att_kv]], axis=1).astype(BF16)
            z = _in_proj(xf, norm_mix[layer], shift_m, scale_m, w_perm, seq, 256, 512)
            conv_ch = ab_conv_w.shape[-1]
            kv0 = (att_q + 2 * conv_ch) // att_kv
            a1 = _attention(z, cos_t, sin_t, ab_sinks[j], batch, seq, kv0, kv0 + 1)
            a2 = _conformer_conv(z, ab_conv_w[j], ab_conv_b[j], ab_conv_ln_g[j], ab_conv_ln_b[j],
                                 batch, seq, att_q // conv_ch, att_q // conv_ch + 1, 128)
            w_out = ab_w_out[j]
        else:
            w_in = cd_w_in[j]
            dim = cd_w0.shape[-1]
            dl = cd_w2.shape[1]
            al = cd_a2.shape[1]
            gl = cd_g2.shape[1]
            s0 = 3 * dim
            seg_zw = w_in[:, s0:s0 + dl]
            seg_za = w_in[:, s0 + dl:s0 + dl + al]
            seg_zg = w_in[:, s0 + dl + al:s0 + dl + al + gl]
            s1 = s0 + dl + al + gl
            w_perm = jnp.concatenate([w_in[:, :s0], w_in[:, s1:], _pad_cols(seg_zw, LANES),
                                      _pad_cols(seg_za, LANES), seg_zg], axis=1).astype(BF16)
            mu = cd_shift_mu[j]
            mu_lo = jnp.concatenate([jnp.pad(mu[s0:s0 + dl], (0, LANES - dl)),
                                     jnp.pad(mu[s0 + dl:s0 + dl + al], (0, LANES - al)),
                                     mu[s0 + dl + al:s1]]).reshape(1, -1)
            row = lambda a: a.reshape(1, -1)
            p = dict(mu_r=row(mu[:dim]), mu_k=row(mu[dim:2 * dim]), mu_v=row(mu[2 * dim:s0]), mu_lo=mu_lo,
                     w0=row(cd_w0[j]), w2=_pad_rows(cd_w2[j], LANES).astype(BF16),
                     a0=row(cd_a0[j]), a2=_pad_rows(cd_a2[j], LANES).astype(BF16), g2=cd_g2[j].astype(BF16),
                     k_k=row(cd_k_k[j]), k_a=row(cd_k_a[j]), r_k=row(cd_r_k[j]),
                     ln_g=row(cd_ln_x_g[j]), ln_b=row(cd_ln_x_b[j]))
            z = _in_proj(xf, norm_mix[layer], shift_m, scale_m, w_perm, seq, 256, 512)
            lo_col = (5 * dim) // (4 * LANES)
            a1 = _rwkv(z, p, batch, seq, lo_col)
            a2 = _rglru(z, cd_lru_conv_w[j], cd_lru_conv_b[j], cd_lru_wa[j], cd_lru_ba[j],
                        cd_lru_wx[j], cd_lru_bx[j], cd_lru_lambda[j], batch, seq, 3, 4, 256)
            w_out = cd_w_out[j]
        x1, h, ridx, rgate = _out_proj(a1, a2, w_out, xf, gate_m, norm_ffn[layer], shift_f, scale_f,
                                       router_w, router_bias, seq, tm)
        y = _moe_ffn(h, ridx, rgate, moe_w1[layer].astype(BF16), moe_w3[layer].astype(BF16),
                     moe_w2[layer].astype(BF16))
        xf = _residual(x1, y, gate_f, final_norm, seq, 512, layer == depth - 1)
    return xf.reshape(batch, seq, d)
```

```python
import functools

import jax
import jax.numpy as jnp
from jax import lax
from jax.experimental import pallas as pl
from jax.experimental.pallas import tpu as pltpu

F32 = jnp.float32
BF16 = jnp.bfloat16

EPS = 1e-6
LN_EPS = 1e-5
RWKV_LN_EPS = 64e-5
HEAD_DIM = 64
LANES = 128
ATT_BLOCK = 128
WINDOW = 128
ATT_HEADS = 16
ATT_KV_HEADS = 4
CONV_WIDTH = 31
CONV_HALO = 32
LRU_CONV_WIDTH = 4
LRU_C = 8.0
N_EXPERTS = 16
N_GROUPS = 4
EXPERTS_PER_GROUP = 4
MOE_BLOCK = 256
RWKV_CHUNK = 64
RWKV_ROWS = 128
ROPE_THETA = 10000.0
VMEM_LIMIT = 56 * 1024 * 1024


def _cparams(sem):
    return pltpu.CompilerParams(dimension_semantics=sem, vmem_limit_bytes=VMEM_LIMIT)


def _bdot(a, b):
    return jnp.dot(a.astype(BF16), b.astype(BF16), preferred_element_type=F32)


def _split3(x):
    hi = x.astype(BF16)
    r1 = x - hi.astype(F32)
    mid = r1.astype(BF16)
    lo = (r1 - mid.astype(F32)).astype(BF16)
    return hi, mid, lo


def _dot_left_exact(m_bf16, x):
    hi, mid, lo = _split3(x)
    d = functools.partial(jnp.dot, preferred_element_type=F32)
    return d(m_bf16, hi) + d(m_bf16, mid) + d(m_bf16, lo)


def _dot_right_exact(x, m_bf16):
    hi, mid, lo = _split3(x)
    d = functools.partial(jnp.dot, preferred_element_type=F32)
    return d(hi, m_bf16) + d(mid, m_bf16) + d(lo, m_bf16)


def _dot_3pass(a, b):
    a_hi = a.astype(BF16)
    a_lo = (a - a_hi.astype(F32)).astype(BF16)
    b_hi = b.astype(BF16)
    b_lo = (b - b_hi.astype(F32)).astype(BF16)
    d = functools.partial(jnp.dot, preferred_element_type=F32)
    return d(a_hi, b_hi) + d(a_hi, b_lo) + d(a_lo, b_hi)


def _sigmoid(x):
    return 1.0 / (1.0 + jnp.exp(-x))


def _softplus(x):
    return jnp.maximum(x, 0.0) + jnp.log(1.0 + jnp.exp(-jnp.abs(x)))


def _rms_mod(x, g, shift, scale):
    ms = jnp.mean(x * x, axis=-1, keepdims=True)
    return (x * lax.rsqrt(ms + EPS) * g) * (1.0 + scale) + shift


def _ada_kernel(c_ref, w_ref, b_ref, o_ref):
    c = c_ref[...]
    o_ref[0] = _dot_3pass(c * _sigmoid(c), w_ref[0]) + b_ref[0]


def _ada_mod(c, ada_w, ada_b):
    depth, d, n6 = ada_w.shape
    b = c.shape[0]
    bp = 8
    tn = 768
    c_pad = jnp.zeros((bp, d), F32).at[:b].set(c)
    out = pl.pallas_call(
        _ada_kernel,
        grid=(depth, n6 // tn),
        in_specs=[pl.BlockSpec((bp, d), lambda l, j: (0, 0)),
                  pl.BlockSpec((1, d, tn), lambda l, j: (l, 0, j)),
                  pl.BlockSpec((1, 1, tn), lambda l, j: (l, 0, j))],
        out_specs=pl.BlockSpec((1, bp, tn), lambda l, j: (l, 0, j)),
        out_shape=jax.ShapeDtypeStruct((depth, bp, n6), F32),
        compiler_params=_cparams(("parallel", "parallel")),
        name="ada_mod",
    )(c_pad, ada_w, ada_b.reshape(depth, 1, n6))
    return out[:, :b]


def _rope_kernel(pos_ref, f_ref, cos_ref, sin_ref):
    ang = pos_ref[...].astype(F32) * f_ref[...]
    lane = lax.broadcasted_iota(jnp.int32, ang.shape, 1)
    s = jnp.sin(ang)
    cos_ref[...] = jnp.cos(ang)
    sin_ref[...] = jnp.where((lane & 32) == 0, -s, s)


def _rope_tables(positions):
    n = positions.size
    half = HEAD_DIM // 2
    inv_freq = ROPE_THETA ** (-jnp.arange(half, dtype=F32) / half)
    f_row = jnp.tile(inv_freq, LANES // half).reshape(1, LANES)
    tm = 512
    return pl.pallas_call(
        _rope_kernel,
        grid=(n // tm,),
        in_specs=[pl.BlockSpec((tm, 1), lambda i: (i, 0)),
                  pl.BlockSpec((1, LANES), lambda i: (0, 0))],
        out_specs=[pl.BlockSpec((tm, LANES), lambda i: (i, 0))] * 2,
        out_shape=[jax.ShapeDtypeStruct((n, LANES), F32)] * 2,
        compiler_params=_cparams(("parallel",)),
        name="rope_tables",
    )(positions.reshape(n, 1), f_row)


def _in_proj_kernel(x_ref, g_ref, sh_ref, sc_ref, w_ref, o_ref, *, tn):
    h = _rms_mod(x_ref[...], g_ref[...], sh_ref[0], sc_ref[0]).astype(BF16)
    for c in range(o_ref.shape[1] // tn):
        o_ref[:, c * tn:(c + 1) * tn] = jnp.dot(h, w_ref[:, c * tn:(c + 1) * tn], preferred_element_type=F32)


def _in_proj(x, g, shift, scale, w_bf16, seq, tm, tn):
    n, d = x.shape
    nout = w_bf16.shape[1]
    per_seq = seq // tm
    return pl.pallas_call(
        functools.partial(_in_proj_kernel, tn=tn),
        grid=(n // tm,),
        in_specs=[pl.BlockSpec((tm, d), lambda i: (i, 0)),
                  pl.BlockSpec((1, d), lambda i: (0, 0)),
                  pl.BlockSpec((1, 1, d), lambda i: (i // per_seq, 0, 0)),
                  pl.BlockSpec((1, 1, d), lambda i: (i // per_seq, 0, 0)),
                  pl.BlockSpec((d, nout), lambda i: (0, 0), pipeline_mode=pl.Buffered(1))],
        out_specs=pl.BlockSpec((tm, nout), lambda i: (i, 0)),
        out_shape=jax.ShapeDtypeStruct((n, nout), F32),
        compiler_params=_cparams(("parallel",)),
        name="in_proj",
    )(x, g.reshape(1, d), shift, scale, w_bf16)


def _route(logits_t, bias_col):
    m = jnp.max(logits_t, axis=0, keepdims=True)
    e = jnp.exp(logits_t - m)
    probs = e / jnp.sum(e, axis=0, keepdims=True)
    sel = probs + bias_col
    s = [sel[i:i + 1, :] for i in range(N_EXPERTS)]
    pr = [probs[i:i + 1, :] for i in range(N_EXPERTS)]

    def top2_sum(v):
        best = v[0] + v[1]
        for i in range(len(v)):
            for j in range(i + 1, len(v)):
                if (i, j) != (0, 1):
                    best = jnp.maximum(best, v[i] + v[j])
        return best

    gs = [top2_sum(s[4 * g:4 * g + 4]) for g in range(N_GROUPS)]
    best = gs[0]
    gi = jnp.zeros_like(best, dtype=jnp.int32)
    for g in range(1, N_GROUPS):
        upd = gs[g] > best
        gi = jnp.where(upd, g, gi)
        best = jnp.where(upd, gs[g], best)

    def pick_group(rows):
        out = []
        for l in range(EXPERTS_PER_GROUP):
            v = rows[l]
            for g in range(1, N_GROUPS):
                v = jnp.where(gi == g, rows[4 * g + l], v)
            out.append(v)
        return out

    ig = pick_group(s)
    pg = pick_group(pr)

    def argmax4(v):
        bv = v[0]
        bi = jnp.zeros_like(gi)
        for l in range(1, EXPERTS_PER_GROUP):
            upd = v[l] > bv
            bi = jnp.where(upd, l, bi)
            bv = jnp.where(upd, v[l], bv)
        return bi

    i1 = argmax4(ig)
    i2 = argmax4([jnp.where(i1 == l, -jnp.inf, ig[l]) for l in range(EXPERTS_PER_GROUP)])

    def pick(v, idx):
        o = v[0]
        for l in range(1, EXPERTS_PER_GROUP):
            o = jnp.where(idx == l, v[l], o)
        return o

    p1 = pick(pg, i1)
    p2 = pick(pg, i2)
    tot = p1 + p2
    return gi * EXPERTS_PER_GROUP + i1, gi * EXPERTS_PER_GROUP + i2, p1 / tot, p2 / tot


def _out_proj_kernel(a1_ref, a2_ref, w1_ref, w2_ref, x_ref, gate_ref, g_ref, sh_ref, sc_ref,
                     rw_ref, rb_ref, upper_ref, xo_ref, h_ref, ridx_ref, rgate_ref, cnt_ref, carry):
    @pl.when(pl.program_id(0) == 0)
    def _():
        carry[...] = jnp.zeros_like(carry)

    y = (jnp.dot(a1_ref[...], w1_ref[...], preferred_element_type=F32)
         + jnp.dot(a2_ref[...], w2_ref[...], preferred_element_type=F32))
    x1 = x_ref[...] + gate_ref[0] * y
    xo_ref[...] = x1
    h = _rms_mod(x1, g_ref[...], sh_ref[0], sc_ref[0])
    h_ref[...] = h
    logits = _dot_3pass(h, rw_ref[...])
    logits_t = logits.T[:N_EXPERTS, :]
    e1, e2, g1, g2 = _route(logits_t, rb_ref[...])
    erow = lax.broadcasted_iota(jnp.int32, logits_t.shape, 0)
    hit1 = erow == e1
    hit2 = erow == e2
    onehot = jnp.where(hit1 | hit2, 1.0, 0.0)
    before = jnp.dot(onehot.astype(BF16), upper_ref[...], preferred_element_type=F32) + carry[:, 0:1]
    rank1 = jnp.sum(jnp.where(hit1, before, 0.0), axis=0, keepdims=True).astype(jnp.int32)
    rank2 = jnp.sum(jnp.where(hit2, before, 0.0), axis=0, keepdims=True).astype(jnp.int32)
    total = carry[...] + jnp.sum(onehot, axis=1, keepdims=True)
    carry[...] = total
    cnt_ref[...] = total.astype(jnp.int32)
    row = lax.broadcasted_iota(jnp.int32, ridx_ref.shape, 0)
    ridx_ref[...] = jnp.where(row == 0, e1, jnp.where(row == 1, e2,
                              jnp.where(row == 2, rank1, jnp.where(row == 3, rank2, 0))))
    rgate_ref[...] = jnp.where(row == 0, g1, jnp.where(row == 1, g2, 0.0))


def _out_proj(a1, a2, w_out, x, gate, g, shift, scale, router_w, router_bias, seq, tm):
    n, d = x.shape
    ka = a1.shape[1]
    per_seq = seq // tm
    w1 = w_out[:ka].astype(BF16)
    w2 = w_out[ka:].astype(BF16)
    rw = jnp.zeros((d, LANES), F32).at[:, :N_EXPERTS].set(router_w)
    rb = router_bias.astype(F32).reshape(N_EXPERTS, 1)
    upper = (jnp.arange(tm)[:, None] < jnp.arange(tm)[None, :]).astype(BF16)
    mod_spec = pl.BlockSpec((1, 1, d), lambda i: (i // per_seq, 0, 0))
    return pl.pallas_call(
        _out_proj_kernel,
        grid=(n // tm,),
        in_specs=[pl.BlockSpec((tm, ka), lambda i: (i, 0)),
                  pl.BlockSpec((tm, ka), lambda i: (i, 0)),
                  pl.BlockSpec((ka, d), lambda i: (0, 0), pipeline_mode=pl.Buffered(1)),
                  pl.BlockSpec((ka, d), lambda i: (0, 0), pipeline_mode=pl.Buffered(1)),
                  pl.BlockSpec((tm, d), lambda i: (i, 0)),
                  mod_spec,
                  pl.BlockSpec((1, d), lambda i: (0, 0)),
                  mod_spec, mod_spec,
                  pl.BlockSpec((d, LANES), lambda i: (0, 0)),
                  pl.BlockSpec((N_EXPERTS, 1), lambda i: (0, 0)),
                  pl.BlockSpec((tm, tm), lambda i: (0, 0))],
        out_specs=[pl.BlockSpec((tm, d), lambda i: (i, 0)),
                   pl.BlockSpec((tm, d), lambda i: (i, 0)),
                   pl.BlockSpec((8, tm), lambda i: (0, i)),
                   pl.BlockSpec((8, tm), lambda i: (0, i)),
                   pl.BlockSpec((N_EXPERTS, LANES), lambda i: (0, 0))],
        out_shape=[jax.ShapeDtypeStruct((n, d), F32),
                   jax.ShapeDtypeStruct((n, d), F32),
                   jax.ShapeDtypeStruct((8, n), jnp.int32),
                   jax.ShapeDtypeStruct((8, n), F32),
                   jax.ShapeDtypeStruct((N_EXPERTS, LANES), jnp.int32)],
        scratch_shapes=[pltpu.VMEM((N_EXPERTS, LANES), F32)],
        compiler_params=_cparams(("arbitrary",)),
        name="out_proj_route",
    )(a1, a2, w1, w2, x, gate, g.reshape(1, d), shift, scale, rw, rb, upper)


def _rope_apply(x, cos, sin_signed):
    outs = []
    lane = lax.broadcasted_iota(jnp.int32, cos.shape, 1)
    first_half = (lane & 32) == 0
    for c in range(x.shape[1] // LANES):
        xc = x[:, c * LANES:(c + 1) * LANES]
        partner = jnp.where(first_half, pltpu.roll(xc, LANES - 32, axis=1), pltpu.roll(xc, 32, axis=1))
        outs.append(xc * cos + partner * sin_signed)
    return outs


def _attn_kernel(sink_ref, q_ref, kp_ref, kc_ref, vp_ref, vc_ref, cq_ref, sq_ref, cp_ref, sp_ref, o_ref):
    has_prev = pl.program_id(1) > 0
    cq, sq = cq_ref[...], sq_ref[...]
    q_cols = _rope_apply(q_ref[...] * (HEAD_DIM ** -0.5), cq, sq)
    k_cols = [jnp.concatenate([kp, kc], axis=0) for kp, kc in
              zip(_rope_apply(kp_ref[...], cp_ref[...], sp_ref[...]), _rope_apply(kc_ref[...], cq, sq))]
    v_all = jnp.concatenate([vp_ref[...], vc_ref[...]], axis=0).astype(BF16)
    qi = lax.broadcasted_iota(jnp.int32, (ATT_BLOCK, 2 * ATT_BLOCK), 0)
    kj = lax.broadcasted_iota(jnp.int32, (ATT_BLOCK, 2 * ATT_BLOCK), 1)
    valid = ((kj > qi) & (kj < ATT_BLOCK) & has_prev) | ((kj >= ATT_BLOCK) & (kj - ATT_BLOCK <= qi))
    group = ATT_HEADS // ATT_KV_HEADS
    out_cols = []
    for pair in range(ATT_HEADS // 2):
        halves = []
        for sub in range(2):
            hq = 2 * pair + sub
            hk = hq // group
            qh = q_cols[pair][:, sub * HEAD_DIM:(sub + 1) * HEAD_DIM].astype(BF16)
            kh = k_cols[hk // 2][:, (hk % 2) * HEAD_DIM:(hk % 2 + 1) * HEAD_DIM].astype(BF16)
            vh = v_all[:, hk * HEAD_DIM:(hk + 1) * HEAD_DIM]
            s = lax.dot_general(qh, kh, (((1,), (1,)), ((), ())), preferred_element_type=F32)
            s = jnp.where(valid, s, -1e30)
            sink = sink_ref[hq]
            m = jnp.maximum(jnp.max(s, axis=-1, keepdims=True), sink)
            p = jnp.exp(s - m)
            den = jnp.sum(p, axis=-1, keepdims=True) + jnp.exp(sink - m)
            halves.append(jnp.dot(p.astype(BF16), vh, preferred_element_type=F32) / den)
        out_cols.append(jnp.concatenate(halves, axis=1))
    o_ref[...] = jnp.concatenate(out_cols, axis=1).astype(o_ref.dtype)


def _attention(z, cos_t, sin_t, sinks, batch, seq, k_col, v_col):
    n = z.shape[0]
    nb = seq // ATT_BLOCK
    att_q = ATT_HEADS * HEAD_DIM
    att_kv = ATT_KV_HEADS * HEAD_DIM
    cur = lambda b, i: b * nb + i
    prev = lambda b, i: b * nb + jnp.maximum(i - 1, 0)
    return pl.pallas_call(
        _attn_kernel,
        grid=(batch, nb),
        in_specs=[pl.BlockSpec(memory_space=pltpu.SMEM),
                  pl.BlockSpec((ATT_BLOCK, att_q), lambda b, i: (cur(b, i), 0)),
                  pl.BlockSpec((ATT_BLOCK, att_kv), lambda b, i: (prev(b, i), k_col)),
                  pl.BlockSpec((ATT_BLOCK, att_kv), lambda b, i: (cur(b, i), k_col)),
                  pl.BlockSpec((ATT_BLOCK, att_kv), lambda b, i: (prev(b, i), v_col)),
                  pl.BlockSpec((ATT_BLOCK, att_kv), lambda b, i: (cur(b, i), v_col)),
                  pl.BlockSpec((ATT_BLOCK, LANES), lambda b, i: (cur(b, i), 0)),
                  pl.BlockSpec((ATT_BLOCK, LANES), lambda b, i: (cur(b, i), 0)),
                  pl.BlockSpec((ATT_BLOCK, LANES), lambda b, i: (prev(b, i), 0)),
                  pl.BlockSpec((ATT_BLOCK, LANES), lambda b, i: (prev(b, i), 0))],
        out_specs=pl.BlockSpec((ATT_BLOCK, att_q), lambda b, i: (cur(b, i), 0)),
        out_shape=jax.ShapeDtypeStruct((n, att_q), BF16),
        compiler_params=_cparams(("parallel", "parallel")),
        name="swa_attention",
    )(sinks.astype(F32), z, z, z, z, z, cos_t, sin_t, cos_t, sin_t)


def _conv_kernel(val_ref, gate_ref, hval_ref, hgate_ref, w_ref, b_ref, g_ref, beta_ref, o_ref, ybuf, acc):
    t = val_ref.shape[0]
    ch = val_ref.shape[1]
    first = pl.program_id(1) == 0
    yh = hval_ref[...] * _sigmoid(hgate_ref[...])
    ybuf[0:CONV_HALO, :] = jnp.where(first, 0.0, yh)
    ybuf[CONV_HALO:, :] = val_ref[...] * _sigmoid(gate_ref[...])
    rt = 64
    off = CONV_HALO - (CONV_WIDTH - 1)
    for cb in range(ch // LANES):
        cs = slice(cb * LANES, (cb + 1) * LANES)
        for r0 in range(0, t, rt):
            a = jnp.broadcast_to(b_ref[:, cs], (rt, LANES))
            for j in range(CONV_WIDTH):
                a = a + w_ref[j:j + 1, cs] * ybuf[r0 + off + j:r0 + off + j + rt, cs]
            acc[r0:r0 + rt, cs] = a
    y = acc[...]
    mu = jnp.mean(y, axis=-1, keepdims=True)
    yc = y - mu
    var = jnp.mean(yc * yc, axis=-1, keepdims=True)
    ln = yc * lax.rsqrt(var + LN_EPS) * g_ref[...] + beta_ref[...]
    o_ref[...] = (ln * _sigmoid(ln)).astype(o_ref.dtype)


def _conformer_conv(z, conv_w, conv_b, ln_g, ln_b, batch, seq, val_col, gate_col, t):
    n = z.shape[0]
    ch = conv_w.shape[-1]
    nt = seq // t
    hpt = t // CONV_HALO
    w = jnp.zeros((CONV_HALO, ch), F32).at[:CONV_WIDTH].set(conv_w.reshape(CONV_WIDTH, ch))
    cur = lambda b, i: b * nt + i
    halo = lambda b, i: jnp.maximum((b * nt + i) * hpt - 1, 0)
    vec = pl.BlockSpec((1, ch), lambda b, i: (0, 0))
    return pl.pallas_call(
        _conv_kernel,
        grid=(batch, nt),
        in_specs=[pl.BlockSpec((t, ch), lambda b, i: (cur(b, i), val_col)),
                  pl.BlockSpec((t, ch), lambda b, i: (cur(b, i), gate_col)),
                  pl.BlockSpec((CONV_HALO, ch), lambda b, i: (halo(b, i), val_col)),
                  pl.BlockSpec((CONV_HALO, ch), lambda b, i: (halo(b, i), gate_col)),
                  pl.BlockSpec((CONV_HALO, ch), lambda b, i: (0, 0)),
                  vec, vec, vec],
        out_specs=pl.BlockSpec((t, ch), lambda b, i: (cur(b, i), 0)),
        out_shape=jax.ShapeDtypeStruct((n, ch), BF16),
        scratch_shapes=[pltpu.VMEM((t + CONV_HALO, ch), F32), pltpu.VMEM((t, ch), F32)],
        compiler_params=_cparams(("parallel", "parallel")),
        name="conformer_conv",
    )(z, z, z, z, w, conv_b.reshape(1, ch), ln_g.reshape(1, ch), ln_b.reshape(1, ch))


def _shift_lerp(z, halo, mu, first):
    prev = pltpu.roll(z, 1, axis=0)
    row = lax.broadcasted_iota(jnp.int32, z.shape, 0)
    hrow = jnp.where(first, 0.0, halo[7:8, :])
    prev = jnp.where(row == 0, hrow, prev)
    return z + mu * (prev - z)


def _rwkv_kernel(r_ref, k_ref, v_ref, lo_ref, rh_ref, kh_ref, vh_ref, loh_ref,
                 mur_ref, muk_ref, muv_ref, mulo_ref, w0_ref, w2_ref, a0_ref, a2_ref, g2_ref,
                 kk_ref, ka_ref, rk_ref, lng_ref, lnb_ref, seg_ref, segt_ref, tril_ref,
                 o_ref, hs_ref):
    first = pl.program_id(1) == 0

    @pl.when(first)
    def _():
        hs_ref[...] = jnp.zeros_like(hs_ref)

    r = _shift_lerp(r_ref[...], rh_ref[...], mur_ref[...], first)
    k = _shift_lerp(k_ref[...], kh_ref[...], muk_ref[...], first)
    v = _shift_lerp(v_ref[...], vh_ref[...], muv_ref[...], first)
    lo = _shift_lerp(lo_ref[...], loh_ref[...], mulo_ref[...], first)
    zw, za, zg = lo[:, 0:LANES], lo[:, LANES:2 * LANES], lo[:, 2 * LANES:]

    seg = seg_ref[...]
    segt = segt_ref[...]

    def head_sum(x):
        return _dot_right_exact(_dot_right_exact(x, seg), segt)

    w_raw = w0_ref[...] + _bdot(jnp.tanh(zw), w2_ref[...])
    log_decay = -jnp.exp(-_softplus(-w_raw) - 0.5)
    a_gate = _sigmoid(a0_ref[...] + _bdot(za, a2_ref[...]))
    g = _bdot(_sigmoid(zg), g2_ref[...])
    kk = k * kk_ref[...]
    kk = kk * lax.rsqrt(jnp.maximum(head_sum(kk * kk), 1e-24))
    k_mod = k * (1.0 + (a_gate - 1.0) * ka_ref[...])
    b_vec = kk * a_gate

    n_chunks = r.shape[0] // RWKV_CHUNK
    rows = [slice(c * RWKV_CHUNK, (c + 1) * RWKV_CHUNK) for c in range(n_chunks)]
    cum = _dot_left_exact(tril_ref[...], log_decay)
    cum_last = [cum[rs.stop - 1:rs.stop, :] for rs in rows]
    w_last = [jnp.exp(cl) for cl in cum_last]
    inv_w = jnp.exp(-cum)
    to_end = jnp.concatenate([jnp.exp(cum_last[c] - cum[rows[c], :]) for c in range(n_chunks)], axis=0)
    a_til = -kk * jnp.exp(cum - log_decay)
    r_til = r * jnp.exp(cum)
    b_til = b_vec * inv_w
    k_til = k_mod * inv_w
    b_hat = b_vec * to_end
    k_hat = k_mod * to_end

    lane = lax.broadcasted_iota(jnp.int32, (RWKV_CHUNK, LANES), 1)
    m0 = lane < HEAD_DIM
    ri = lax.broadcasted_iota(jnp.int32, (LANES, LANES), 0)
    ci = lax.broadcasted_iota(jnp.int32, (LANES, LANES), 1)
    same = (ri < HEAD_DIM) == (ci < HEAD_DIM)
    mask_sl = same & (ci < ri)
    mask_li = same & (ci <= ri)
    eye = ri == ci

    def stack(x):
        return jnp.concatenate([jnp.where(m0, x, 0.0), jnp.where(m0, 0.0, x)], axis=0)

    nt_dims = (((1,), (1,)), ((), ()))
    dot = functools.partial(jnp.dot, preferred_element_type=F32)
    n_pairs = r.shape[1] // LANES
    cols = [slice(p * LANES, (p + 1) * LANES) for p in range(n_pairs)]
    units = [(rows[c], cols[p]) for c in range(n_chunks) for p in range(n_pairs)]
    ids = range(len(units))
    xa_f = [stack(a_til[u]) for u in units]
    xr_f = [stack(r_til[u]) for u in units]
    vs = [stack(v[u]).astype(BF16) for u in units]
    m = [lax.dot_general(jnp.concatenate([xa_f[i], xr_f[i]], axis=0).astype(BF16),
                         jnp.concatenate([b_til[units[i]]] * 2 + [k_til[units[i]]] * 2, axis=0).astype(BF16),
                         nt_dims, preferred_element_type=F32) for i in ids]
    power = [jnp.where(mask_sl, m[i][:LANES, :LANES], 0.0).astype(BF16) for i in ids]
    a_ak = [jnp.where(mask_sl, m[i][:LANES, LANES:], 0.0).astype(BF16) for i in ids]
    z = [jnp.concatenate([xa_f[i], dot(a_ak[i], vs[i])], axis=1) for i in ids]
    for level in range(6):
        z = [z[i] + dot(power[i], z[i].astype(BF16)) for i in ids]
        if level < 5:
            power = [dot(power[i], power[i]).astype(BF16) for i in ids]
    zb = [z[i].astype(BF16) for i in ids]
    m_rb = [jnp.where(mask_li, m[i][LANES:, :LANES], 0.0).astype(BF16) for i in ids]
    m_rk = [jnp.where(mask_li, m[i][LANES:, LANES:], 0.0).astype(BF16) for i in ids]
    q = [jnp.concatenate([xr_f[i], dot(m_rk[i], vs[i])], axis=1) + dot(m_rb[i], zb[i]) for i in ids]
    ge = [dot(stack(b_hat[units[i]]).T.astype(BF16), zb[i]) for i in ids]
    e_mat = [ge[i][:, LANES:] + dot(stack(k_hat[units[i]]).T.astype(BF16), vs[i]) for i in ids]
    qg = [jnp.concatenate([q[i][:, :LANES],
                           ge[i][:, :LANES] + jnp.where(eye, w_last[i // n_pairs][:, cols[i % n_pairs]], 0.0)],
                          axis=0).astype(BF16) for i in ids]
    state = [hs_ref[p] for p in range(n_pairs)]
    y_rows = []
    for c in range(n_chunks):
        y_cols = []
        for p in range(n_pairs):
            i = c * n_pairs + p
            yh = dot(qg[i], state[p].astype(BF16))
            state[p] = yh[LANES:] + e_mat[i]
            y2 = yh[:LANES] + q[i][:, LANES:]
            y_cols.append(y2[:RWKV_CHUNK] + y2[RWKV_CHUNK:])
        y_rows.append(jnp.concatenate(y_cols, axis=1))
    for p in range(n_pairs):
        hs_ref[p] = state[p]
    y = jnp.concatenate(y_rows, axis=0)

    inv_hd = 1.0 / HEAD_DIM
    mu = head_sum(y) * inv_hd
    yc = y - mu
    var = head_sum(yc * yc) * inv_hd
    yn = yc * lax.rsqrt(var + RWKV_LN_EPS) * lng_ref[...] + lnb_ref[...]
    bonus = head_sum(r * k_mod * rk_ref[...]) * v
    o_ref[...] = ((yn + bonus) * g).astype(o_ref.dtype)


def _rwkv(z, p, batch, seq, lo_col):
    n = z.shape[0]
    dim = p["w0"].shape[-1]
    t = RWKV_ROWS
    nt = seq // t
    hpt = t // 8
    lo_w = 4 * LANES
    cur = lambda b, i: b * nt + i
    halo = lambda b, i: jnp.maximum((b * nt + i) * hpt - 1, 0)
    heads = dim // HEAD_DIM
    seg = (jnp.arange(dim)[:, None] // HEAD_DIM == jnp.arange(heads)[None, :]).astype(BF16)
    ti = jnp.arange(t)
    tril = ((ti[:, None] >= ti[None, :]) & (ti[:, None] // RWKV_CHUNK == ti[None, :] // RWKV_CHUNK)).astype(BF16)

    def full(a):
        return pl.BlockSpec(a.shape, lambda b, i: (0,) * a.ndim)

    consts = [p["mu_r"], p["mu_k"], p["mu_v"], p["mu_lo"], p["w0"], p["w2"], p["a0"], p["a2"], p["g2"],
              p["k_k"], p["k_a"], p["r_k"], p["ln_g"], p["ln_b"], seg, seg.T, tril]
    col = lambda c: pl.BlockSpec((t, dim), lambda b, i: (cur(b, i), c))
    hcol = lambda c: pl.BlockSpec((8, dim), lambda b, i: (halo(b, i), c))
    return pl.pallas_call(
        _rwkv_kernel,
        grid=(batch, nt),
        in_specs=[col(0), col(1), col(2), pl.BlockSpec((t, lo_w), lambda b, i: (cur(b, i), lo_col)),
                  hcol(0), hcol(1), hcol(2), pl.BlockSpec((8, lo_w), lambda b, i: (halo(b, i), lo_col))]
                 + [full(a) for a in consts],
        out_specs=pl.BlockSpec((t, dim), lambda b, i: (cur(b, i), 0)),
        out_shape=jax.ShapeDtypeStruct((n, dim), BF16),
        scratch_shapes=[pltpu.VMEM((dim // LANES, LANES, LANES), F32)],
        compiler_params=_cparams(("parallel", "arbitrary")),
        name="rwkv7_chunked",
    )(z, z, z, z, z, z, z, z, *consts)


def _lru_kernel(x_ref, gb_ref, xh_ref, cw_ref, cb_ref, wg_ref, ba_ref, bx_ref, lam_ref, o_ref,
                xbuf, carry):
    t, ch = x_ref.shape
    first = pl.program_id(1) == 0

    @pl.when(first)
    def _():
        carry[...] = jnp.zeros_like(carry)

    xbuf[0:8, :] = jnp.where(first, 0.0, xh_ref[...])
    xbuf[8:, :] = x_ref[...]
    off = 8 - (LRU_CONV_WIDTH - 1)
    xc = jnp.broadcast_to(cb_ref[...], (t, ch))
    for j in range(LRU_CONV_WIDTH):
        xc = xc + cw_ref[j:j + 1, :] * xbuf[off + j:off + j + t, :]
    gr, gi = [], []
    for p in range(ch // LANES):
        gp = _bdot(xc[:, p * LANES:(p + 1) * LANES], wg_ref[p])
        gr.append(gp[:, :LANES])
        gi.append(gp[:, LANES:])
    r = _sigmoid(jnp.concatenate(gr, axis=1) + ba_ref[...])
    i = _sigmoid(jnp.concatenate(gi, axis=1) + bx_ref[...])
    log_a = -LRU_C * r * _softplus(-lam_ref[...])
    a = jnp.exp(log_a)
    u = jnp.sqrt(1.0 - jnp.exp(2.0 * log_a)) * (i * xc)
    row = lax.broadcasted_iota(jnp.int32, (t, ch), 0)
    d = 1
    while d < t:
        keep = row >= d
        a_sh = jnp.where(keep, pltpu.roll(a, d, axis=0), 1.0)
        u_sh = jnp.where(keep, pltpu.roll(u, d, axis=0), 0.0)
        u = a * u_sh + u
        a = a * a_sh
        d *= 2
    h = u + a * carry[...]
    carry[...] = h[t - 1:t, :]
    gb = gb_ref[...]
    gelu = 0.5 * gb * (1.0 + jnp.tanh(0.7978845608028654 * (gb + 0.044715 * (gb * gb * gb))))
    o_ref[...] = (h * gelu).astype(o_ref.dtype)


def _rglru(z, conv_w, conv_b, wa, ba, wx, bx, lam, batch, seq, x_col, g_col, t):
    n = z.shape[0]
    ch = lam.shape[-1]
    nt = seq // t
    hpt = t // 8
    nblk, bd = wa.shape[0], wa.shape[1]
    per = LANES // bd
    def blockdiag(w):
        w = w.reshape(nblk // per, per, bd, bd)
        out = jnp.zeros((nblk // per, LANES, LANES), F32)
        for q in range(per):
            out = out.at[:, q * bd:(q + 1) * bd, q * bd:(q + 1) * bd].set(w[:, q])
        return out
    wg = jnp.concatenate([blockdiag(wa), blockdiag(wx)], axis=-1).astype(BF16)
    cw = jnp.zeros((8, ch), F32).at[:LRU_CONV_WIDTH].set(conv_w.reshape(LRU_CONV_WIDTH, ch))
    cur = lambda b, i: b * nt + i
    halo = lambda b, i: jnp.maximum((b * nt + i) * hpt - 1, 0)
    vec = pl.BlockSpec((1, ch), lambda b, i: (0, 0))
    return pl.pallas_call(
        _lru_kernel,
        grid=(batch, nt),
        in_specs=[pl.BlockSpec((t, ch), lambda b, i: (cur(b, i), x_col)),
                  pl.BlockSpec((t, ch), lambda b, i: (cur(b, i), g_col)),
                  pl.BlockSpec((8, ch), lambda b, i: (halo(b, i), x_col)),
                  pl.BlockSpec((8, ch), lambda b, i: (0, 0)),
                  vec,
                  pl.BlockSpec(wg.shape, lambda b, i: (0, 0, 0)),
                  vec, vec, vec],
        out_specs=pl.BlockSpec((t, ch), lambda b, i: (cur(b, i), 0)),
        out_shape=jax.ShapeDtypeStruct((n, ch), BF16),
        scratch_shapes=[pltpu.VMEM((t + 8, ch), F32), pltpu.VMEM((1, ch), F32)],
        compiler_params=_cparams(("parallel", "arbitrary")),
        name="rglru",
    )(z, z, z, cw, conv_b.reshape(1, ch), wg, ba.reshape(1, ch), bx.reshape(1, ch), lam.reshape(1, ch))


def _moe_kernel(be_ref, tok_ref, dst_ref, h_hbm, w1_ref, w3_ref, w2_ref, o_hbm, x0, x1, o0, o1, gsem, ssem):
    i = pl.program_id(0)
    last = pl.num_programs(0) - 1
    xs, outs = (x0, x1), (o0, o1)

    def gather_row(j, r, s):
        return pltpu.make_async_copy(h_hbm.at[pl.ds(tok_ref[j, r], 1)], xs[s].at[pl.ds(r, 1)], gsem.at[s])

    def scatter_row(j, r, s):
        return pltpu.make_async_copy(outs[s].at[pl.ds(r, 1)], o_hbm.at[pl.ds(dst_ref[j, r], 1)], ssem.at[s])

    def gather_block(s):
        return pltpu.make_async_copy(h_hbm.at[pl.ds(0, MOE_BLOCK)], xs[s], gsem.at[s])

    def scatter_block(s):
        return pltpu.make_async_copy(outs[s], o_hbm.at[pl.ds(0, MOE_BLOCK)], ssem.at[s])

    @pl.when(i == 0)
    def _():
        o1[...] = jnp.zeros_like(o1)
        for r in range(MOE_BLOCK):
            gather_row(0, r, 0).start()

    def step(s):
        t = 1 - s
        gather_block(s).wait()

        @pl.when(i >= 1)
        def _():
            scatter_block(s).wait()

        nxt = jnp.minimum(i + 1, last)
        prv = jnp.maximum(i - 1, 0)
        for r in range(MOE_BLOCK):
            gather_row(nxt, r, t).start()
            scatter_row(prv, r, t).start()
        x = xs[s][...].astype(BF16)
        h1 = jnp.dot(x, w1_ref[0], preferred_element_type=F32)
        h3 = jnp.dot(x, w3_ref[0], preferred_element_type=F32)
        hid = (h1 * _sigmoid(h1) * h3).astype(BF16)
        outs[s][...] = jnp.dot(hid, w2_ref[0], preferred_element_type=F32)

        @pl.when(i == last)
        def _():
            for r in range(MOE_BLOCK):
                scatter_row(i, r, s).start()
            scatter_block(t).wait()
            scatter_block(s).wait()
            gather_block(t).wait()

    for s in range(2):
        pl.when(i % 2 == s)(functools.partial(step, s))


def _moe_experts(h, block_e, row_tok, row_dst, w1, w3, w2):
    n, d = h.shape
    f = w1.shape[-1]
    n_blocks = row_dst.shape[0]
    rows = n_blocks * MOE_BLOCK
    grid_spec = pltpu.PrefetchScalarGridSpec(
        num_scalar_prefetch=3,
        grid=(n_blocks,),
        in_specs=[pl.BlockSpec(memory_space=pl.ANY),
                  pl.BlockSpec((1, d, f), lambda i, be, tok, dst: (be[i], 0, 0)),
                  pl.BlockSpec((1, d, f), lambda i, be, tok, dst: (be[i], 0, 0)),
                  pl.BlockSpec((1, f, d), lambda i, be, tok, dst: (be[i], 0, 0))],
        out_specs=pl.BlockSpec(memory_space=pl.ANY),
        scratch_shapes=[pltpu.VMEM((MOE_BLOCK, d), F32)] * 4
                       + [pltpu.SemaphoreType.DMA((2,)), pltpu.SemaphoreType.DMA((2,))],
    )
    return pl.pallas_call(
        _moe_kernel,
        grid_spec=grid_spec,
        out_shape=jax.ShapeDtypeStruct((rows, d), F32),
        compiler_params=_cparams(("arbitrary",)),
        name="moe_experts",
    )(block_e, row_tok, row_dst, h, w1, w3, w2)


def _moe_plan(ridx, counts, n):
    nk = 2 * n
    e = ridx[0:2]
    rank = ridx[2:4]
    padded = (counts + MOE_BLOCK - 1) // MOE_BLOCK * MOE_BLOCK
    pend = jnp.cumsum(padded)
    pstart = pend - padded
    dest = pstart[e] + rank
    n_blocks = nk // MOE_BLOCK + N_EXPERTS
    rows = n_blocks * MOE_BLOCK
    result_row = jnp.arange(nk, dtype=jnp.int32)
    placed = jnp.full((rows,), -1, jnp.int32).at[dest.reshape(-1)].set(result_row)
    is_pad = placed < 0
    pad_rank = jnp.cumsum(is_pad.astype(jnp.int32)) - 1
    row_dst = jnp.where(is_pad, nk + pad_rank, placed)
    row_tok = jnp.where(is_pad, 0, jnp.where(placed >= n, placed - n, placed))
    block_start = jnp.arange(n_blocks, dtype=jnp.int32) * MOE_BLOCK
    block_e = jnp.minimum(jnp.sum(block_start[:, None] >= pend[None, :], axis=1), N_EXPERTS - 1).astype(jnp.int32)
    return block_e, row_tok.reshape(n_blocks, MOE_BLOCK), row_dst.reshape(n_blocks, MOE_BLOCK)


def _residual_kernel(x_ref, o1_ref, o2_ref, rg_ref, gate_ref, g_ref, out_ref, *, final):
    g = rg_ref[...].T
    y = g[:, 0:1] * o1_ref[...] + g[:, 1:2] * o2_ref[...]
    x2 = x_ref[...] + gate_ref[0] * y
    if final:
        ms = jnp.mean(x2 * x2, axis=-1, keepdims=True)
        x2 = x2 * lax.rsqrt(ms + EPS) * g_ref[...]
    out_ref[...] = x2


def _residual(x, o_all, rgate, gate, g, seq, tm, final):
    n, d = x.shape
    per_seq = seq // tm
    nt = n // tm
    return pl.pallas_call(
        functools.partial(_residual_kernel, final=final),
        grid=(nt,),
        in_specs=[pl.BlockSpec((tm, d), lambda i: (i, 0)),
                  pl.BlockSpec((tm, d), lambda i: (i, 0)),
                  pl.BlockSpec((tm, d), lambda i: (i + nt, 0)),
                  pl.BlockSpec((8, tm), lambda i: (0, i)),
                  pl.BlockSpec((1, 1, d), lambda i: (i // per_seq, 0, 0)),
                  pl.BlockSpec((1, d), lambda i: (0, 0))],
        out_specs=pl.BlockSpec((tm, d), lambda i: (i, 0)),
        out_shape=jax.ShapeDtypeStruct((n, d), F32),
        compiler_params=_cparams(("parallel",)),
        name="ffn_residual",
    )(x, o_all, o_all, rgate, gate, g.reshape(1, d))


def _pad_cols(w, width):
    return jnp.pad(w, ((0, 0), (0, width - w.shape[1])))


def _pad_rows(w, height):
    return jnp.pad(w, ((0, height - w.shape[0]), (0, 0)))


def kernel(x, c, positions, router_w, router_bias, ada_w, ada_b, norm_mix, norm_ffn, moe_w1, moe_w3, moe_w2, ab_w_in, ab_sinks, ab_conv_w, ab_conv_b, ab_conv_ln_g, ab_conv_ln_b, ab_w_out, cd_w_in, cd_shift_mu, cd_w0, cd_w2, cd_a0, cd_a2, cd_g2, cd_k_k, cd_k_a, cd_r_k, cd_ln_x_g, cd_ln_x_b, cd_lru_conv_w, cd_lru_conv_b, cd_lru_wa, cd_lru_ba, cd_lru_wx, cd_lru_bx, cd_lru_lambda, cd_w_out, final_norm):
    batch, seq, d = x.shape
    n = batch * seq
    depth = ada_w.shape[0]
    xf = x.reshape(n, d)
    mod = _ada_mod(c, ada_w, ada_b).reshape(depth, batch, 6, 1, d)
    cos_t, sin_t = _rope_tables(positions)
    att_q = ATT_HEADS * HEAD_DIM
    att_kv = ATT_KV_HEADS * HEAD_DIM
    tm = 256

    for layer in range(depth):
        j = layer // 2
        shift_m, scale_m, gate_m, shift_f, scale_f, gate_f = (mod[layer, :, q] for q in range(6))
        if layer % 2 == 0:
            w_in = ab_w_in[j]
            w_perm = jnp.concatenate([w_in[:, :att_q], w_in[:, att_q + 2 * att_kv:],
                                      w_in[:, att_q:att_q + 2 * att_kv]], axis=1).astype(BF16)
            z = _in_proj(xf, norm_mix[layer], shift_m, scale_m, w_perm, seq, 256, 512)
            conv_ch = ab_conv_w.shape[-1]
            kv0 = (att_q + 2 * conv_ch) // att_kv
            a1 = _attention(z, cos_t, sin_t, ab_sinks[j], batch, seq, kv0, kv0 + 1)
            a2 = _conformer_conv(z, ab_conv_w[j], ab_conv_b[j], ab_conv_ln_g[j], ab_conv_ln_b[j],
                                 batch, seq, att_q // conv_ch, att_q // conv_ch + 1, 128)
            w_out = ab_w_out[j]
        else:
            w_in = cd_w_in[j]
            dim = cd_w0.shape[-1]
            dl = cd_w2.shape[1]
            al = cd_a2.shape[1]
            gl = cd_g2.shape[1]
            s0 = 3 * dim
            seg_zw = w_in[:, s0:s0 + dl]
            seg_za = w_in[:, s0 + dl:s0 + dl + al]
            seg_zg = w_in[:, s0 + dl + al:s0 + dl + al + gl]
            s1 = s0 + dl + al + gl
            w_perm = jnp.concatenate([w_in[:, :s0], w_in[:, s1:], _pad_cols(seg_zw, LANES),
                                      _pad_cols(seg_za, LANES), seg_zg], axis=1).astype(BF16)
            mu = cd_shift_mu[j]
            mu_lo = jnp.concatenate([jnp.pad(mu[s0:s0 + dl], (0, LANES - dl)),
                                     jnp.pad(mu[s0 + dl:s0 + dl + al], (0, LANES - al)),
                                     mu[s0 + dl + al:s1]]).reshape(1, -1)
            row = lambda a: a.reshape(1, -1)
            p = dict(mu_r=row(mu[:dim]), mu_k=row(mu[dim:2 * dim]), mu_v=row(mu[2 * dim:s0]), mu_lo=mu_lo,
                     w0=row(cd_w0[j]), w2=_pad_rows(cd_w2[j], LANES).astype(BF16),
                     a0=row(cd_a0[j]), a2=_pad_rows(cd_a2[j], LANES).astype(BF16), g2=cd_g2[j].astype(BF16),
                     k_k=row(cd_k_k[j]), k_a=row(cd_k_a[j]), r_k=row(cd_r_k[j]),
                     ln_g=row(cd_ln_x_g[j]), ln_b=row(cd_ln_x_b[j]))
            z = _in_proj(xf, norm_mix[layer], shift_m, scale_m, w_perm, seq, 256, 512)
            lo_col = (5 * dim) // (4 * LANES)
            a1 = _rwkv(z, p, batch, seq, lo_col)
            a2 = _rglru(z, cd_lru_conv_w[j], cd_lru_conv_b[j], cd_lru_wa[j], cd_lru_ba[j],
                        cd_lru_wx[j], cd_lru_bx[j], cd_lru_lambda[j], batch, seq, 3, 4, 256)
            w_out = cd_w_out[j]
        x1, h, ridx, rgate, counts = _out_proj(a1, a2, w_out, xf, gate_m, norm_ffn[layer], shift_f, scale_f,
                                               router_w, router_bias, seq, tm)
        block_e, row_tok, row_dst = _moe_plan(ridx, counts[:, 0], n)
        o_all = _moe_experts(h, block_e, row_tok, row_dst, moe_w1[layer].astype(BF16),
                             moe_w3[layer].astype(BF16), moe_w2[layer].astype(BF16))
        xf = _residual(x1, o_all, rgate, gate_f, final_norm, seq, 512, layer == depth - 1)
    return xf.reshape(batch, seq, d)
```

```python
import functools

import jax
import jax.numpy as jnp
from jax import lax
from jax.experimental import pallas as pl
from jax.experimental.pallas import tpu as pltpu

F32 = jnp.float32
BF16 = jnp.bfloat16

EPS = 1e-6
LN_EPS = 1e-5
RWKV_LN_EPS = 64e-5
HEAD_DIM = 64
LANES = 128
ATT_BLOCK = 128
WINDOW = 128
ATT_HEADS = 16
ATT_KV_HEADS = 4
CONV_WIDTH = 31
CONV_HALO = 32
LRU_CONV_WIDTH = 4
LRU_C = 8.0
N_EXPERTS = 16
N_GROUPS = 4
EXPERTS_PER_GROUP = 4
MOE_BLOCK = 256
MOE_FCHUNK = 256
RWKV_CHUNK = 64
RWKV_ROWS = 128
ROPE_THETA = 10000.0
VMEM_LIMIT = 56 * 1024 * 1024


def _cparams(sem):
    return pltpu.CompilerParams(dimension_semantics=sem, vmem_limit_bytes=VMEM_LIMIT)


def _bdot(a, b):
    return jnp.dot(a.astype(BF16), b.astype(BF16), preferred_element_type=F32)


def _split3(x):
    hi = x.astype(BF16)
    r1 = x - hi.astype(F32)
    mid = r1.astype(BF16)
    lo = (r1 - mid.astype(F32)).astype(BF16)
    return hi, mid, lo


def _dot_left_exact(m_bf16, x):
    hi, mid, lo = _split3(x)
    d = functools.partial(jnp.dot, preferred_element_type=F32)
    return d(m_bf16, hi) + d(m_bf16, mid) + d(m_bf16, lo)


def _dot_right_exact(x, m_bf16):
    hi, mid, lo = _split3(x)
    d = functools.partial(jnp.dot, preferred_element_type=F32)
    return d(hi, m_bf16) + d(mid, m_bf16) + d(lo, m_bf16)


def _dot_3pass(a, b):
    a_hi = a.astype(BF16)
    a_lo = (a - a_hi.astype(F32)).astype(BF16)
    b_hi = b.astype(BF16)
    b_lo = (b - b_hi.astype(F32)).astype(BF16)
    d = functools.partial(jnp.dot, preferred_element_type=F32)
    return d(a_hi, b_hi) + d(a_hi, b_lo) + d(a_lo, b_hi)


def _sigmoid(x):
    return 1.0 / (1.0 + jnp.exp(-x))


def _softplus(x):
    return jnp.maximum(x, 0.0) + jnp.log(1.0 + jnp.exp(-jnp.abs(x)))


def _rms_mod(x, g, shift, scale):
    ms = jnp.mean(x * x, axis=-1, keepdims=True)
    return (x * lax.rsqrt(ms + EPS) * g) * (1.0 + scale) + shift


def _ada_kernel(c_ref, w_ref, b_ref, o_ref):
    c = c_ref[...]
    o_ref[0] = _dot_3pass(c * _sigmoid(c), w_ref[0]) + b_ref[0]


def _ada_mod(c, ada_w, ada_b):
    depth, d, n6 = ada_w.shape
    b = c.shape[0]
    bp = 8
    tn = 768
    c_pad = jnp.zeros((bp, d), F32).at[:b].set(c)
    out = pl.pallas_call(
        _ada_kernel,
        grid=(depth, n6 // tn),
        in_specs=[pl.BlockSpec((bp, d), lambda l, j: (0, 0)),
                  pl.BlockSpec((1, d, tn), lambda l, j: (l, 0, j)),
                  pl.BlockSpec((1, 1, tn), lambda l, j: (l, 0, j))],
        out_specs=pl.BlockSpec((1, bp, tn), lambda l, j: (l, 0, j)),
        out_shape=jax.ShapeDtypeStruct((depth, bp, n6), F32),
        compiler_params=_cparams(("parallel", "parallel")),
        name="ada_mod",
    )(c_pad, ada_w, ada_b.reshape(depth, 1, n6))
    return out[:, :b]


def _rope_kernel(pos_ref, f_ref, cos_ref, sin_ref):
    ang = pos_ref[...].astype(F32) * f_ref[...]
    lane = lax.broadcasted_iota(jnp.int32, ang.shape, 1)
    s = jnp.sin(ang)
    cos_ref[...] = jnp.cos(ang)
    sin_ref[...] = jnp.where((lane & 32) == 0, -s, s)


def _rope_tables(positions):
    n = positions.size
    half = HEAD_DIM // 2
    inv_freq = ROPE_THETA ** (-jnp.arange(half, dtype=F32) / half)
    f_row = jnp.tile(inv_freq, LANES // half).reshape(1, LANES)
    tm = 512
    return pl.pallas_call(
        _rope_kernel,
        grid=(n // tm,),
        in_specs=[pl.BlockSpec((tm, 1), lambda i: (i, 0)),
                  pl.BlockSpec((1, LANES), lambda i: (0, 0))],
        out_specs=[pl.BlockSpec((tm, LANES), lambda i: (i, 0))] * 2,
        out_shape=[jax.ShapeDtypeStruct((n, LANES), F32)] * 2,
        compiler_params=_cparams(("parallel",)),
        name="rope_tables",
    )(positions.reshape(n, 1), f_row)


def _in_proj_kernel(x_ref, g_ref, sh_ref, sc_ref, w_ref, o_ref, *, tn):
    h = _rms_mod(x_ref[...], g_ref[...], sh_ref[0], sc_ref[0]).astype(BF16)
    for c in range(o_ref.shape[1] // tn):
        o_ref[:, c * tn:(c + 1) * tn] = jnp.dot(h, w_ref[:, c * tn:(c + 1) * tn], preferred_element_type=F32)


def _in_proj(x, g, shift, scale, w_bf16, seq, tm, tn):
    n, d = x.shape
    nout = w_bf16.shape[1]
    per_seq = seq // tm
    return pl.pallas_call(
        functools.partial(_in_proj_kernel, tn=tn),
        grid=(n // tm,),
        in_specs=[pl.BlockSpec((tm, d), lambda i: (i, 0)),
                  pl.BlockSpec((1, d), lambda i: (0, 0)),
                  pl.BlockSpec((1, 1, d), lambda i: (i // per_seq, 0, 0)),
                  pl.BlockSpec((1, 1, d), lambda i: (i // per_seq, 0, 0)),
                  pl.BlockSpec((d, nout), lambda i: (0, 0), pipeline_mode=pl.Buffered(1))],
        out_specs=pl.BlockSpec((tm, nout), lambda i: (i, 0)),
        out_shape=jax.ShapeDtypeStruct((n, nout), F32),
        compiler_params=_cparams(("parallel",)),
        name="in_proj",
    )(x, g.reshape(1, d), shift, scale, w_bf16)


def _route(logits_t, bias_col):
    m = jnp.max(logits_t, axis=0, keepdims=True)
    e = jnp.exp(logits_t - m)
    probs = e / jnp.sum(e, axis=0, keepdims=True)
    sel = probs + bias_col
    s = [sel[i:i + 1, :] for i in range(N_EXPERTS)]
    pr = [probs[i:i + 1, :] for i in range(N_EXPERTS)]

    def top2_sum(v):
        best = v[0] + v[1]
        for i in range(len(v)):
            for j in range(i + 1, len(v)):
                if (i, j) != (0, 1):
                    best = jnp.maximum(best, v[i] + v[j])
        return best

    gs = [top2_sum(s[4 * g:4 * g + 4]) for g in range(N_GROUPS)]
    best = gs[0]
    gi = jnp.zeros_like(best, dtype=jnp.int32)
    for g in range(1, N_GROUPS):
        upd = gs[g] > best
        gi = jnp.where(upd, g, gi)
        best = jnp.where(upd, gs[g], best)

    def pick_group(rows):
        out = []
        for l in range(EXPERTS_PER_GROUP):
            v = rows[l]
            for g in range(1, N_GROUPS):
                v = jnp.where(gi == g, rows[4 * g + l], v)
            out.append(v)
        return out

    ig = pick_group(s)
    pg = pick_group(pr)

    def argmax4(v):
        bv = v[0]
        bi = jnp.zeros_like(gi)
        for l in range(1, EXPERTS_PER_GROUP):
            upd = v[l] > bv
            bi = jnp.where(upd, l, bi)
            bv = jnp.where(upd, v[l], bv)
        return bi

    i1 = argmax4(ig)
    i2 = argmax4([jnp.where(i1 == l, -jnp.inf, ig[l]) for l in range(EXPERTS_PER_GROUP)])

    def pick(v, idx):
        o = v[0]
        for l in range(1, EXPERTS_PER_GROUP):
            o = jnp.where(idx == l, v[l], o)
        return o

    p1 = pick(pg, i1)
    p2 = pick(pg, i2)
    tot = p1 + p2
    return gi * EXPERTS_PER_GROUP + i1, gi * EXPERTS_PER_GROUP + i2, p1 / tot, p2 / tot


def _out_proj_kernel(a1_ref, a2_ref, w1_ref, w2_ref, x_ref, gate_ref, g_ref, sh_ref, sc_ref,
                     rw_ref, rb_ref, upper_ref, xo_ref, h_ref, ridx_ref, rgate_ref, cnt_ref, carry):
    @pl.when(pl.program_id(0) == 0)
    def _():
        carry[...] = jnp.zeros_like(carry)

    y = (jnp.dot(a1_ref[...], w1_ref[...], preferred_element_type=F32)
         + jnp.dot(a2_ref[...], w2_ref[...], preferred_element_type=F32))
    x1 = x_ref[...] + gate_ref[0] * y
    xo_ref[...] = x1
    h = _rms_mod(x1, g_ref[...], sh_ref[0], sc_ref[0])
    h_ref[...] = h
    logits = _dot_3pass(h, rw_ref[...])
    logits_t = logits.T[:N_EXPERTS, :]
    e1, e2, g1, g2 = _route(logits_t, rb_ref[...])
    erow = lax.broadcasted_iota(jnp.int32, logits_t.shape, 0)
    hit1 = erow == e1
    hit2 = erow == e2
    onehot = jnp.where(hit1 | hit2, 1.0, 0.0)
    before = jnp.dot(onehot.astype(BF16), upper_ref[...], preferred_element_type=F32) + carry[:, 0:1]
    rank1 = jnp.sum(jnp.where(hit1, before, 0.0), axis=0, keepdims=True).astype(jnp.int32)
    rank2 = jnp.sum(jnp.where(hit2, before, 0.0), axis=0, keepdims=True).astype(jnp.int32)
    total = carry[...] + jnp.sum(onehot, axis=1, keepdims=True)
    carry[...] = total
    cnt_ref[...] = total.astype(jnp.int32)
    row = lax.broadcasted_iota(jnp.int32, ridx_ref.shape, 0)
    ridx_ref[...] = jnp.where(row == 0, e1, jnp.where(row == 1, e2,
                              jnp.where(row == 2, rank1, jnp.where(row == 3, rank2, 0))))
    rgate_ref[...] = jnp.where(row == 0, g1, jnp.where(row == 1, g2, 0.0))


def _out_proj(a1, a2, w_out, x, gate, g, shift, scale, router_w, router_bias, seq, tm):
    n, d = x.shape
    ka = a1.shape[1]
    per_seq = seq // tm
    w1 = w_out[:ka].astype(BF16)
    w2 = w_out[ka:].astype(BF16)
    rw = jnp.zeros((d, LANES), F32).at[:, :N_EXPERTS].set(router_w)
    rb = router_bias.astype(F32).reshape(N_EXPERTS, 1)
    upper = (jnp.arange(tm)[:, None] < jnp.arange(tm)[None, :]).astype(BF16)
    mod_spec = pl.BlockSpec((1, 1, d), lambda i: (i // per_seq, 0, 0))
    return pl.pallas_call(
        _out_proj_kernel,
        grid=(n // tm,),
        in_specs=[pl.BlockSpec((tm, ka), lambda i: (i, 0)),
                  pl.BlockSpec((tm, ka), lambda i: (i, 0)),
                  pl.BlockSpec((ka, d), lambda i: (0, 0), pipeline_mode=pl.Buffered(1)),
                  pl.BlockSpec((ka, d), lambda i: (0, 0), pipeline_mode=pl.Buffered(1)),
                  pl.BlockSpec((tm, d), lambda i: (i, 0)),
                  mod_spec,
                  pl.BlockSpec((1, d), lambda i: (0, 0)),
                  mod_spec, mod_spec,
                  pl.BlockSpec((d, LANES), lambda i: (0, 0)),
                  pl.BlockSpec((N_EXPERTS, 1), lambda i: (0, 0)),
                  pl.BlockSpec((tm, tm), lambda i: (0, 0))],
        out_specs=[pl.BlockSpec((tm, d), lambda i: (i, 0)),
                   pl.BlockSpec((tm, d), lambda i: (i, 0)),
                   pl.BlockSpec((8, tm), lambda i: (0, i)),
                   pl.BlockSpec((8, tm), lambda i: (0, i)),
                   pl.BlockSpec((N_EXPERTS, LANES), lambda i: (0, 0))],
        out_shape=[jax.ShapeDtypeStruct((n, d), F32),
                   jax.ShapeDtypeStruct((n, d), F32),
                   jax.ShapeDtypeStruct((8, n), jnp.int32),
                   jax.ShapeDtypeStruct((8, n), F32),
                   jax.ShapeDtypeStruct((N_EXPERTS, LANES), jnp.int32)],
        scratch_shapes=[pltpu.VMEM((N_EXPERTS, LANES), F32)],
        compiler_params=_cparams(("arbitrary",)),
        name="out_proj_route",
    )(a1, a2, w1, w2, x, gate, g.reshape(1, d), shift, scale, rw, rb, upper)


def _rope_apply(x, cos, sin_signed):
    outs = []
    lane = lax.broadcasted_iota(jnp.int32, cos.shape, 1)
    first_half = (lane & 32) == 0
    for c in range(x.shape[1] // LANES):
        xc = x[:, c * LANES:(c + 1) * LANES]
        partner = jnp.where(first_half, pltpu.roll(xc, LANES - 32, axis=1), pltpu.roll(xc, 32, axis=1))
        outs.append(xc * cos + partner * sin_signed)
    return outs


def _attn_kernel(sink_ref, q_ref, kp_ref, kc_ref, vp_ref, vc_ref, cq_ref, sq_ref, cp_ref, sp_ref, o_ref):
    has_prev = pl.program_id(1) > 0
    cq, sq = cq_ref[...], sq_ref[...]
    q_cols = _rope_apply(q_ref[...] * (HEAD_DIM ** -0.5), cq, sq)
    k_cols = [jnp.concatenate([kp, kc], axis=0) for kp, kc in
              zip(_rope_apply(kp_ref[...], cp_ref[...], sp_ref[...]), _rope_apply(kc_ref[...], cq, sq))]
    v_all = jnp.concatenate([vp_ref[...], vc_ref[...]], axis=0).astype(BF16)
    qi = lax.broadcasted_iota(jnp.int32, (ATT_BLOCK, 2 * ATT_BLOCK), 0)
    kj = lax.broadcasted_iota(jnp.int32, (ATT_BLOCK, 2 * ATT_BLOCK), 1)
    valid = ((kj > qi) & (kj < ATT_BLOCK) & has_prev) | ((kj >= ATT_BLOCK) & (kj - ATT_BLOCK <= qi))
    group = ATT_HEADS // ATT_KV_HEADS
    heads = range(ATT_HEADS)
    scores = []
    for hq in heads:
        hk = hq // group
        qh = q_cols[hq // 2][:, (hq % 2) * HEAD_DIM:(hq % 2 + 1) * HEAD_DIM].astype(BF16)
        kh = k_cols[hk // 2][:, (hk % 2) * HEAD_DIM:(hk % 2 + 1) * HEAD_DIM].astype(BF16)
        s = lax.dot_general(qh, kh, (((1,), (1,)), ((), ())), preferred_element_type=F32)
        scores.append(jnp.where(valid, s, -1e30))
    tops = [jnp.maximum(jnp.max(scores[hq], axis=-1, keepdims=True), sink_ref[hq]) for hq in heads]
    probs = [jnp.exp(scores[hq] - tops[hq]) for hq in heads]
    dens = [jnp.sum(probs[hq], axis=-1, keepdims=True) + jnp.exp(sink_ref[hq] - tops[hq]) for hq in heads]
    outs = [jnp.dot(probs[hq].astype(BF16), v_all[:, (hq // group) * HEAD_DIM:(hq // group + 1) * HEAD_DIM],
                    preferred_element_type=F32) / dens[hq] for hq in heads]
    out_cols = [jnp.concatenate([outs[2 * pair], outs[2 * pair + 1]], axis=1) for pair in range(ATT_HEADS // 2)]
    o_ref[...] = jnp.concatenate(out_cols, axis=1).astype(o_ref.dtype)


def _attention(z, cos_t, sin_t, sinks, batch, seq, k_col, v_col):
    n = z.shape[0]
    nb = seq // ATT_BLOCK
    att_q = ATT_HEADS * HEAD_DIM
    att_kv = ATT_KV_HEADS * HEAD_DIM
    cur = lambda b, i: b * nb + i
    prev = lambda b, i: b * nb + jnp.maximum(i - 1, 0)
    return pl.pallas_call(
        _attn_kernel,
        grid=(batch, nb),
        in_specs=[pl.BlockSpec(memory_space=pltpu.SMEM),
                  pl.BlockSpec((ATT_BLOCK, att_q), lambda b, i: (cur(b, i), 0)),
                  pl.BlockSpec((ATT_BLOCK, att_kv), lambda b, i: (prev(b, i), k_col)),
                  pl.BlockSpec((ATT_BLOCK, att_kv), lambda b, i: (cur(b, i), k_col)),
                  pl.BlockSpec((ATT_BLOCK, att_kv), lambda b, i: (prev(b, i), v_col)),
                  pl.BlockSpec((ATT_BLOCK, att_kv), lambda b, i: (cur(b, i), v_col)),
                  pl.BlockSpec((ATT_BLOCK, LANES), lambda b, i: (cur(b, i), 0)),
                  pl.BlockSpec((ATT_BLOCK, LANES), lambda b, i: (cur(b, i), 0)),
                  pl.BlockSpec((ATT_BLOCK, LANES), lambda b, i: (prev(b, i), 0)),
                  pl.BlockSpec((ATT_BLOCK, LANES), lambda b, i: (prev(b, i), 0))],
        out_specs=pl.BlockSpec((ATT_BLOCK, att_q), lambda b, i: (cur(b, i), 0)),
        out_shape=jax.ShapeDtypeStruct((n, att_q), BF16),
        compiler_params=_cparams(("parallel", "parallel")),
        name="swa_attention",
    )(sinks.astype(F32), z, z, z, z, z, cos_t, sin_t, cos_t, sin_t)


def _conv_kernel(val_ref, gate_ref, hval_ref, hgate_ref, w_ref, b_ref, g_ref, beta_ref, o_ref, ybuf, acc, shifted):
    t = val_ref.shape[0]
    ch = val_ref.shape[1]
    first = pl.program_id(1) == 0
    yh = hval_ref[...] * _sigmoid(hgate_ref[...])
    ybuf[0:CONV_HALO, :] = jnp.where(first, 0.0, yh)
    ybuf[CONV_HALO:, :] = val_ref[...] * _sigmoid(gate_ref[...])
    off = CONV_HALO - (CONV_WIDTH - 1)
    span = t + CONV_HALO - 8
    for ph in range(1, 8):
        shifted[ph - 1] = ybuf[ph:ph + span, :]
    rt = 64
    for cb in range(ch // LANES):
        cs = slice(cb * LANES, (cb + 1) * LANES)
        for r0 in range(0, t, rt):
            a = jnp.broadcast_to(b_ref[:, cs], (rt, LANES))
            for j in range(CONV_WIDTH):
                ph, base = (off + j) % 8, (off + j) // 8 * 8
                src = ybuf if ph == 0 else shifted.at[ph - 1]
                a = a + w_ref[j:j + 1, cs] * src[r0 + base:r0 + base + rt, cs]
            acc[r0:r0 + rt, cs] = a
    y = acc[...]
    mu = jnp.mean(y, axis=-1, keepdims=True)
    yc = y - mu
    var = jnp.mean(yc * yc, axis=-1, keepdims=True)
    ln = yc * lax.rsqrt(var + LN_EPS) * g_ref[...] + beta_ref[...]
    o_ref[...] = (ln * _sigmoid(ln)).astype(o_ref.dtype)


def _conformer_conv(z, conv_w, conv_b, ln_g, ln_b, batch, seq, val_col, gate_col, t):
    n = z.shape[0]
    ch = conv_w.shape[-1]
    nt = seq // t
    hpt = t // CONV_HALO
    w = jnp.zeros((CONV_HALO, ch), F32).at[:CONV_WIDTH].set(conv_w.reshape(CONV_WIDTH, ch))
    cur = lambda b, i: b * nt + i
    halo = lambda b, i: jnp.maximum((b * nt + i) * hpt - 1, 0)
    vec = pl.BlockSpec((1, ch), lambda b, i: (0, 0))
    return pl.pallas_call(
        _conv_kernel,
        grid=(batch, nt),
        in_specs=[pl.BlockSpec((t, ch), lambda b, i: (cur(b, i), val_col)),
                  pl.BlockSpec((t, ch), lambda b, i: (cur(b, i), gate_col)),
                  pl.BlockSpec((CONV_HALO, ch), lambda b, i: (halo(b, i), val_col)),
                  pl.BlockSpec((CONV_HALO, ch), lambda b, i: (halo(b, i), gate_col)),
                  pl.BlockSpec((CONV_HALO, ch), lambda b, i: (0, 0)),
                  vec, vec, vec],
        out_specs=pl.BlockSpec((t, ch), lambda b, i: (cur(b, i), 0)),
        out_shape=jax.ShapeDtypeStruct((n, ch), BF16),
        scratch_shapes=[pltpu.VMEM((t + CONV_HALO, ch), F32), pltpu.VMEM((t, ch), F32),
                        pltpu.VMEM((7, t + CONV_HALO - 8, ch), F32)],
        compiler_params=_cparams(("parallel", "parallel")),
        name="conformer_conv",
    )(z, z, z, z, w, conv_b.reshape(1, ch), ln_g.reshape(1, ch), ln_b.reshape(1, ch))


def _shift_lerp(z, halo, mu, first):
    prev = pltpu.roll(z, 1, axis=0)
    row = lax.broadcasted_iota(jnp.int32, z.shape, 0)
    hrow = jnp.where(first, 0.0, halo[7:8, :])
    prev = jnp.where(row == 0, hrow, prev)
    return z + mu * (prev - z)


def _rwkv_kernel(r_ref, k_ref, v_ref, lo_ref, rh_ref, kh_ref, vh_ref, loh_ref,
                 mur_ref, muk_ref, muv_ref, mulo_ref, w0_ref, w2_ref, a0_ref, a2_ref, g2_ref,
                 kk_ref, ka_ref, rk_ref, lng_ref, lnb_ref, seg_ref, segt_ref, tril_ref,
                 o_ref, hs_ref):
    first = pl.program_id(1) == 0

    @pl.when(first)
    def _():
        hs_ref[...] = jnp.zeros_like(hs_ref)

    r = _shift_lerp(r_ref[...], rh_ref[...], mur_ref[...], first)
    k = _shift_lerp(k_ref[...], kh_ref[...], muk_ref[...], first)
    v = _shift_lerp(v_ref[...], vh_ref[...], muv_ref[...], first)
    lo = _shift_lerp(lo_ref[...], loh_ref[...], mulo_ref[...], first)
    zw, za, zg = lo[:, 0:LANES], lo[:, LANES:2 * LANES], lo[:, 2 * LANES:]

    seg = seg_ref[...]
    segt = segt_ref[...]

    def head_sum(x):
        return _dot_right_exact(_dot_right_exact(x, seg), segt)

    w_raw = w0_ref[...] + _bdot(jnp.tanh(zw), w2_ref[...])
    log_decay = -jnp.exp(-_softplus(-w_raw) - 0.5)
    a_gate = _sigmoid(a0_ref[...] + _bdot(za, a2_ref[...]))
    g = _bdot(_sigmoid(zg), g2_ref[...])
    kk = k * kk_ref[...]
    kk = kk * lax.rsqrt(jnp.maximum(head_sum(kk * kk), 1e-24))
    k_mod = k * (1.0 + (a_gate - 1.0) * ka_ref[...])
    b_vec = kk * a_gate

    n_chunks = r.shape[0] // RWKV_CHUNK
    rows = [slice(c * RWKV_CHUNK, (c + 1) * RWKV_CHUNK) for c in range(n_chunks)]
    cum = _dot_left_exact(tril_ref[...], log_decay)
    cum_last = [cum[rs.stop - 1:rs.stop, :] for rs in rows]
    w_last = [jnp.exp(cl) for cl in cum_last]
    inv_w = jnp.exp(-cum)
    to_end = jnp.concatenate([jnp.exp(cum_last[c] - cum[rows[c], :]) for c in range(n_chunks)], axis=0)
    a_til = -kk * jnp.exp(cum - log_decay)
    r_til = r * jnp.exp(cum)
    b_til = b_vec * inv_w
    k_til = k_mod * inv_w
    b_hat = b_vec * to_end
    k_hat = k_mod * to_end

    lane = lax.broadcasted_iota(jnp.int32, (RWKV_CHUNK, LANES), 1)
    m0 = lane < HEAD_DIM
    ri = lax.broadcasted_iota(jnp.int32, (LANES, LANES), 0)
    ci = lax.broadcasted_iota(jnp.int32, (LANES, LANES), 1)
    same = (ri < HEAD_DIM) == (ci < HEAD_DIM)
    mask_sl = same & (ci < ri)
    mask_li = same & (ci <= ri)
    eye = ri == ci

    def stack(x):
        return jnp.concatenate([jnp.where(m0, x, 0.0), jnp.where(m0, 0.0, x)], axis=0)

    nt_dims = (((1,), (1,)), ((), ()))
    dot = functools.partial(jnp.dot, preferred_element_type=F32)
    n_pairs = r.shape[1] // LANES
    cols = [slice(p * LANES, (p + 1) * LANES) for p in range(n_pairs)]
    units = [(rows[c], cols[p]) for c in range(n_chunks) for p in range(n_pairs)]
    ids = range(len(units))
    xa_f = [stack(a_til[u]) for u in units]
    xr_f = [stack(r_til[u]) for u in units]
    vs = [stack(v[u]).astype(BF16) for u in units]
    m = [lax.dot_general(jnp.concatenate([xa_f[i], xr_f[i]], axis=0).astype(BF16),
                         jnp.concatenate([b_til[units[i]]] * 2 + [k_til[units[i]]] * 2, axis=0).astype(BF16),
                         nt_dims, preferred_element_type=F32) for i in ids]
    power = [jnp.where(mask_sl, m[i][:LANES, :LANES], 0.0).astype(BF16) for i in ids]
    a_ak = [jnp.where(mask_sl, m[i][:LANES, LANES:], 0.0).astype(BF16) for i in ids]
    z = [jnp.concatenate([xa_f[i], dot(a_ak[i], vs[i])], axis=1) for i in ids]
    for level in range(6):
        z = [z[i] + dot(power[i], z[i].astype(BF16)) for i in ids]
        if level < 5:
            power = [dot(power[i], power[i]).astype(BF16) for i in ids]
    zb = [z[i].astype(BF16) for i in ids]
    m_rb = [jnp.where(mask_li, m[i][LANES:, :LANES], 0.0).astype(BF16) for i in ids]
    m_rk = [jnp.where(mask_li, m[i][LANES:, LANES:], 0.0).astype(BF16) for i in ids]
    q = [jnp.concatenate([xr_f[i], dot(m_rk[i], vs[i])], axis=1) + dot(m_rb[i], zb[i]) for i in ids]
    ge = [dot(stack(b_hat[units[i]]).T.astype(BF16), zb[i]) for i in ids]
    e_mat = [ge[i][:, LANES:] + dot(stack(k_hat[units[i]]).T.astype(BF16), vs[i]) for i in ids]
    qg = [jnp.concatenate([q[i][:, :LANES],
                           ge[i][:, :LANES] + jnp.where(eye, w_last[i // n_pairs][:, cols[i % n_pairs]], 0.0)],
                          axis=0).astype(BF16) for i in ids]
    state = [hs_ref[p] for p in range(n_pairs)]
    y_rows = []
    for c in range(n_chunks):
        y_cols = []
        for p in range(n_pairs):
            i = c * n_pairs + p
            yh = dot(qg[i], state[p].astype(BF16))
            state[p] = yh[LANES:] + e_mat[i]
            y2 = yh[:LANES] + q[i][:, LANES:]
            y_cols.append(y2[:RWKV_CHUNK] + y2[RWKV_CHUNK:])
        y_rows.append(jnp.concatenate(y_cols, axis=1))
    for p in range(n_pairs):
        hs_ref[p] = state[p]
    y = jnp.concatenate(y_rows, axis=0)

    inv_hd = 1.0 / HEAD_DIM
    mu = head_sum(y) * inv_hd
    yc = y - mu
    var = head_sum(yc * yc) * inv_hd
    yn = yc * lax.rsqrt(var + RWKV_LN_EPS) * lng_ref[...] + lnb_ref[...]
    bonus = head_sum(r * k_mod * rk_ref[...]) * v
    o_ref[...] = ((yn + bonus) * g).astype(o_ref.dtype)


def _rwkv(z, p, batch, seq, lo_col):
    n = z.shape[0]
    dim = p["w0"].shape[-1]
    t = RWKV_ROWS
    nt = seq // t
    hpt = t // 8
    lo_w = 4 * LANES
    cur = lambda b, i: b * nt + i
    halo = lambda b, i: jnp.maximum((b * nt + i) * hpt - 1, 0)
    heads = dim // HEAD_DIM
    seg = (jnp.arange(dim)[:, None] // HEAD_DIM == jnp.arange(heads)[None, :]).astype(BF16)
    ti = jnp.arange(t)
    tril = ((ti[:, None] >= ti[None, :]) & (ti[:, None] // RWKV_CHUNK == ti[None, :] // RWKV_CHUNK)).astype(BF16)

    def full(a):
        return pl.BlockSpec(a.shape, lambda b, i: (0,) * a.ndim)

    consts = [p["mu_r"], p["mu_k"], p["mu_v"], p["mu_lo"], p["w0"], p["w2"], p["a0"], p["a2"], p["g2"],
              p["k_k"], p["k_a"], p["r_k"], p["ln_g"], p["ln_b"], seg, seg.T, tril]
    col = lambda c: pl.BlockSpec((t, dim), lambda b, i: (cur(b, i), c))
    hcol = lambda c: pl.BlockSpec((8, dim), lambda b, i: (halo(b, i), c))
    return pl.pallas_call(
        _rwkv_kernel,
        grid=(batch, nt),
        in_specs=[col(0), col(1), col(2), pl.BlockSpec((t, lo_w), lambda b, i: (cur(b, i), lo_col)),
                  hcol(0), hcol(1), hcol(2), pl.BlockSpec((8, lo_w), lambda b, i: (halo(b, i), lo_col))]
                 + [full(a) for a in consts],
        out_specs=pl.BlockSpec((t, dim), lambda b, i: (cur(b, i), 0)),
        out_shape=jax.ShapeDtypeStruct((n, dim), BF16),
        scratch_shapes=[pltpu.VMEM((dim // LANES, LANES, LANES), F32)],
        compiler_params=_cparams(("parallel", "arbitrary")),
        name="rwkv7_chunked",
    )(z, z, z, z, z, z, z, z, *consts)


def _lru_kernel(x_ref, gb_ref, xh_ref, cw_ref, cb_ref, wg_ref, ba_ref, bx_ref, lam_ref, o_ref,
                xbuf, carry):
    t, ch = x_ref.shape
    first = pl.program_id(1) == 0

    @pl.when(first)
    def _():
        carry[...] = jnp.zeros_like(carry)

    xbuf[0:8, :] = jnp.where(first, 0.0, xh_ref[...])
    xbuf[8:, :] = x_ref[...]
    off = 8 - (LRU_CONV_WIDTH - 1)
    xc = jnp.broadcast_to(cb_ref[...], (t, ch))
    for j in range(LRU_CONV_WIDTH):
        xc = xc + cw_ref[j:j + 1, :] * xbuf[off + j:off + j + t, :]
    gr, gi = [], []
    for p in range(ch // LANES):
        gp = _bdot(xc[:, p * LANES:(p + 1) * LANES], wg_ref[p])
        gr.append(gp[:, :LANES])
        gi.append(gp[:, LANES:])
    r = _sigmoid(jnp.concatenate(gr, axis=1) + ba_ref[...])
    i = _sigmoid(jnp.concatenate(gi, axis=1) + bx_ref[...])
    log_a = -LRU_C * r * _softplus(-lam_ref[...])
    a = jnp.exp(log_a)
    u = jnp.sqrt(1.0 - jnp.exp(2.0 * log_a)) * (i * xc)
    row = lax.broadcasted_iota(jnp.int32, (t, ch), 0)
    d = 1
    while d < t:
        keep = row >= d
        a_sh = jnp.where(keep, pltpu.roll(a, d, axis=0), 1.0)
        u_sh = jnp.where(keep, pltpu.roll(u, d, axis=0), 0.0)
        u = a * u_sh + u
        a = a * a_sh
        d *= 2
    h = u + a * carry[...]
    carry[...] = h[t - 1:t, :]
    gb = gb_ref[...]
    gelu = 0.5 * gb * (1.0 + jnp.tanh(0.7978845608028654 * (gb + 0.044715 * (gb * gb * gb))))
    o_ref[...] = (h * gelu).astype(o_ref.dtype)


def _rglru(z, conv_w, conv_b, wa, ba, wx, bx, lam, batch, seq, x_col, g_col, t):
    n = z.shape[0]
    ch = lam.shape[-1]
    nt = seq // t
    hpt = t // 8
    nblk, bd = wa.shape[0], wa.shape[1]
    per = LANES // bd
    def blockdiag(w):
        w = w.reshape(nblk // per, per, bd, bd)
        out = jnp.zeros((nblk // per, LANES, LANES), F32)
        for q in range(per):
            out = out.at[:, q * bd:(q + 1) * bd, q * bd:(q + 1) * bd].set(w[:, q])
        return out
    wg = jnp.concatenate([blockdiag(wa), blockdiag(wx)], axis=-1).astype(BF16)
    cw = jnp.zeros((8, ch), F32).at[:LRU_CONV_WIDTH].set(conv_w.reshape(LRU_CONV_WIDTH, ch))
    cur = lambda b, i: b * nt + i
    halo = lambda b, i: jnp.maximum((b * nt + i) * hpt - 1, 0)
    vec = pl.BlockSpec((1, ch), lambda b, i: (0, 0))
    return pl.pallas_call(
        _lru_kernel,
        grid=(batch, nt),
        in_specs=[pl.BlockSpec((t, ch), lambda b, i: (cur(b, i), x_col)),
                  pl.BlockSpec((t, ch), lambda b, i: (cur(b, i), g_col)),
                  pl.BlockSpec((8, ch), lambda b, i: (halo(b, i), x_col)),
                  pl.BlockSpec((8, ch), lambda b, i: (0, 0)),
                  vec,
                  pl.BlockSpec(wg.shape, lambda b, i: (0, 0, 0)),
                  vec, vec, vec],
        out_specs=pl.BlockSpec((t, ch), lambda b, i: (cur(b, i), 0)),
        out_shape=jax.ShapeDtypeStruct((n, ch), BF16),
        scratch_shapes=[pltpu.VMEM((t + 8, ch), F32), pltpu.VMEM((1, ch), F32)],
        compiler_params=_cparams(("parallel", "arbitrary")),
        name="rglru",
    )(z, z, z, cw, conv_b.reshape(1, ch), wg, ba.reshape(1, ch), bx.reshape(1, ch), lam.reshape(1, ch))


def _moe_kernel(be_ref, tok_ref, dst_ref, h_hbm, w1_ref, w3_ref, w2_ref, o_hbm, x0, x1, o0, o1, gsem, ssem):
    i = pl.program_id(0)
    last = pl.num_programs(0) - 1
    xs, outs = (x0, x1), (o0, o1)

    def gather_row(j, r, s):
        return pltpu.make_async_copy(h_hbm.at[pl.ds(tok_ref[j, r], 1)], xs[s].at[pl.ds(r, 1)], gsem.at[s])

    def scatter_row(j, r, s):
        return pltpu.make_async_copy(outs[s].at[pl.ds(r, 1)], o_hbm.at[pl.ds(dst_ref[j, r], 1)], ssem.at[s])

    def gather_block(s):
        return pltpu.make_async_copy(h_hbm.at[pl.ds(0, MOE_BLOCK)], xs[s], gsem.at[s])

    def scatter_block(s):
        return pltpu.make_async_copy(outs[s], o_hbm.at[pl.ds(0, MOE_BLOCK)], ssem.at[s])

    @pl.when(i == 0)
    def _():
        o1[...] = jnp.zeros_like(o1)
        for r in range(MOE_BLOCK):
            gather_row(0, r, 0).start()

    def step(s):
        t = 1 - s
        gather_block(s).wait()

        @pl.when(i >= 1)
        def _():
            scatter_block(s).wait()

        nxt = jnp.minimum(i + 1, last)
        prv = jnp.maximum(i - 1, 0)
        f = w1_ref.shape[-1]
        n_groups = f // MOE_FCHUNK
        per_group = MOE_BLOCK // n_groups
        for c in range(n_groups):
            @pl.when(i >= 0)
            def _():
                for r in range(c * per_group, (c + 1) * per_group):
                    gather_row(nxt, r, t).start()
                    scatter_row(prv, r, t).start()

            fs = slice(c * MOE_FCHUNK, (c + 1) * MOE_FCHUNK)
            x = xs[s][...].astype(BF16)
            h1 = jnp.dot(x, w1_ref[0, :, fs], preferred_element_type=F32)
            h3 = jnp.dot(x, w3_ref[0, :, fs], preferred_element_type=F32)
            hid = (h1 * _sigmoid(h1) * h3).astype(BF16)
            part = jnp.dot(hid, w2_ref[0, fs, :], preferred_element_type=F32)
            if c == 0:
                outs[s][...] = part
            else:
                outs[s][...] += part

        @pl.when(i == last)
        def _():
            for r in range(MOE_BLOCK):
                scatter_row(i, r, s).start()
            scatter_block(t).wait()
            scatter_block(s).wait()
            gather_block(t).wait()

    for s in range(2):
        pl.when(i % 2 == s)(functools.partial(step, s))


def _moe_experts(h, block_e, row_tok, row_dst, w1, w3, w2):
    n, d = h.shape
    f = w1.shape[-1]
    n_blocks = row_dst.shape[0]
    rows = n_blocks * MOE_BLOCK
    grid_spec = pltpu.PrefetchScalarGridSpec(
        num_scalar_prefetch=3,
        grid=(n_blocks,),
        in_specs=[pl.BlockSpec(memory_space=pl.ANY),
                  pl.BlockSpec((1, d, f), lambda i, be, tok, dst: (be[i], 0, 0)),
                  pl.BlockSpec((1, d, f), lambda i, be, tok, dst: (be[i], 0, 0)),
                  pl.BlockSpec((1, f, d), lambda i, be, tok, dst: (be[i], 0, 0))],
        out_specs=pl.BlockSpec(memory_space=pl.ANY),
        scratch_shapes=[pltpu.VMEM((MOE_BLOCK, d), F32)] * 4
                       + [pltpu.SemaphoreType.DMA((2,)), pltpu.SemaphoreType.DMA((2,))],
    )
    return pl.pallas_call(
        _moe_kernel,
        grid_spec=grid_spec,
        out_shape=jax.ShapeDtypeStruct((rows, d), F32),
        compiler_params=_cparams(("arbitrary",)),
        name="moe_experts",
    )(block_e, row_tok, row_dst, h, w1, w3, w2)


def _moe_plan(ridx, counts, n):
    nk = 2 * n
    e = ridx[0:2]
    rank = ridx[2:4]
    padded = (counts + MOE_BLOCK - 1) // MOE_BLOCK * MOE_BLOCK
    pend = jnp.cumsum(padded)
    pstart = pend - padded
    experts = jnp.arange(N_EXPERTS, dtype=jnp.int32)[:, None, None]
    dest = rank + jnp.sum(jnp.where(e[None] == experts, pstart[:, None, None], 0), axis=0)
    n_blocks = nk // MOE_BLOCK + N_EXPERTS
    rows = n_blocks * MOE_BLOCK
    result_row = jnp.arange(nk, dtype=jnp.int32)
    placed = jnp.full((rows,), -1, jnp.int32).at[dest.reshape(-1)].set(result_row)
    is_pad = placed < 0
    pad_rank = jnp.cumsum(is_pad.astype(jnp.int32)) - 1
    row_dst = jnp.where(is_pad, nk + pad_rank, placed)
    row_tok = jnp.where(is_pad, 0, jnp.where(placed >= n, placed - n, placed))
    block_start = jnp.arange(n_blocks, dtype=jnp.int32) * MOE_BLOCK
    block_e = jnp.minimum(jnp.sum(block_start[:, None] >= pend[None, :], axis=1), N_EXPERTS - 1).astype(jnp.int32)
    return block_e, row_tok.reshape(n_blocks, MOE_BLOCK), row_dst.reshape(n_blocks, MOE_BLOCK)


def _residual_kernel(x_ref, o1_ref, o2_ref, rg_ref, gate_ref, g_ref, out_ref, *, final):
    g = rg_ref[...].T
    y = g[:, 0:1] * o1_ref[...] + g[:, 1:2] * o2_ref[...]
    x2 = x_ref[...] + gate_ref[0] * y
    if final:
        ms = jnp.mean(x2 * x2, axis=-1, keepdims=True)
        x2 = x2 * lax.rsqrt(ms + EPS) * g_ref[...]
    out_ref[...] = x2


def _residual(x, o_all, rgate, gate, g, seq, tm, final):
    n, d = x.shape
    per_seq = seq // tm
    nt = n // tm
    return pl.pallas_call(
        functools.partial(_residual_kernel, final=final),
        grid=(nt,),
        in_specs=[pl.BlockSpec((tm, d), lambda i: (i, 0)),
                  pl.BlockSpec((tm, d), lambda i: (i, 0)),
                  pl.BlockSpec((tm, d), lambda i: (i + nt, 0)),
                  pl.BlockSpec((8, tm), lambda i: (0, i)),
                  pl.BlockSpec((1, 1, d), lambda i: (i // per_seq, 0, 0)),
                  pl.BlockSpec((1, d), lambda i: (0, 0))],
        out_specs=pl.BlockSpec((tm, d), lambda i: (i, 0)),
        out_shape=jax.ShapeDtypeStruct((n, d), F32),
        compiler_params=_cparams(("parallel",)),
        name="ffn_residual",
    )(x, o_all, o_all, rgate, gate, g.reshape(1, d))


def _pad_cols(w, width):
    return jnp.pad(w, ((0, 0), (0, width - w.shape[1])))


def _pad_rows(w, height):
    return jnp.pad(w, ((0, height - w.shape[0]), (0, 0)))


def kernel(x, c, positions, router_w, router_bias, ada_w, ada_b, norm_mix, norm_ffn, moe_w1, moe_w3, moe_w2, ab_w_in, ab_sinks, ab_conv_w, ab_conv_b, ab_conv_ln_g, ab_conv_ln_b, ab_w_out, cd_w_in, cd_shift_mu, cd_w0, cd_w2, cd_a0, cd_a2, cd_g2, cd_k_k, cd_k_a, cd_r_k, cd_ln_x_g, cd_ln_x_b, cd_lru_conv_w, cd_lru_conv_b, cd_lru_wa, cd_lru_ba, cd_lru_wx, cd_lru_bx, cd_lru_lambda, cd_w_out, final_norm):
    batch, seq, d = x.shape
    n = batch * seq
    depth = ada_w.shape[0]
    xf = x.reshape(n, d)
    mod = _ada_mod(c, ada_w, ada_b).reshape(depth, batch, 6, 1, d)
    cos_t, sin_t = _rope_tables(positions)
    att_q = ATT_HEADS * HEAD_DIM
    att_kv = ATT_KV_HEADS * HEAD_DIM
    tm = 256

    for layer in range(depth):
        j = layer // 2
        shift_m, scale_m, gate_m, shift_f, scale_f, gate_f = (mod[layer, :, q] for q in range(6))
        if layer % 2 == 0:
            w_in = ab_w_in[j]
            w_perm = jnp.concatenate([w_in[:, :att_q], w_in[:, att_q + 2 * att_kv:],
                                      w_in[:, att_q:att_q + 2 * att_kv]], axis=1).astype(BF16)
            z = _in_proj(xf, norm_mix[layer], shift_m, scale_m, w_perm, seq, 256, 512)
            conv_ch = ab_conv_w.shape[-1]
            kv0 = (att_q + 2 * conv_ch) // att_kv
            a1 = _attention(z, cos_t, sin_t, ab_sinks[j], batch, seq, kv0, kv0 + 1)
            a2 = _conformer_conv(z, ab_conv_w[j], ab_conv_b[j], ab_conv_ln_g[j], ab_conv_ln_b[j],
                                 batch, seq, att_q // conv_ch, att_q // conv_ch + 1, 128)
            w_out = ab_w_out[j]
        else:
            w_in = cd_w_in[j]
            dim = cd_w0.shape[-1]
            dl = cd_w2.shape[1]
            al = cd_a2.shape[1]
            gl = cd_g2.shape[1]
            s0 = 3 * dim
            seg_zw = w_in[:, s0:s0 + dl]
            seg_za = w_in[:, s0 + dl:s0 + dl + al]
            seg_zg = w_in[:, s0 + dl + al:s0 + dl + al + gl]
            s1 = s0 + dl + al + gl
            w_perm = jnp.concatenate([w_in[:, :s0], w_in[:, s1:], _pad_cols(seg_zw, LANES),
                                      _pad_cols(seg_za, LANES), seg_zg], axis=1).astype(BF16)
            mu = cd_shift_mu[j]
            mu_lo = jnp.concatenate([jnp.pad(mu[s0:s0 + dl], (0, LANES - dl)),
                                     jnp.pad(mu[s0 + dl:s0 + dl + al], (0, LANES - al)),
                                     mu[s0 + dl + al:s1]]).reshape(1, -1)
            row = lambda a: a.reshape(1, -1)
            p = dict(mu_r=row(mu[:dim]), mu_k=row(mu[dim:2 * dim]), mu_v=row(mu[2 * dim:s0]), mu_lo=mu_lo,
                     w0=row(cd_w0[j]), w2=_pad_rows(cd_w2[j], LANES).astype(BF16),
                     a0=row(cd_a0[j]), a2=_pad_rows(cd_a2[j], LANES).astype(BF16), g2=cd_g2[j].astype(BF16),
                     k_k=row(cd_k_k[j]), k_a=row(cd_k_a[j]), r_k=row(cd_r_k[j]),
                     ln_g=row(cd_ln_x_g[j]), ln_b=row(cd_ln_x_b[j]))
            z = _in_proj(xf, norm_mix[layer], shift_m, scale_m, w_perm, seq, 256, 512)
            lo_col = (5 * dim) // (4 * LANES)
            a1 = _rwkv(z, p, batch, seq, lo_col)
            a2 = _rglru(z, cd_lru_conv_w[j], cd_lru_conv_b[j], cd_lru_wa[j], cd_lru_ba[j],
                        cd_lru_wx[j], cd_lru_bx[j], cd_lru_lambda[j], batch, seq, 3, 4, 256)
            w_out = cd_w_out[j]
        x1, h, ridx, rgate, counts = _out_proj(a1, a2, w_out, xf, gate_m, norm_ffn[layer], shift_f, scale_f,
                                               router_w, router_bias, seq, tm)
        block_e, row_tok, row_dst = _moe_plan(ridx, counts[:, 0], n)
        o_all = _moe_experts(h, block_e, row_tok, row_dst, moe_w1[layer].astype(BF16),
                             moe_w3[layer].astype(BF16), moe_w2[layer].astype(BF16))
        xf = _residual(x1, o_all, rgate, gate_f, final_norm, seq, 512, layer == depth - 1)
    return xf.reshape(batch, seq, d)
```

```python
import functools

import jax
import jax.numpy as jnp
from jax import lax
from jax.experimental import pallas as pl
from jax.experimental.pallas import tpu as pltpu

F32 = jnp.float32
BF16 = jnp.bfloat16

EPS = 1e-6
LN_EPS = 1e-5
RWKV_LN_EPS = 64e-5
HEAD_DIM = 64
LANES = 128
MXU_TILE = 256
ATT_BLOCK = 128
WINDOW = 128
ATT_HEADS = 16
ATT_KV_HEADS = 4
CONV_WIDTH = 31
CONV_HALO = 32
LRU_CONV_WIDTH = 4
LRU_C = 8.0
N_EXPERTS = 16
N_GROUPS = 4
EXPERTS_PER_GROUP = 4
MOE_BLOCK = 256
MOE_FCHUNK = 256
RWKV_CHUNK = 64
RWKV_ROWS = 256
ROPE_THETA = 10000.0
VMEM_LIMIT = 56 * 1024 * 1024


def _cparams(sem):
    return pltpu.CompilerParams(dimension_semantics=sem, vmem_limit_bytes=VMEM_LIMIT)


def _bdot(a, b):
    return jnp.dot(a.astype(BF16), b.astype(BF16), preferred_element_type=F32)


def _split2(x):
    hi = x.astype(BF16)
    lo = (x - hi.astype(F32)).astype(BF16)
    return hi, lo


def _dot_left_exact(m_bf16, x):
    hi, lo = _split2(x)
    d = functools.partial(jnp.dot, preferred_element_type=F32)
    return d(m_bf16, hi) + d(m_bf16, lo)


def _dot_right_exact(x, m_bf16):
    hi, lo = _split2(x)
    d = functools.partial(jnp.dot, preferred_element_type=F32)
    return d(hi, m_bf16) + d(lo, m_bf16)


def _dot_3pass(a, b):
    a_hi = a.astype(BF16)
    a_lo = (a - a_hi.astype(F32)).astype(BF16)
    b_hi = b.astype(BF16)
    b_lo = (b - b_hi.astype(F32)).astype(BF16)
    d = functools.partial(jnp.dot, preferred_element_type=F32)
    return d(a_hi, b_hi) + d(a_hi, b_lo) + d(a_lo, b_hi)


def _sigmoid(x):
    return 1.0 / (1.0 + jnp.exp(-x))


def _softplus(x):
    return jnp.maximum(x, 0.0) + jnp.log(1.0 + jnp.exp(-jnp.abs(x)))


def _rms_mod(x, g, shift, scale):
    ms = jnp.mean(x * x, axis=-1, keepdims=True)
    return (x * lax.rsqrt(ms + EPS) * g) * (1.0 + scale) + shift


def _ada_kernel(c_ref, w_ref, b_ref, o_ref):
    c = c_ref[...]
    o_ref[0] = _dot_3pass(c * _sigmoid(c), w_ref[0]) + b_ref[0]


def _ada_mod(c, ada_w, ada_b):
    depth, d, n6 = ada_w.shape
    b = c.shape[0]
    bp = 8
    tn = 768
    c_pad = jnp.zeros((bp, d), F32).at[:b].set(c)
    out = pl.pallas_call(
        _ada_kernel,
        grid=(depth, n6 // tn),
        in_specs=[pl.BlockSpec((bp, d), lambda l, j: (0, 0)),
                  pl.BlockSpec((1, d, tn), lambda l, j: (l, 0, j)),
                  pl.BlockSpec((1, 1, tn), lambda l, j: (l, 0, j))],
        out_specs=pl.BlockSpec((1, bp, tn), lambda l, j: (l, 0, j)),
        out_shape=jax.ShapeDtypeStruct((depth, bp, n6), F32),
        compiler_params=_cparams(("parallel", "parallel")),
        name="ada_mod",
    )(c_pad, ada_w, ada_b.reshape(depth, 1, n6))
    return out[:, :b]


def _rope_kernel(pos_ref, f_ref, cos_ref, sin_ref):
    ang = pos_ref[...].astype(F32) * f_ref[...]
    lane = lax.broadcasted_iota(jnp.int32, ang.shape, 1)
    s = jnp.sin(ang)
    cos_ref[...] = jnp.cos(ang)
    sin_ref[...] = jnp.where((lane & 32) == 0, -s, s)


def _rope_tables(positions):
    n = positions.size
    half = HEAD_DIM // 2
    inv_freq = ROPE_THETA ** (-jnp.arange(half, dtype=F32) / half)
    f_row = jnp.tile(inv_freq, LANES // half).reshape(1, LANES)
    tm = 512
    return pl.pallas_call(
        _rope_kernel,
        grid=(n // tm,),
        in_specs=[pl.BlockSpec((tm, 1), lambda i: (i, 0)),
                  pl.BlockSpec((1, LANES), lambda i: (0, 0))],
        out_specs=[pl.BlockSpec((tm, LANES), lambda i: (i, 0))] * 2,
        out_shape=[jax.ShapeDtypeStruct((n, LANES), F32)] * 2,
        compiler_params=_cparams(("parallel",)),
        name="rope_tables",
    )(positions.reshape(n, 1), f_row)


def _in_proj_kernel(x_ref, g_ref, sh_ref, sc_ref, w_ref, o_ref, *, tn):
    h = _rms_mod(x_ref[...], g_ref[...], sh_ref[0], sc_ref[0]).astype(BF16)
    for c in range(o_ref.shape[1] // tn):
        o_ref[:, c * tn:(c + 1) * tn] = jnp.dot(h, w_ref[:, c * tn:(c + 1) * tn], preferred_element_type=F32)


def _in_proj(x, g, shift, scale, w_bf16, seq, tm, tn):
    n, d = x.shape
    nout = w_bf16.shape[1]
    per_seq = seq // tm
    return pl.pallas_call(
        functools.partial(_in_proj_kernel, tn=tn),
        grid=(n // tm,),
        in_specs=[pl.BlockSpec((tm, d), lambda i: (i, 0)),
                  pl.BlockSpec((1, d), lambda i: (0, 0)),
                  pl.BlockSpec((1, 1, d), lambda i: (i // per_seq, 0, 0)),
                  pl.BlockSpec((1, 1, d), lambda i: (i // per_seq, 0, 0)),
                  pl.BlockSpec((d, nout), lambda i: (0, 0), pipeline_mode=pl.Buffered(1))],
        out_specs=pl.BlockSpec((tm, nout), lambda i: (i, 0)),
        out_shape=jax.ShapeDtypeStruct((n, nout), F32),
        compiler_params=_cparams(("parallel",)),
        name="in_proj",
    )(x, g.reshape(1, d), shift, scale, w_bf16)


def _route(logits_t, bias_col):
    m = jnp.max(logits_t, axis=0, keepdims=True)
    e = jnp.exp(logits_t - m)
    probs = e / jnp.sum(e, axis=0, keepdims=True)
    sel = probs + bias_col
    s = [sel[i:i + 1, :] for i in range(N_EXPERTS)]
    pr = [probs[i:i + 1, :] for i in range(N_EXPERTS)]

    def top2_sum(v):
        best = v[0] + v[1]
        for i in range(len(v)):
            for j in range(i + 1, len(v)):
                if (i, j) != (0, 1):
                    best = jnp.maximum(best, v[i] + v[j])
        return best

    gs = [top2_sum(s[4 * g:4 * g + 4]) for g in range(N_GROUPS)]
    best = gs[0]
    gi = jnp.zeros_like(best, dtype=jnp.int32)
    for g in range(1, N_GROUPS):
        upd = gs[g] > best
        gi = jnp.where(upd, g, gi)
        best = jnp.where(upd, gs[g], best)

    def pick_group(rows):
        out = []
        for l in range(EXPERTS_PER_GROUP):
            v = rows[l]
            for g in range(1, N_GROUPS):
                v = jnp.where(gi == g, rows[4 * g + l], v)
            out.append(v)
        return out

    ig = pick_group(s)
    pg = pick_group(pr)

    def argmax4(v):
        bv = v[0]
        bi = jnp.zeros_like(gi)
        for l in range(1, EXPERTS_PER_GROUP):
            upd = v[l] > bv
            bi = jnp.where(upd, l, bi)
            bv = jnp.where(upd, v[l], bv)
        return bi

    i1 = argmax4(ig)
    i2 = argmax4([jnp.where(i1 == l, -jnp.inf, ig[l]) for l in range(EXPERTS_PER_GROUP)])

    def pick(v, idx):
        o = v[0]
        for l in range(1, EXPERTS_PER_GROUP):
            o = jnp.where(idx == l, v[l], o)
        return o

    p1 = pick(pg, i1)
    p2 = pick(pg, i2)
    tot = p1 + p2
    return gi * EXPERTS_PER_GROUP + i1, gi * EXPERTS_PER_GROUP + i2, p1 / tot, p2 / tot


def _out_proj_kernel(a1_ref, a2_ref, w1_ref, w2_ref, x_ref, gate_ref, g_ref, sh_ref, sc_ref,
                     rw_ref, rb_ref, upper_ref, xo_ref, h_ref, ridx_ref, rgate_ref, cnt_ref, carry):
    @pl.when(pl.program_id(0) == 0)
    def _():
        carry[...] = jnp.zeros_like(carry)

    y = (jnp.dot(a1_ref[...], w1_ref[...], preferred_element_type=F32)
         + jnp.dot(a2_ref[...], w2_ref[...], preferred_element_type=F32))
    x1 = x_ref[...] + gate_ref[0] * y
    xo_ref[...] = x1
    h = _rms_mod(x1, g_ref[...], sh_ref[0], sc_ref[0])
    h_ref[...] = h
    logits = _dot_3pass(h, rw_ref[...])
    logits_t = logits.T[:N_EXPERTS, :]
    e1, e2, g1, g2 = _route(logits_t, rb_ref[...])
    erow = lax.broadcasted_iota(jnp.int32, logits_t.shape, 0)
    hit1 = erow == e1
    hit2 = erow == e2
    onehot = jnp.where(hit1 | hit2, 1.0, 0.0)
    before = jnp.dot(onehot.astype(BF16), upper_ref[...], preferred_element_type=F32) + carry[:, 0:1]
    rank1 = jnp.sum(jnp.where(hit1, before, 0.0), axis=0, keepdims=True).astype(jnp.int32)
    rank2 = jnp.sum(jnp.where(hit2, before, 0.0), axis=0, keepdims=True).astype(jnp.int32)
    total = carry[...] + jnp.sum(onehot, axis=1, keepdims=True)
    carry[...] = total
    cnt_ref[...] = total.astype(jnp.int32)
    row = lax.broadcasted_iota(jnp.int32, ridx_ref.shape, 0)
    ridx_ref[...] = jnp.where(row == 0, e1, jnp.where(row == 1, e2,
                              jnp.where(row == 2, rank1, jnp.where(row == 3, rank2, 0))))
    rgate_ref[...] = jnp.where(row == 0, g1, jnp.where(row == 1, g2, 0.0))


def _out_proj(a1, a2, w_out, x, gate, g, shift, scale, router_w, router_bias, seq, tm):
    n, d = x.shape
    ka = a1.shape[1]
    per_seq = seq // tm
    w1 = w_out[:ka].astype(BF16)
    w2 = w_out[ka:].astype(BF16)
    rw = jnp.zeros((d, LANES), F32).at[:, :N_EXPERTS].set(router_w)
    rb = router_bias.astype(F32).reshape(N_EXPERTS, 1)
    upper = (jnp.arange(tm)[:, None] < jnp.arange(tm)[None, :]).astype(BF16)
    mod_spec = pl.BlockSpec((1, 1, d), lambda i: (i // per_seq, 0, 0))
    return pl.pallas_call(
        _out_proj_kernel,
        grid=(n // tm,),
        in_specs=[pl.BlockSpec((tm, ka), lambda i: (i, 0)),
                  pl.BlockSpec((tm, ka), lambda i: (i, 0)),
                  pl.BlockSpec((ka, d), lambda i: (0, 0), pipeline_mode=pl.Buffered(1)),
                  pl.BlockSpec((ka, d), lambda i: (0, 0), pipeline_mode=pl.Buffered(1)),
                  pl.BlockSpec((tm, d), lambda i: (i, 0)),
                  mod_spec,
                  pl.BlockSpec((1, d), lambda i: (0, 0)),
                  mod_spec, mod_spec,
                  pl.BlockSpec((d, LANES), lambda i: (0, 0)),
                  pl.BlockSpec((N_EXPERTS, 1), lambda i: (0, 0)),
                  pl.BlockSpec((tm, tm), lambda i: (0, 0))],
        out_specs=[pl.BlockSpec((tm, d), lambda i: (i, 0)),
                   pl.BlockSpec((tm, d), lambda i: (i, 0)),
                   pl.BlockSpec((8, tm), lambda i: (0, i)),
                   pl.BlockSpec((8, tm), lambda i: (0, i)),
                   pl.BlockSpec((N_EXPERTS, LANES), lambda i: (0, 0))],
        out_shape=[jax.ShapeDtypeStruct((n, d), F32),
                   jax.ShapeDtypeStruct((n, d), F32),
                   jax.ShapeDtypeStruct((8, n), jnp.int32),
                   jax.ShapeDtypeStruct((8, n), F32),
                   jax.ShapeDtypeStruct((N_EXPERTS, LANES), jnp.int32)],
        scratch_shapes=[pltpu.VMEM((N_EXPERTS, LANES), F32)],
        compiler_params=_cparams(("arbitrary",)),
        name="out_proj_route",
    )(a1, a2, w1, w2, x, gate, g.reshape(1, d), shift, scale, rw, rb, upper)


def _rope_apply(x, cos, sin_signed):
    outs = []
    lane = lax.broadcasted_iota(jnp.int32, cos.shape, 1)
    first_half = (lane & 32) == 0
    for c in range(x.shape[1] // LANES):
        xc = x[:, c * LANES:(c + 1) * LANES]
        partner = jnp.where(first_half, pltpu.roll(xc, LANES - 32, axis=1), pltpu.roll(xc, 32, axis=1))
        outs.append(xc * cos + partner * sin_signed)
    return outs


def _attn_kernel(sink_ref, q_ref, kp_ref, kc_ref, vp_ref, vc_ref, cq_ref, sq_ref, cp_ref, sp_ref, o_ref):
    has_prev = pl.program_id(1) > 0
    cq, sq = cq_ref[...], sq_ref[...]
    q_cols = _rope_apply(q_ref[...] * (HEAD_DIM ** -0.5), cq, sq)
    k_cols = [jnp.concatenate([kp, kc], axis=0) for kp, kc in
              zip(_rope_apply(kp_ref[...], cp_ref[...], sp_ref[...]), _rope_apply(kc_ref[...], cq, sq))]
    v_all = jnp.concatenate([vp_ref[...], vc_ref[...]], axis=0).astype(BF16)
    qi = lax.broadcasted_iota(jnp.int32, (ATT_BLOCK, 2 * ATT_BLOCK), 0)
    kj = lax.broadcasted_iota(jnp.int32, (ATT_BLOCK, 2 * ATT_BLOCK), 1)
    valid = ((kj > qi) & (kj < ATT_BLOCK) & has_prev) | ((kj >= ATT_BLOCK) & (kj - ATT_BLOCK <= qi))
    group = ATT_HEADS // ATT_KV_HEADS
    heads = range(ATT_HEADS)
    scores = []
    for hq in heads:
        hk = hq // group
        qh = q_cols[hq // 2][:, (hq % 2) * HEAD_DIM:(hq % 2 + 1) * HEAD_DIM].astype(BF16)
        kh = k_cols[hk // 2][:, (hk % 2) * HEAD_DIM:(hk % 2 + 1) * HEAD_DIM].astype(BF16)
        s = lax.dot_general(qh, kh, (((1,), (1,)), ((), ())), preferred_element_type=F32)
        scores.append(jnp.where(valid, s, -1e30))
    tops = [jnp.maximum(jnp.max(scores[hq], axis=-1, keepdims=True), sink_ref[hq]) for hq in heads]
    probs = [jnp.exp(scores[hq] - tops[hq]) for hq in heads]
    dens = [jnp.sum(probs[hq], axis=-1, keepdims=True) + jnp.exp(sink_ref[hq] - tops[hq]) for hq in heads]
    outs = [jnp.dot(probs[hq].astype(BF16), v_all[:, (hq // group) * HEAD_DIM:(hq // group + 1) * HEAD_DIM],
                    preferred_element_type=F32) / dens[hq] for hq in heads]
    out_cols = [jnp.concatenate([outs[2 * pair], outs[2 * pair + 1]], axis=1) for pair in range(ATT_HEADS // 2)]
    o_ref[...] = jnp.concatenate(out_cols, axis=1).astype(o_ref.dtype)


def _attention(z, cos_t, sin_t, sinks, batch, seq, k_col, v_col):
    n = z.shape[0]
    nb = seq // ATT_BLOCK
    att_q = ATT_HEADS * HEAD_DIM
    att_kv = ATT_KV_HEADS * HEAD_DIM
    cur = lambda b, i: b * nb + i
    prev = lambda b, i: b * nb + jnp.maximum(i - 1, 0)
    return pl.pallas_call(
        _attn_kernel,
        grid=(batch, nb),
        in_specs=[pl.BlockSpec(memory_space=pltpu.SMEM),
                  pl.BlockSpec((ATT_BLOCK, att_q), lambda b, i: (cur(b, i), 0)),
                  pl.BlockSpec((ATT_BLOCK, att_kv), lambda b, i: (prev(b, i), k_col)),
                  pl.BlockSpec((ATT_BLOCK, att_kv), lambda b, i: (cur(b, i), k_col)),
                  pl.BlockSpec((ATT_BLOCK, att_kv), lambda b, i: (prev(b, i), v_col)),
                  pl.BlockSpec((ATT_BLOCK, att_kv), lambda b, i: (cur(b, i), v_col)),
                  pl.BlockSpec((ATT_BLOCK, LANES), lambda b, i: (cur(b, i), 0)),
                  pl.BlockSpec((ATT_BLOCK, LANES), lambda b, i: (cur(b, i), 0)),
                  pl.BlockSpec((ATT_BLOCK, LANES), lambda b, i: (prev(b, i), 0)),
                  pl.BlockSpec((ATT_BLOCK, LANES), lambda b, i: (prev(b, i), 0))],
        out_specs=pl.BlockSpec((ATT_BLOCK, att_q), lambda b, i: (cur(b, i), 0)),
        out_shape=jax.ShapeDtypeStruct((n, att_q), BF16),
        compiler_params=_cparams(("parallel", "parallel")),
        name="swa_attention",
    )(sinks.astype(F32), z, z, z, z, z, cos_t, sin_t, cos_t, sin_t)


def _conv_kernel(val_ref, gate_ref, hval_ref, hgate_ref, w_ref, b_ref, g_ref, beta_ref, o_ref, ybuf, acc, shifted):
    t = val_ref.shape[0]
    ch = val_ref.shape[1]
    first = pl.program_id(1) == 0
    yh = hval_ref[...] * _sigmoid(hgate_ref[...])
    ybuf[0:CONV_HALO, :] = jnp.where(first, 0.0, yh)
    ybuf[CONV_HALO:, :] = val_ref[...] * _sigmoid(gate_ref[...])
    off = CONV_HALO - (CONV_WIDTH - 1)
    span = t + CONV_HALO - 8
    for ph in range(1, 8):
        shifted[ph - 1] = ybuf[ph:ph + span, :]
    for cb in range(ch // LANES):
        cs = slice(cb * LANES, (cb + 1) * LANES)
        a = jnp.broadcast_to(b_ref[:, cs], (t, LANES))
        for j in range(CONV_WIDTH):
            ph, base = (off + j) % 8, (off + j) // 8 * 8
            src = ybuf if ph == 0 else shifted.at[ph - 1]
            wj = jnp.broadcast_to(w_ref[j:j + 1, cs], (t, LANES))
            a = a + wj * src[base:base + t, cs]
        acc[:, cs] = a
    y = acc[...]
    mu = jnp.mean(y, axis=-1, keepdims=True)
    yc = y - mu
    var = jnp.mean(yc * yc, axis=-1, keepdims=True)
    ln = yc * lax.rsqrt(var + LN_EPS) * g_ref[...] + beta_ref[...]
    o_ref[...] = (ln * _sigmoid(ln)).astype(o_ref.dtype)


def _conformer_conv(z, conv_w, conv_b, ln_g, ln_b, batch, seq, val_col, gate_col, t):
    n = z.shape[0]
    ch = conv_w.shape[-1]
    nt = seq // t
    hpt = t // CONV_HALO
    w = jnp.zeros((CONV_HALO, ch), F32).at[:CONV_WIDTH].set(conv_w.reshape(CONV_WIDTH, ch))
    cur = lambda b, i: b * nt + i
    halo = lambda b, i: jnp.maximum((b * nt + i) * hpt - 1, 0)
    vec = pl.BlockSpec((1, ch), lambda b, i: (0, 0))
    return pl.pallas_call(
        _conv_kernel,
        grid=(batch, nt),
        in_specs=[pl.BlockSpec((t, ch), lambda b, i: (cur(b, i), val_col)),
                  pl.BlockSpec((t, ch), lambda b, i: (cur(b, i), gate_col)),
                  pl.BlockSpec((CONV_HALO, ch), lambda b, i: (halo(b, i), val_col)),
                  pl.BlockSpec((CONV_HALO, ch), lambda b, i: (halo(b, i), gate_col)),
                  pl.BlockSpec((CONV_HALO, ch), lambda b, i: (0, 0)),
                  vec, vec, vec],
        out_specs=pl.BlockSpec((t, ch), lambda b, i: (cur(b, i), 0)),
        out_shape=jax.ShapeDtypeStruct((n, ch), BF16),
        scratch_shapes=[pltpu.VMEM((t + CONV_HALO, ch), F32), pltpu.VMEM((t, ch), F32),
                        pltpu.VMEM((7, t + CONV_HALO - 8, ch), F32)],
        compiler_params=_cparams(("parallel", "parallel")),
        name="conformer_conv",
    )(z, z, z, z, w, conv_b.reshape(1, ch), ln_g.reshape(1, ch), ln_b.reshape(1, ch))


def _shift_lerp(z, halo, mu, first):
    prev = pltpu.roll(z, 1, axis=0)
    row = lax.broadcasted_iota(jnp.int32, z.shape, 0)
    hrow = jnp.where(first, 0.0, halo[7:8, :])
    prev = jnp.where(row == 0, hrow, prev)
    return z + mu * (prev - z)


def _rwkv_kernel(r_ref, k_ref, v_ref, lo_ref, rh_ref, kh_ref, vh_ref, loh_ref,
                 mur_ref, muk_ref, muv_ref, mulo_ref, w0_ref, w2_ref, a0_ref, a2_ref, g2_ref,
                 kk_ref, ka_ref, rk_ref, lng_ref, lnb_ref, seg_ref, tril_ref,
                 o_ref, hs_ref):
    first = pl.program_id(1) == 0

    @pl.when(first)
    def _():
        hs_ref[...] = jnp.zeros_like(hs_ref)

    r = _shift_lerp(r_ref[...], rh_ref[...], mur_ref[...], first)
    k = _shift_lerp(k_ref[...], kh_ref[...], muk_ref[...], first)
    v = _shift_lerp(v_ref[...], vh_ref[...], muv_ref[...], first)
    lo = _shift_lerp(lo_ref[...], loh_ref[...], mulo_ref[...], first)
    zw, za, zg = lo[:, 0:LANES], lo[:, LANES:2 * LANES], lo[:, 2 * LANES:]

    seg = seg_ref[...]
    seg_w = seg.shape[0]

    def head_sum(x):
        return jnp.concatenate([_dot_right_exact(x[:, c:c + seg_w], seg) for c in range(0, x.shape[1], seg_w)],
                               axis=1)

    w_raw = w0_ref[...] + _bdot(jnp.tanh(zw), w2_ref[...])
    log_decay = -jnp.exp(-_softplus(-w_raw) - 0.5)
    a_gate = _sigmoid(a0_ref[...] + _bdot(za, a2_ref[...]))
    g = _bdot(_sigmoid(zg), g2_ref[...])
    kk = k * kk_ref[...]
    kk = kk * lax.rsqrt(jnp.maximum(head_sum(kk * kk), 1e-24))
    k_mod = k * (1.0 + (a_gate - 1.0) * ka_ref[...])
    b_vec = kk * a_gate

    n_chunks = r.shape[0] // RWKV_CHUNK
    rows = [slice(c * RWKV_CHUNK, (c + 1) * RWKV_CHUNK) for c in range(n_chunks)]
    cum = _dot_left_exact(tril_ref[...], log_decay)
    cum_last = [cum[rs.stop - 1:rs.stop, :] for rs in rows]
    w_last = [jnp.exp(cl) for cl in cum_last]
    inv_w = jnp.exp(-cum)
    to_end = jnp.concatenate([jnp.exp(cum_last[c] - cum[rows[c], :]) for c in range(n_chunks)], axis=0)
    a_til = -kk * jnp.exp(cum - log_decay)
    r_til = r * jnp.exp(cum)
    b_til = b_vec * inv_w
    k_til = k_mod * inv_w
    b_hat = b_vec * to_end
    k_hat = k_mod * to_end

    lane = lax.broadcasted_iota(jnp.int32, (RWKV_CHUNK, LANES), 1)
    m0 = lane < HEAD_DIM
    ri = lax.broadcasted_iota(jnp.int32, (LANES, LANES), 0)
    ci = lax.broadcasted_iota(jnp.int32, (LANES, LANES), 1)
    same = (ri < HEAD_DIM) == (ci < HEAD_DIM)
    mask_sl = same & (ci < ri)
    mask_li = same & (ci <= ri)
    eye = ri == ci

    def stack(x):
        return jnp.concatenate([jnp.where(m0, x, 0.0), jnp.where(m0, 0.0, x)], axis=0)

    nt_dims = (((1,), (1,)), ((), ()))
    dot = functools.partial(jnp.dot, preferred_element_type=F32)
    n_pairs = r.shape[1] // LANES
    cols = [slice(p * LANES, (p + 1) * LANES) for p in range(n_pairs)]
    units = [(rows[c], cols[p]) for c in range(n_chunks) for p in range(n_pairs)]
    ids = range(len(units))
    xa_f = [stack(a_til[u]) for u in units]
    xr_f = [stack(r_til[u]) for u in units]
    vs = [stack(v[u]).astype(BF16) for u in units]
    m = [lax.dot_general(jnp.concatenate([xa_f[i], xr_f[i]], axis=0).astype(BF16),
                         jnp.concatenate([b_til[units[i]]] * 2 + [k_til[units[i]]] * 2, axis=0).astype(BF16),
                         nt_dims, preferred_element_type=F32) for i in ids]
    power = [jnp.where(mask_sl, m[i][:LANES, :LANES], 0.0).astype(BF16) for i in ids]
    a_ak = [jnp.where(mask_sl, m[i][:LANES, LANES:], 0.0).astype(BF16) for i in ids]
    z = [jnp.concatenate([xa_f[i], dot(a_ak[i], vs[i])], axis=1) for i in ids]
    for level in range(6):
        z = [z[i] + dot(power[i], z[i].astype(BF16)) for i in ids]
        if level < 5:
            power = [dot(power[i], power[i]).astype(BF16) for i in ids]
    zb = [z[i].astype(BF16) for i in ids]
    m_rb = [jnp.where(mask_li, m[i][LANES:, :LANES], 0.0).astype(BF16) for i in ids]
    m_rk = [jnp.where(mask_li, m[i][LANES:, LANES:], 0.0).astype(BF16) for i in ids]
    q = [jnp.concatenate([xr_f[i], dot(m_rk[i], vs[i])], axis=1) + dot(m_rb[i], zb[i]) for i in ids]
    ge = [dot(stack(b_hat[units[i]]).T.astype(BF16), zb[i]) for i in ids]
    e_mat = [ge[i][:, LANES:] + dot(stack(k_hat[units[i]]).T.astype(BF16), vs[i]) for i in ids]
    qg = [jnp.concatenate([q[i][:, :LANES],
                           ge[i][:, :LANES] + jnp.where(eye, w_last[i // n_pairs][:, cols[i % n_pairs]], 0.0)],
                          axis=0).astype(BF16) for i in ids]
    state = [hs_ref[p] for p in range(n_pairs)]
    y_rows = []
    for c in range(n_chunks):
        y_cols = []
        for p in range(n_pairs):
            i = c * n_pairs + p
            yh = dot(qg[i], state[p].astype(BF16))
            state[p] = yh[LANES:] + e_mat[i]
            y2 = yh[:LANES] + q[i][:, LANES:]
            y_cols.append(y2[:RWKV_CHUNK] + y2[RWKV_CHUNK:])
        y_rows.append(jnp.concatenate(y_cols, axis=1))
    for p in range(n_pairs):
        hs_ref[p] = state[p]
    y = jnp.concatenate(y_rows, axis=0)

    inv_hd = 1.0 / HEAD_DIM
    mu = head_sum(y) * inv_hd
    yc = y - mu
    var = head_sum(yc * yc) * inv_hd
    yn = yc * lax.rsqrt(var + RWKV_LN_EPS) * lng_ref[...] + lnb_ref[...]
    bonus = head_sum(r * k_mod * rk_ref[...]) * v
    o_ref[...] = ((yn + bonus) * g).astype(o_ref.dtype)


def _rwkv(z, p, batch, seq, lo_col):
    n = z.shape[0]
    dim = p["w0"].shape[-1]
    t = RWKV_ROWS
    nt = seq // t
    hpt = t // 8
    lo_w = 4 * LANES
    cur = lambda b, i: b * nt + i
    halo = lambda b, i: jnp.maximum((b * nt + i) * hpt - 1, 0)
    si = jnp.arange(MXU_TILE)
    seg = (si[:, None] // HEAD_DIM == si[None, :] // HEAD_DIM).astype(BF16)
    ti = jnp.arange(t)
    tril = ((ti[:, None] >= ti[None, :]) & (ti[:, None] // RWKV_CHUNK == ti[None, :] // RWKV_CHUNK)).astype(BF16)

    def full(a):
        return pl.BlockSpec(a.shape, lambda b, i: (0,) * a.ndim)

    consts = [p["mu_r"], p["mu_k"], p["mu_v"], p["mu_lo"], p["w0"], p["w2"], p["a0"], p["a2"], p["g2"],
              p["k_k"], p["k_a"], p["r_k"], p["ln_g"], p["ln_b"], seg, tril]
    col = lambda c: pl.BlockSpec((t, dim), lambda b, i: (cur(b, i), c))
    hcol = lambda c: pl.BlockSpec((8, dim), lambda b, i: (halo(b, i), c))
    return pl.pallas_call(
        _rwkv_kernel,
        grid=(batch, nt),
        in_specs=[col(0), col(1), col(2), pl.BlockSpec((t, lo_w), lambda b, i: (cur(b, i), lo_col)),
                  hcol(0), hcol(1), hcol(2), pl.BlockSpec((8, lo_w), lambda b, i: (halo(b, i), lo_col))]
                 + [full(a) for a in consts],
        out_specs=pl.BlockSpec((t, dim), lambda b, i: (cur(b, i), 0)),
        out_shape=jax.ShapeDtypeStruct((n, dim), BF16),
        scratch_shapes=[pltpu.VMEM((dim // LANES, LANES, LANES), F32)],
        compiler_params=_cparams(("parallel", "arbitrary")),
        name="rwkv7_chunked",
    )(z, z, z, z, z, z, z, z, *consts)


def _lru_kernel(x_ref, gb_ref, xh_ref, cw_ref, cb_ref, wg_ref, ba_ref, bx_ref, lam_ref, o_ref,
                xbuf, carry):
    t, ch = x_ref.shape
    first = pl.program_id(1) == 0

    @pl.when(first)
    def _():
        carry[...] = jnp.zeros_like(carry)

    xbuf[0:8, :] = jnp.where(first, 0.0, xh_ref[...])
    xbuf[8:, :] = x_ref[...]
    off = 8 - (LRU_CONV_WIDTH - 1)
    xc = jnp.broadcast_to(cb_ref[...], (t, ch))
    for j in range(LRU_CONV_WIDTH):
        xc = xc + cw_ref[j:j + 1, :] * xbuf[off + j:off + j + t, :]
    gr, gi = [], []
    for p in range(ch // LANES):
        gp = _bdot(xc[:, p * LANES:(p + 1) * LANES], wg_ref[p])
        gr.append(gp[:, :LANES])
        gi.append(gp[:, LANES:])
    r = _sigmoid(jnp.concatenate(gr, axis=1) + ba_ref[...])
    i = _sigmoid(jnp.concatenate(gi, axis=1) + bx_ref[...])
    log_a = -LRU_C * r * _softplus(-lam_ref[...])
    a = jnp.exp(log_a)
    u = jnp.sqrt(1.0 - jnp.exp(2.0 * log_a)) * (i * xc)
    a = a.reshape(t // 8, 8, ch)
    u = u.reshape(t // 8, 8, ch)
    row = lax.broadcasted_iota(jnp.int32, a.shape, 1)
    d = 1
    while d < 8:
        keep = row >= d
        a_sh = jnp.where(keep, pltpu.roll(a, d, axis=1), 1.0)
        u_sh = jnp.where(keep, pltpu.roll(u, d, axis=1), 0.0)
        u = a * u_sh + u
        a = a * a_sh
        d *= 2
    h_prev = carry[...]
    groups = []
    for g in range(t // 8):
        hg = u[g] + a[g] * h_prev
        groups.append(hg)
        h_prev = hg[7:8, :]
    h = jnp.concatenate(groups, axis=0)
    carry[...] = h_prev
    gb = gb_ref[...]
    gelu = 0.5 * gb * (1.0 + jnp.tanh(0.7978845608028654 * (gb + 0.044715 * (gb * gb * gb))))
    o_ref[...] = (h * gelu).astype(o_ref.dtype)


def _rglru(z, conv_w, conv_b, wa, ba, wx, bx, lam, batch, seq, x_col, g_col, t):
    n = z.shape[0]
    ch = lam.shape[-1]
    nt = seq // t
    hpt = t // 8
    nblk, bd = wa.shape[0], wa.shape[1]
    per = LANES // bd
    def blockdiag(w):
        w = w.reshape(nblk // per, per, bd, bd)
        out = jnp.zeros((nblk // per, LANES, LANES), F32)
        for q in range(per):
            out = out.at[:, q * bd:(q + 1) * bd, q * bd:(q + 1) * bd].set(w[:, q])
        return out
    wg = jnp.concatenate([blockdiag(wa), blockdiag(wx)], axis=-1).astype(BF16)
    cw = jnp.zeros((8, ch), F32).at[:LRU_CONV_WIDTH].set(conv_w.reshape(LRU_CONV_WIDTH, ch))
    cur = lambda b, i: b * nt + i
    halo = lambda b, i: jnp.maximum((b * nt + i) * hpt - 1, 0)
    vec = pl.BlockSpec((1, ch), lambda b, i: (0, 0))
    return pl.pallas_call(
        _lru_kernel,
        grid=(batch, nt),
        in_specs=[pl.BlockSpec((t, ch), lambda b, i: (cur(b, i), x_col)),
                  pl.BlockSpec((t, ch), lambda b, i: (cur(b, i), g_col)),
                  pl.BlockSpec((8, ch), lambda b, i: (halo(b, i), x_col)),
                  pl.BlockSpec((8, ch), lambda b, i: (0, 0)),
                  vec,
                  pl.BlockSpec(wg.shape, lambda b, i: (0, 0, 0)),
                  vec, vec, vec],
        out_specs=pl.BlockSpec((t, ch), lambda b, i: (cur(b, i), 0)),
        out_shape=jax.ShapeDtypeStruct((n, ch), BF16),
        scratch_shapes=[pltpu.VMEM((t + 8, ch), F32), pltpu.VMEM((1, ch), F32)],
        compiler_params=_cparams(("parallel", "arbitrary")),
        name="rglru",
    )(z, z, z, cw, conv_b.reshape(1, ch), wg, ba.reshape(1, ch), bx.reshape(1, ch), lam.reshape(1, ch))


def _moe_kernel(be_ref, tok_ref, dst_ref, h_hbm, w1_ref, w3_ref, w2_ref, o_hbm, x0, x1, o0, o1, gsem, ssem):
    i = pl.program_id(0)
    last = pl.num_programs(0) - 1
    xs, outs = (x0, x1), (o0, o1)

    def gather_row(j, r, s):
        return pltpu.make_async_copy(h_hbm.at[pl.ds(tok_ref[j, r], 1)], xs[s].at[pl.ds(r, 1)], gsem.at[s])

    def scatter_row(j, r, s):
        return pltpu.make_async_copy(outs[s].at[pl.ds(r, 1)], o_hbm.at[pl.ds(dst_ref[j, r], 1)], ssem.at[s])

    def gather_block(s):
        return pltpu.make_async_copy(h_hbm.at[pl.ds(0, MOE_BLOCK)], xs[s], gsem.at[s])

    def scatter_block(s):
        return pltpu.make_async_copy(outs[s], o_hbm.at[pl.ds(0, MOE_BLOCK)], ssem.at[s])

    @pl.when(i == 0)
    def _():
        o1[...] = jnp.zeros_like(o1)
        for r in range(MOE_BLOCK):
            gather_row(0, r, 0).start()

    def step(s):
        t = 1 - s
        gather_block(s).wait()

        @pl.when(i >= 1)
        def _():
            scatter_block(s).wait()

        nxt = jnp.minimum(i + 1, last)
        prv = jnp.maximum(i - 1, 0)
        f = w1_ref.shape[-1]
        n_groups = f // MOE_FCHUNK
        per_group = MOE_BLOCK // n_groups
        for c in range(n_groups):
            @pl.when(i >= 0)
            def _():
                for r in range(c * per_group, (c + 1) * per_group):
                    gather_row(nxt, r, t).start()
                    scatter_row(prv, r, t).start()

            fs = slice(c * MOE_FCHUNK, (c + 1) * MOE_FCHUNK)
            x = xs[s][...].astype(BF16)
            h1 = jnp.dot(x, w1_ref[0, :, fs], preferred_element_type=F32)
            h3 = jnp.dot(x, w3_ref[0, :, fs], preferred_element_type=F32)
            hid = (h1 * _sigmoid(h1) * h3).astype(BF16)
            part = jnp.dot(hid, w2_ref[0, fs, :], preferred_element_type=F32)
            if c == 0:
                outs[s][...] = part
            else:
                outs[s][...] += part

        @pl.when(i == last)
        def _():
            for r in range(MOE_BLOCK):
                scatter_row(i, r, s).start()
            scatter_block(t).wait()
            scatter_block(s).wait()
            gather_block(t).wait()

    for s in range(2):
        pl.when(i % 2 == s)(functools.partial(step, s))


def _moe_experts(h, block_e, row_tok, row_dst, w1, w3, w2):
    n, d = h.shape
    f = w1.shape[-1]
    n_blocks = row_dst.shape[0]
    rows = n_blocks * MOE_BLOCK
    grid_spec = pltpu.PrefetchScalarGridSpec(
        num_scalar_prefetch=3,
        grid=(n_blocks,),
        in_specs=[pl.BlockSpec(memory_space=pl.ANY),
                  pl.BlockSpec((1, d, f), lambda i, be, tok, dst: (be[i], 0, 0)),
                  pl.BlockSpec((1, d, f), lambda i, be, tok, dst: (be[i], 0, 0)),
                  pl.BlockSpec((1, f, d), lambda i, be, tok, dst: (be[i], 0, 0))],
        out_specs=pl.BlockSpec(memory_space=pl.ANY),
        scratch_shapes=[pltpu.VMEM((MOE_BLOCK, d), F32)] * 4
                       + [pltpu.SemaphoreType.DMA((2,)), pltpu.SemaphoreType.DMA((2,))],
    )
    return pl.pallas_call(
        _moe_kernel,
        grid_spec=grid_spec,
        out_shape=jax.ShapeDtypeStruct((rows, d), F32),
        compiler_params=_cparams(("arbitrary",)),
        name="moe_experts",
    )(block_e, row_tok, row_dst, h, w1, w3, w2)


def _moe_plan(ridx, counts, n):
    nk = 2 * n
    e = ridx[0:2]
    rank = ridx[2:4]
    padded = (counts + MOE_BLOCK - 1) // MOE_BLOCK * MOE_BLOCK
    pend = jnp.cumsum(padded)
    pstart = pend - padded
    experts = jnp.arange(N_EXPERTS, dtype=jnp.int32)[:, None, None]
    dest = rank + jnp.sum(jnp.where(e[None] == experts, pstart[:, None, None], 0), axis=0)
    n_blocks = nk // MOE_BLOCK + N_EXPERTS
    rows = n_blocks * MOE_BLOCK
    result_row = jnp.arange(nk, dtype=jnp.int32)
    placed = jnp.full((rows,), -1, jnp.int32).at[dest.reshape(-1)].set(result_row)
    is_pad = placed < 0
    pad_rank = jnp.cumsum(is_pad.astype(jnp.int32)) - 1
    row_dst = jnp.where(is_pad, nk + pad_rank, placed)
    row_tok = jnp.where(is_pad, 0, jnp.where(placed >= n, placed - n, placed))
    block_start = jnp.arange(n_blocks, dtype=jnp.int32) * MOE_BLOCK
    block_e = jnp.minimum(jnp.sum(block_start[:, None] >= pend[None, :], axis=1), N_EXPERTS - 1).astype(jnp.int32)
    return block_e, row_tok.reshape(n_blocks, MOE_BLOCK), row_dst.reshape(n_blocks, MOE_BLOCK)


def _residual_kernel(x_ref, o1_ref, o2_ref, rg_ref, gate_ref, g_ref, out_ref, *, final):
    g = rg_ref[...].T
    y = g[:, 0:1] * o1_ref[...] + g[:, 1:2] * o2_ref[...]
    x2 = x_ref[...] + gate_ref[0] * y
    if final:
        ms = jnp.mean(x2 * x2, axis=-1, keepdims=True)
        x2 = x2 * lax.rsqrt(ms + EPS) * g_ref[...]
    out_ref[...] = x2


def _residual(x, o_all, rgate, gate, g, seq, tm, final):
    n, d = x.shape
    per_seq = seq // tm
    nt = n // tm
    return pl.pallas_call(
        functools.partial(_residual_kernel, final=final),
        grid=(nt,),
        in_specs=[pl.BlockSpec((tm, d), lambda i: (i, 0)),
                  pl.BlockSpec((tm, d), lambda i: (i, 0)),
                  pl.BlockSpec((tm, d), lambda i: (i + nt, 0)),
                  pl.BlockSpec((8, tm), lambda i: (0, i)),
                  pl.BlockSpec((1, 1, d), lambda i: (i // per_seq, 0, 0)),
                  pl.BlockSpec((1, d), lambda i: (0, 0))],
        out_specs=pl.BlockSpec((tm, d), lambda i: (i, 0)),
        out_shape=jax.ShapeDtypeStruct((n, d), F32),
        compiler_params=_cparams(("parallel",)),
        name="ffn_residual",
    )(x, o_all, o_all, rgate, gate, g.reshape(1, d))


def _pad_cols(w, width):
    return jnp.pad(w, ((0, 0), (0, width - w.shape[1])))


def _pad_rows(w, height):
    return jnp.pad(w, ((0, height - w.shape[0]), (0, 0)))


def kernel(x, c, positions, router_w, router_bias, ada_w, ada_b, norm_mix, norm_ffn, moe_w1, moe_w3, moe_w2, ab_w_in, ab_sinks, ab_conv_w, ab_conv_b, ab_conv_ln_g, ab_conv_ln_b, ab_w_out, cd_w_in, cd_shift_mu, cd_w0, cd_w2, cd_a0, cd_a2, cd_g2, cd_k_k, cd_k_a, cd_r_k, cd_ln_x_g, cd_ln_x_b, cd_lru_conv_w, cd_lru_conv_b, cd_lru_wa, cd_lru_ba, cd_lru_wx, cd_lru_bx, cd_lru_lambda, cd_w_out, final_norm):
    batch, seq, d = x.shape
    n = batch * seq
    depth = ada_w.shape[0]
    xf = x.reshape(n, d)
    mod = _ada_mod(c, ada_w, ada_b).reshape(depth, batch, 6, 1, d)
    cos_t, sin_t = _rope_tables(positions)
    att_q = ATT_HEADS * HEAD_DIM
    att_kv = ATT_KV_HEADS * HEAD_DIM
    tm = 256

    for layer in range(depth):
        j = layer // 2
        shift_m, scale_m, gate_m, shift_f, scale_f, gate_f = (mod[layer, :, q] for q in range(6))
        if layer % 2 == 0:
            w_in = ab_w_in[j]
            w_perm = jnp.concatenate([w_in[:, :att_q], w_in[:, att_q + 2 * att_kv:],
                                      w_in[:, att_q:att_q + 2 * att_kv]], axis=1).astype(BF16)
            z = _in_proj(xf, norm_mix[layer], shift_m, scale_m, w_perm, seq, 256, 512)
            conv_ch = ab_conv_w.shape[-1]
            kv0 = (att_q + 2 * conv_ch) // att_kv
            a1 = _attention(z, cos_t, sin_t, ab_sinks[j], batch, seq, kv0, kv0 + 1)
            a2 = _conformer_conv(z, ab_conv_w[j], ab_conv_b[j], ab_conv_ln_g[j], ab_conv_ln_b[j],
                                 batch, seq, att_q // conv_ch, att_q // conv_ch + 1, 128)
            w_out = ab_w_out[j]
        else:
            w_in = cd_w_in[j]
            dim = cd_w0.shape[-1]
            dl = cd_w2.shape[1]
            al = cd_a2.shape[1]
            gl = cd_g2.shape[1]
            s0 = 3 * dim
            seg_zw = w_in[:, s0:s0 + dl]
            seg_za = w_in[:, s0 + dl:s0 + dl + al]
            seg_zg = w_in[:, s0 + dl + al:s0 + dl + al + gl]
            s1 = s0 + dl + al + gl
            w_perm = jnp.concatenate([w_in[:, :s0], w_in[:, s1:], _pad_cols(seg_zw, LANES),
                                      _pad_cols(seg_za, LANES), seg_zg], axis=1).astype(BF16)
            mu = cd_shift_mu[j]
            mu_lo = jnp.concatenate([jnp.pad(mu[s0:s0 + dl], (0, LANES - dl)),
                                     jnp.pad(mu[s0 + dl:s0 + dl + al], (0, LANES - al)),
                                     mu[s0 + dl + al:s1]]).reshape(1, -1)
            row = lambda a: a.reshape(1, -1)
            p = dict(mu_r=row(mu[:dim]), mu_k=row(mu[dim:2 * dim]), mu_v=row(mu[2 * dim:s0]), mu_lo=mu_lo,
                     w0=row(cd_w0[j]), w2=_pad_rows(cd_w2[j], LANES).astype(BF16),
                     a0=row(cd_a0[j]), a2=_pad_rows(cd_a2[j], LANES).astype(BF16), g2=cd_g2[j].astype(BF16),
                     k_k=row(cd_k_k[j]), k_a=row(cd_k_a[j]), r_k=row(cd_r_k[j]),
                     ln_g=row(cd_ln_x_g[j]), ln_b=row(cd_ln_x_b[j]))
            z = _in_proj(xf, norm_mix[layer], shift_m, scale_m, w_perm, seq, 256, 512)
            lo_col = (5 * dim) // (4 * LANES)
            a1 = _rwkv(z, p, batch, seq, lo_col)
            a2 = _rglru(z, cd_lru_conv_w[j], cd_lru_conv_b[j], cd_lru_wa[j], cd_lru_ba[j],
                        cd_lru_wx[j], cd_lru_bx[j], cd_lru_lambda[j], batch, seq, 3, 4, 256)
            w_out = cd_w_out[j]
        x1, h, ridx, rgate, counts = _out_proj(a1, a2, w_out, xf, gate_m, norm_ffn[layer], shift_f, scale_f,
                                               router_w, router_bias, seq, tm)
        block_e, row_tok, row_dst = _moe_plan(ridx, counts[:, 0], n)
        o_all = _moe_experts(h, block_e, row_tok, row_dst, moe_w1[layer].astype(BF16),
                             moe_w3[layer].astype(BF16), moe_w2[layer].astype(BF16))
        xf = _residual(x1, o_all, rgate, gate_f, final_norm, seq, 512, layer == depth - 1)
    return xf.reshape(batch, seq, d)
```

```python
import functools

import jax
import jax.numpy as jnp
from jax import lax
from jax.experimental import pallas as pl
from jax.experimental.pallas import tpu as pltpu

F32 = jnp.float32
BF16 = jnp.bfloat16

EPS = 1e-6
LN_EPS = 1e-5
RWKV_LN_EPS = 64e-5
HEAD_DIM = 64
LANES = 128
MXU_TILE = 256
ATT_BLOCK = 128
WINDOW = 128
ATT_HEADS = 16
ATT_KV_HEADS = 4
CONV_WIDTH = 31
CONV_HALO = 32
LRU_CONV_WIDTH = 4
LRU_C = 8.0
N_EXPERTS = 16
N_GROUPS = 4
EXPERTS_PER_GROUP = 4
MOE_BLOCK = 256
MOE_FCHUNK = 256
RWKV_CHUNK = 64
RWKV_ROWS = 256
ROPE_THETA = 10000.0
VMEM_LIMIT = 56 * 1024 * 1024


def _cparams(sem):
    return pltpu.CompilerParams(dimension_semantics=sem, vmem_limit_bytes=VMEM_LIMIT)


def _bdot(a, b):
    return jnp.dot(a.astype(BF16), b.astype(BF16), preferred_element_type=F32)


def _split2(x):
    hi = x.astype(BF16)
    lo = (x - hi.astype(F32)).astype(BF16)
    return hi, lo


def _dot_left_exact(m_bf16, x):
    hi, lo = _split2(x)
    d = functools.partial(jnp.dot, preferred_element_type=F32)
    return d(m_bf16, hi) + d(m_bf16, lo)


def _dot_right_exact(x, m_bf16):
    hi, lo = _split2(x)
    d = functools.partial(jnp.dot, preferred_element_type=F32)
    return d(hi, m_bf16) + d(lo, m_bf16)


def _dot_3pass(a, b):
    a_hi = a.astype(BF16)
    a_lo = (a - a_hi.astype(F32)).astype(BF16)
    b_hi = b.astype(BF16)
    b_lo = (b - b_hi.astype(F32)).astype(BF16)
    d = functools.partial(jnp.dot, preferred_element_type=F32)
    return d(a_hi, b_hi) + d(a_hi, b_lo) + d(a_lo, b_hi)


def _sigmoid(x):
    return 1.0 / (1.0 + jnp.exp(-x))


def _softplus(x):
    return jnp.maximum(x, 0.0) + jnp.log(1.0 + jnp.exp(-jnp.abs(x)))


def _rms_mod(x, g, shift, scale):
    ms = jnp.mean(x * x, axis=-1, keepdims=True)
    return (x * lax.rsqrt(ms + EPS) * g) * (1.0 + scale) + shift


def _ada_kernel(c_ref, w_ref, b_ref, o_ref):
    c = c_ref[...]
    o_ref[0] = _dot_3pass(c * _sigmoid(c), w_ref[0]) + b_ref[0]


def _ada_mod(c, ada_w, ada_b):
    depth, d, n6 = ada_w.shape
    b = c.shape[0]
    bp = 8
    tn = 768
    c_pad = jnp.zeros((bp, d), F32).at[:b].set(c)
    out = pl.pallas_call(
        _ada_kernel,
        grid=(depth, n6 // tn),
        in_specs=[pl.BlockSpec((bp, d), lambda l, j: (0, 0)),
                  pl.BlockSpec((1, d, tn), lambda l, j: (l, 0, j)),
                  pl.BlockSpec((1, 1, tn), lambda l, j: (l, 0, j))],
        out_specs=pl.BlockSpec((1, bp, tn), lambda l, j: (l, 0, j)),
        out_shape=jax.ShapeDtypeStruct((depth, bp, n6), F32),
        compiler_params=_cparams(("parallel", "parallel")),
        name="ada_mod",
    )(c_pad, ada_w, ada_b.reshape(depth, 1, n6))
    return out[:, :b]


def _rope_kernel(pos_ref, f_ref, cos_ref, sin_ref):
    ang = pos_ref[...].astype(F32) * f_ref[...]
    lane = lax.broadcasted_iota(jnp.int32, ang.shape, 1)
    s = jnp.sin(ang)
    cos_ref[...] = jnp.cos(ang)
    sin_ref[...] = jnp.where((lane & 32) == 0, -s, s)


def _rope_tables(positions):
    n = positions.size
    half = HEAD_DIM // 2
    inv_freq = ROPE_THETA ** (-jnp.arange(half, dtype=F32) / half)
    f_row = jnp.tile(inv_freq, LANES // half).reshape(1, LANES)
    tm = 512
    return pl.pallas_call(
        _rope_kernel,
        grid=(n // tm,),
        in_specs=[pl.BlockSpec((tm, 1), lambda i: (i, 0)),
                  pl.BlockSpec((1, LANES), lambda i: (0, 0))],
        out_specs=[pl.BlockSpec((tm, LANES), lambda i: (i, 0))] * 2,
        out_shape=[jax.ShapeDtypeStruct((n, LANES), F32)] * 2,
        compiler_params=_cparams(("parallel",)),
        name="rope_tables",
    )(positions.reshape(n, 1), f_row)


def _in_proj_kernel(x_ref, g_ref, sh_ref, sc_ref, w_ref, o_ref, *, tn):
    h = _rms_mod(x_ref[...], g_ref[...], sh_ref[0], sc_ref[0]).astype(BF16)
    for c in range(o_ref.shape[1] // tn):
        o_ref[:, c * tn:(c + 1) * tn] = jnp.dot(h, w_ref[:, c * tn:(c + 1) * tn], preferred_element_type=F32)


def _in_proj(x, g, shift, scale, w_bf16, seq, tm, tn):
    n, d = x.shape
    nout = w_bf16.shape[1]
    per_seq = seq // tm
    return pl.pallas_call(
        functools.partial(_in_proj_kernel, tn=tn),
        grid=(n // tm,),
        in_specs=[pl.BlockSpec((tm, d), lambda i: (i, 0)),
                  pl.BlockSpec((1, d), lambda i: (0, 0)),
                  pl.BlockSpec((1, 1, d), lambda i: (i // per_seq, 0, 0)),
                  pl.BlockSpec((1, 1, d), lambda i: (i // per_seq, 0, 0)),
                  pl.BlockSpec((d, nout), lambda i: (0, 0), pipeline_mode=pl.Buffered(1))],
        out_specs=pl.BlockSpec((tm, nout), lambda i: (i, 0)),
        out_shape=jax.ShapeDtypeStruct((n, nout), F32),
        compiler_params=_cparams(("parallel",)),
        name="in_proj",
    )(x, g.reshape(1, d), shift, scale, w_bf16)


def _route(logits_t, bias_col):
    m = jnp.max(logits_t, axis=0, keepdims=True)
    e = jnp.exp(logits_t - m)
    probs = e / jnp.sum(e, axis=0, keepdims=True)
    sel = probs + bias_col
    s = [sel[i:i + 1, :] for i in range(N_EXPERTS)]
    pr = [probs[i:i + 1, :] for i in range(N_EXPERTS)]

    def top2_sum(v):
        best = v[0] + v[1]
        for i in range(len(v)):
            for j in range(i + 1, len(v)):
                if (i, j) != (0, 1):
                    best = jnp.maximum(best, v[i] + v[j])
        return best

    gs = [top2_sum(s[4 * g:4 * g + 4]) for g in range(N_GROUPS)]
    best = gs[0]
    gi = jnp.zeros_like(best, dtype=jnp.int32)
    for g in range(1, N_GROUPS):
        upd = gs[g] > best
        gi = jnp.where(upd, g, gi)
        best = jnp.where(upd, gs[g], best)

    def pick_group(rows):
        out = []
        for l in range(EXPERTS_PER_GROUP):
            v = rows[l]
            for g in range(1, N_GROUPS):
                v = jnp.where(gi == g, rows[4 * g + l], v)
            out.append(v)
        return out

    ig = pick_group(s)
    pg = pick_group(pr)

    def argmax4(v):
        bv = v[0]
        bi = jnp.zeros_like(gi)
        for l in range(1, EXPERTS_PER_GROUP):
            upd = v[l] > bv
            bi = jnp.where(upd, l, bi)
            bv = jnp.where(upd, v[l], bv)
        return bi

    i1 = argmax4(ig)
    i2 = argmax4([jnp.where(i1 == l, -jnp.inf, ig[l]) for l in range(EXPERTS_PER_GROUP)])

    def pick(v, idx):
        o = v[0]
        for l in range(1, EXPERTS_PER_GROUP):
            o = jnp.where(idx == l, v[l], o)
        return o

    p1 = pick(pg, i1)
    p2 = pick(pg, i2)
    tot = p1 + p2
    return gi * EXPERTS_PER_GROUP + i1, gi * EXPERTS_PER_GROUP + i2, p1 / tot, p2 / tot


def _out_proj_kernel(a1_ref, a2_ref, w1_ref, w2_ref, x_ref, gate_ref, g_ref, sh_ref, sc_ref,
                     rw_ref, rb_ref, upper_ref, xo_ref, h_ref, ridx_ref, rgate_ref, cnt_ref, carry):
    @pl.when(pl.program_id(0) == 0)
    def _():
        carry[...] = jnp.zeros_like(carry)

    y = (jnp.dot(a1_ref[...], w1_ref[...], preferred_element_type=F32)
         + jnp.dot(a2_ref[...], w2_ref[...], preferred_element_type=F32))
    x1 = x_ref[...] + gate_ref[0] * y
    xo_ref[...] = x1
    h = _rms_mod(x1, g_ref[...], sh_ref[0], sc_ref[0])
    h_ref[...] = h
    logits = _dot_3pass(h, rw_ref[...])
    logits_t = logits.T[:N_EXPERTS, :]
    e1, e2, g1, g2 = _route(logits_t, rb_ref[...])
    erow = lax.broadcasted_iota(jnp.int32, logits_t.shape, 0)
    hit1 = erow == e1
    hit2 = erow == e2
    onehot = jnp.where(hit1 | hit2, 1.0, 0.0)
    before = jnp.dot(onehot.astype(BF16), upper_ref[...], preferred_element_type=F32) + carry[:, 0:1]
    rank1 = jnp.sum(jnp.where(hit1, before, 0.0), axis=0, keepdims=True).astype(jnp.int32)
    rank2 = jnp.sum(jnp.where(hit2, before, 0.0), axis=0, keepdims=True).astype(jnp.int32)
    total = carry[...] + jnp.sum(onehot, axis=1, keepdims=True)
    carry[...] = total
    cnt_ref[...] = total.astype(jnp.int32)
    row = lax.broadcasted_iota(jnp.int32, ridx_ref.shape, 0)
    ridx_ref[...] = jnp.where(row == 0, e1, jnp.where(row == 1, e2,
                              jnp.where(row == 2, rank1, jnp.where(row == 3, rank2, 0))))
    rgate_ref[...] = jnp.where(row == 0, g1, jnp.where(row == 1, g2, 0.0))


def _out_proj(a1, a2, w_out, x, gate, g, shift, scale, router_w, router_bias, seq, tm):
    n, d = x.shape
    ka = a1.shape[1]
    per_seq = seq // tm
    w1 = w_out[:ka].astype(BF16)
    w2 = w_out[ka:].astype(BF16)
    rw = jnp.zeros((d, LANES), F32).at[:, :N_EXPERTS].set(router_w)
    rb = router_bias.astype(F32).reshape(N_EXPERTS, 1)
    upper = (jnp.arange(tm)[:, None] < jnp.arange(tm)[None, :]).astype(BF16)
    mod_spec = pl.BlockSpec((1, 1, d), lambda i: (i // per_seq, 0, 0))
    return pl.pallas_call(
        _out_proj_kernel,
        grid=(n // tm,),
        in_specs=[pl.BlockSpec((tm, ka), lambda i: (i, 0)),
                  pl.BlockSpec((tm, ka), lambda i: (i, 0)),
                  pl.BlockSpec((ka, d), lambda i: (0, 0), pipeline_mode=pl.Buffered(1)),
                  pl.BlockSpec((ka, d), lambda i: (0, 0), pipeline_mode=pl.Buffered(1)),
                  pl.BlockSpec((tm, d), lambda i: (i, 0)),
                  mod_spec,
                  pl.BlockSpec((1, d), lambda i: (0, 0)),
                  mod_spec, mod_spec,
                  pl.BlockSpec((d, LANES), lambda i: (0, 0)),
                  pl.BlockSpec((N_EXPERTS, 1), lambda i: (0, 0)),
                  pl.BlockSpec((tm, tm), lambda i: (0, 0))],
        out_specs=[pl.BlockSpec((tm, d), lambda i: (i, 0)),
                   pl.BlockSpec((tm, d), lambda i: (i, 0)),
                   pl.BlockSpec((8, tm), lambda i: (0, i)),
                   pl.BlockSpec((8, tm), lambda i: (0, i)),
                   pl.BlockSpec((N_EXPERTS, LANES), lambda i: (0, 0))],
        out_shape=[jax.ShapeDtypeStruct((n, d), F32),
                   jax.ShapeDtypeStruct((n, d), F32),
                   jax.ShapeDtypeStruct((8, n), jnp.int32),
                   jax.ShapeDtypeStruct((8, n), F32),
                   jax.ShapeDtypeStruct((N_EXPERTS, LANES), jnp.int32)],
        scratch_shapes=[pltpu.VMEM((N_EXPERTS, LANES), F32)],
        compiler_params=_cparams(("arbitrary",)),
        name="out_proj_route",
    )(a1, a2, w1, w2, x, gate, g.reshape(1, d), shift, scale, rw, rb, upper)


def _rope_apply(x, cos, sin_signed):
    outs = []
    lane = lax.broadcasted_iota(jnp.int32, cos.shape, 1)
    first_half = (lane & 32) == 0
    for c in range(x.shape[1] // LANES):
        xc = x[:, c * LANES:(c + 1) * LANES]
        partner = jnp.where(first_half, pltpu.roll(xc, LANES - 32, axis=1), pltpu.roll(xc, 32, axis=1))
        outs.append(xc * cos + partner * sin_signed)
    return outs


def _attn_kernel(sink_ref, q_ref, kp_ref, kc_ref, vp_ref, vc_ref, cq_ref, sq_ref, cp_ref, sp_ref, o_ref):
    has_prev = pl.program_id(1) > 0
    cq, sq = cq_ref[...], sq_ref[...]
    q_cols = _rope_apply(q_ref[...] * (HEAD_DIM ** -0.5), cq, sq)
    k_cols = [jnp.concatenate([kp, kc], axis=0) for kp, kc in
              zip(_rope_apply(kp_ref[...], cp_ref[...], sp_ref[...]), _rope_apply(kc_ref[...], cq, sq))]
    v_all = jnp.concatenate([vp_ref[...], vc_ref[...]], axis=0).astype(BF16)
    qi = lax.broadcasted_iota(jnp.int32, (ATT_BLOCK, 2 * ATT_BLOCK), 0)
    kj = lax.broadcasted_iota(jnp.int32, (ATT_BLOCK, 2 * ATT_BLOCK), 1)
    valid = ((kj > qi) & (kj < ATT_BLOCK) & has_prev) | ((kj >= ATT_BLOCK) & (kj - ATT_BLOCK <= qi))
    group = ATT_HEADS // ATT_KV_HEADS
    heads = range(ATT_HEADS)
    scores = []
    for hq in heads:
        hk = hq // group
        qh = q_cols[hq // 2][:, (hq % 2) * HEAD_DIM:(hq % 2 + 1) * HEAD_DIM].astype(BF16)
        kh = k_cols[hk // 2][:, (hk % 2) * HEAD_DIM:(hk % 2 + 1) * HEAD_DIM].astype(BF16)
        s = lax.dot_general(qh, kh, (((1,), (1,)), ((), ())), preferred_element_type=F32)
        scores.append(jnp.where(valid, s, -1e30))
    tops = [jnp.maximum(jnp.max(scores[hq], axis=-1, keepdims=True), sink_ref[hq]) for hq in heads]
    probs = [jnp.exp(scores[hq] - tops[hq]) for hq in heads]
    dens = [jnp.sum(probs[hq], axis=-1, keepdims=True) + jnp.exp(sink_ref[hq] - tops[hq]) for hq in heads]
    outs = [jnp.dot(probs[hq].astype(BF16), v_all[:, (hq // group) * HEAD_DIM:(hq // group + 1) * HEAD_DIM],
                    preferred_element_type=F32) / dens[hq] for hq in heads]
    out_cols = [jnp.concatenate([outs[2 * pair], outs[2 * pair + 1]], axis=1) for pair in range(ATT_HEADS // 2)]
    o_ref[...] = jnp.concatenate(out_cols, axis=1).astype(o_ref.dtype)


def _attention(z, cos_t, sin_t, sinks, batch, seq, k_col, v_col):
    n = z.shape[0]
    nb = seq // ATT_BLOCK
    att_q = ATT_HEADS * HEAD_DIM
    att_kv = ATT_KV_HEADS * HEAD_DIM
    cur = lambda b, i: b * nb + i
    prev = lambda b, i: b * nb + jnp.maximum(i - 1, 0)
    return pl.pallas_call(
        _attn_kernel,
        grid=(batch, nb),
        in_specs=[pl.BlockSpec(memory_space=pltpu.SMEM),
                  pl.BlockSpec((ATT_BLOCK, att_q), lambda b, i: (cur(b, i), 0)),
                  pl.BlockSpec((ATT_BLOCK, att_kv), lambda b, i: (prev(b, i), k_col)),
                  pl.BlockSpec((ATT_BLOCK, att_kv), lambda b, i: (cur(b, i), k_col)),
                  pl.BlockSpec((ATT_BLOCK, att_kv), lambda b, i: (prev(b, i), v_col)),
                  pl.BlockSpec((ATT_BLOCK, att_kv), lambda b, i: (cur(b, i), v_col)),
                  pl.BlockSpec((ATT_BLOCK, LANES), lambda b, i: (cur(b, i), 0)),
                  pl.BlockSpec((ATT_BLOCK, LANES), lambda b, i: (cur(b, i), 0)),
                  pl.BlockSpec((ATT_BLOCK, LANES), lambda b, i: (prev(b, i), 0)),
                  pl.BlockSpec((ATT_BLOCK, LANES), lambda b, i: (prev(b, i), 0))],
        out_specs=pl.BlockSpec((ATT_BLOCK, att_q), lambda b, i: (cur(b, i), 0)),
        out_shape=jax.ShapeDtypeStruct((n, att_q), BF16),
        compiler_params=_cparams(("parallel", "parallel")),
        name="swa_attention",
    )(sinks.astype(F32), z, z, z, z, z, cos_t, sin_t, cos_t, sin_t)


def _conv_kernel(val_ref, gate_ref, hval_ref, hgate_ref, w_ref, b_ref, g_ref, beta_ref, o_ref, ybuf, acc, shifted):
    t = val_ref.shape[0]
    ch = val_ref.shape[1]
    first = pl.program_id(1) == 0
    yh = hval_ref[...] * _sigmoid(hgate_ref[...])
    ybuf[0:CONV_HALO, :] = jnp.where(first, 0.0, yh)
    ybuf[CONV_HALO:, :] = val_ref[...] * _sigmoid(gate_ref[...])
    off = CONV_HALO - (CONV_WIDTH - 1)
    span = t + CONV_HALO - 8
    for ph in range(1, 8):
        shifted[ph - 1] = ybuf[ph:ph + span, :]
    for cb in range(ch // LANES):
        cs = slice(cb * LANES, (cb + 1) * LANES)
        a = jnp.broadcast_to(b_ref[:, cs], (t, LANES))
        for j in range(CONV_WIDTH):
            ph, base = (off + j) % 8, (off + j) // 8 * 8
            src = ybuf if ph == 0 else shifted.at[ph - 1]
            wj = jnp.broadcast_to(w_ref[j:j + 1, cs], (t, LANES))
            a = a + wj * src[base:base + t, cs]
        acc[:, cs] = a
    y = acc[...]
    mu = jnp.mean(y, axis=-1, keepdims=True)
    yc = y - mu
    var = jnp.mean(yc * yc, axis=-1, keepdims=True)
    ln = yc * lax.rsqrt(var + LN_EPS) * g_ref[...] + beta_ref[...]
    o_ref[...] = (ln * _sigmoid(ln)).astype(o_ref.dtype)


def _conformer_conv(z, conv_w, conv_b, ln_g, ln_b, batch, seq, val_col, gate_col, t):
    n = z.shape[0]
    ch = conv_w.shape[-1]
    nt = seq // t
    hpt = t // CONV_HALO
    w = jnp.zeros((CONV_HALO, ch), F32).at[:CONV_WIDTH].set(conv_w.reshape(CONV_WIDTH, ch))
    cur = lambda b, i: b * nt + i
    halo = lambda b, i: jnp.maximum((b * nt + i) * hpt - 1, 0)
    vec = pl.BlockSpec((1, ch), lambda b, i: (0, 0))
    return pl.pallas_call(
        _conv_kernel,
        grid=(batch, nt),
        in_specs=[pl.BlockSpec((t, ch), lambda b, i: (cur(b, i), val_col)),
                  pl.BlockSpec((t, ch), lambda b, i: (cur(b, i), gate_col)),
                  pl.BlockSpec((CONV_HALO, ch), lambda b, i: (halo(b, i), val_col)),
                  pl.BlockSpec((CONV_HALO, ch), lambda b, i: (halo(b, i), gate_col)),
                  pl.BlockSpec((CONV_HALO, ch), lambda b, i: (0, 0)),
                  vec, vec, vec],
        out_specs=pl.BlockSpec((t, ch), lambda b, i: (cur(b, i), 0)),
        out_shape=jax.ShapeDtypeStruct((n, ch), BF16),
        scratch_shapes=[pltpu.VMEM((t + CONV_HALO, ch), F32), pltpu.VMEM((t, ch), F32),
                        pltpu.VMEM((7, t + CONV_HALO - 8, ch), F32)],
        compiler_params=_cparams(("parallel", "parallel")),
        name="conformer_conv",
    )(z, z, z, z, w, conv_b.reshape(1, ch), ln_g.reshape(1, ch), ln_b.reshape(1, ch))


def _shift_lerp(z, halo, mu, first):
    prev = pltpu.roll(z, 1, axis=0)
    row = lax.broadcasted_iota(jnp.int32, z.shape, 0)
    hrow = jnp.where(first, 0.0, halo[7:8, :])
    prev = jnp.where(row == 0, hrow, prev)
    return z + mu * (prev - z)


def _rwkv_kernel(r_ref, k_ref, v_ref, lo_ref, rh_ref, kh_ref, vh_ref, loh_ref,
                 mur_ref, muk_ref, muv_ref, mulo_ref, w0_ref, w2_ref, a0_ref, a2_ref, g2_ref,
                 kk_ref, ka_ref, rk_ref, lng_ref, lnb_ref, seg_ref, tril_ref,
                 o_ref, hs_ref):
    first = pl.program_id(1) == 0

    @pl.when(first)
    def _():
        hs_ref[...] = jnp.zeros_like(hs_ref)

    r = _shift_lerp(r_ref[...], rh_ref[...], mur_ref[...], first)
    k = _shift_lerp(k_ref[...], kh_ref[...], muk_ref[...], first)
    v = _shift_lerp(v_ref[...], vh_ref[...], muv_ref[...], first)
    lo = _shift_lerp(lo_ref[...], loh_ref[...], mulo_ref[...], first)
    zw, za, zg = lo[:, 0:LANES], lo[:, LANES:2 * LANES], lo[:, 2 * LANES:]

    seg = seg_ref[...]
    seg_w = seg.shape[0]

    def head_sum(x):
        return jnp.concatenate([_dot_right_exact(x[:, c:c + seg_w], seg) for c in range(0, x.shape[1], seg_w)],
                               axis=1)

    w_raw = w0_ref[...] + _bdot(jnp.tanh(zw), w2_ref[...])
    log_decay = -jnp.exp(-_softplus(-w_raw) - 0.5)
    a_gate = _sigmoid(a0_ref[...] + _bdot(za, a2_ref[...]))
    g = _bdot(_sigmoid(zg), g2_ref[...])
    kk = k * kk_ref[...]
    kk = kk * lax.rsqrt(jnp.maximum(head_sum(kk * kk), 1e-24))
    k_mod = k * (1.0 + (a_gate - 1.0) * ka_ref[...])
    b_vec = kk * a_gate

    n_chunks = r.shape[0] // RWKV_CHUNK
    rows = [slice(c * RWKV_CHUNK, (c + 1) * RWKV_CHUNK) for c in range(n_chunks)]
    cum = _dot_left_exact(tril_ref[...], log_decay)
    cum_last = [cum[rs.stop - 1:rs.stop, :] for rs in rows]
    w_last = [jnp.exp(cl) for cl in cum_last]
    inv_w = jnp.exp(-cum)
    to_end = jnp.concatenate([jnp.exp(cum_last[c] - cum[rows[c], :]) for c in range(n_chunks)], axis=0)
    a_til = -kk * jnp.exp(cum - log_decay)
    r_til = r * jnp.exp(cum)
    b_til = b_vec * inv_w
    k_til = k_mod * inv_w
    b_hat = b_vec * to_end
    k_hat = k_mod * to_end

    lane = lax.broadcasted_iota(jnp.int32, (RWKV_CHUNK, LANES), 1)
    m0 = lane < HEAD_DIM
    ri = lax.broadcasted_iota(jnp.int32, (LANES, LANES), 0)
    ci = lax.broadcasted_iota(jnp.int32, (LANES, LANES), 1)
    same = (ri < HEAD_DIM) == (ci < HEAD_DIM)
    mask_sl = same & (ci < ri)
    mask_li = same & (ci <= ri)
    eye = ri == ci

    def stack(x):
        return jnp.concatenate([jnp.where(m0, x, 0.0), jnp.where(m0, 0.0, x)], axis=0)

    nt_dims = (((1,), (1,)), ((), ()))
    dot = functools.partial(jnp.dot, preferred_element_type=F32)
    n_pairs = r.shape[1] // LANES
    cols = [slice(p * LANES, (p + 1) * LANES) for p in range(n_pairs)]
    units = [(rows[c], cols[p]) for c in range(n_chunks) for p in range(n_pairs)]
    ids = range(len(units))
    xa_f = [stack(a_til[u]) for u in units]
    xr_f = [stack(r_til[u]) for u in units]
    vs = [stack(v[u]).astype(BF16) for u in units]
    m = [lax.dot_general(jnp.concatenate([xa_f[i], xr_f[i]], axis=0).astype(BF16),
                         jnp.concatenate([b_til[units[i]]] * 2 + [k_til[units[i]]] * 2, axis=0).astype(BF16),
                         nt_dims, preferred_element_type=F32) for i in ids]
    power = [jnp.where(mask_sl, m[i][:LANES, :LANES], 0.0).astype(BF16) for i in ids]
    a_ak = [jnp.where(mask_sl, m[i][:LANES, LANES:], 0.0).astype(BF16) for i in ids]
    z = [jnp.concatenate([xa_f[i], dot(a_ak[i], vs[i])], axis=1) for i in ids]
    for level in range(6):
        z = [z[i] + dot(power[i], z[i].astype(BF16)) for i in ids]
        if level < 5:
            power = [dot(power[i], power[i]).astype(BF16) for i in ids]
    zb = [z[i].astype(BF16) for i in ids]
    m_rb = [jnp.where(mask_li, m[i][LANES:, :LANES], 0.0).astype(BF16) for i in ids]
    m_rk = [jnp.where(mask_li, m[i][LANES:, LANES:], 0.0).astype(BF16) for i in ids]
    q = [jnp.concatenate([xr_f[i], dot(m_rk[i], vs[i])], axis=1) + dot(m_rb[i], zb[i]) for i in ids]
    ge = [dot(stack(b_hat[units[i]]).T.astype(BF16), zb[i]) for i in ids]
    e_mat = [ge[i][:, LANES:] + dot(stack(k_hat[units[i]]).T.astype(BF16), vs[i]) for i in ids]
    qg = [jnp.concatenate([q[i][:, :LANES],
                           ge[i][:, :LANES] + jnp.where(eye, w_last[i // n_pairs][:, cols[i % n_pairs]], 0.0)],
                          axis=0).astype(BF16) for i in ids]
    state = [hs_ref[p] for p in range(n_pairs)]
    y_rows = []
    for c in range(n_chunks):
        y_cols = []
        for p in range(n_pairs):
            i = c * n_pairs + p
            yh = dot(qg[i], state[p].astype(BF16))
            state[p] = yh[LANES:] + e_mat[i]
            y2 = yh[:LANES] + q[i][:, LANES:]
            y_cols.append(y2[:RWKV_CHUNK] + y2[RWKV_CHUNK:])
        y_rows.append(jnp.concatenate(y_cols, axis=1))
    for p in range(n_pairs):
        hs_ref[p] = state[p]
    y = jnp.concatenate(y_rows, axis=0)

    inv_hd = 1.0 / HEAD_DIM
    mu = head_sum(y) * inv_hd
    yc = y - mu
    var = head_sum(yc * yc) * inv_hd
    yn = yc * lax.rsqrt(var + RWKV_LN_EPS) * lng_ref[...] + lnb_ref[...]
    bonus = head_sum(r * k_mod * rk_ref[...]) * v
    o_ref[...] = ((yn + bonus) * g).astype(o_ref.dtype)


def _rwkv(z, p, batch, seq, lo_col):
    n = z.shape[0]
    dim = p["w0"].shape[-1]
    t = RWKV_ROWS
    nt = seq // t
    hpt = t // 8
    lo_w = 4 * LANES
    cur = lambda b, i: b * nt + i
    halo = lambda b, i: jnp.maximum((b * nt + i) * hpt - 1, 0)
    si = jnp.arange(MXU_TILE)
    seg = (si[:, None] // HEAD_DIM == si[None, :] // HEAD_DIM).astype(BF16)
    ti = jnp.arange(t)
    tril = ((ti[:, None] >= ti[None, :]) & (ti[:, None] // RWKV_CHUNK == ti[None, :] // RWKV_CHUNK)).astype(BF16)

    def full(a):
        return pl.BlockSpec(a.shape, lambda b, i: (0,) * a.ndim)

    consts = [p["mu_r"], p["mu_k"], p["mu_v"], p["mu_lo"], p["w0"], p["w2"], p["a0"], p["a2"], p["g2"],
              p["k_k"], p["k_a"], p["r_k"], p["ln_g"], p["ln_b"], seg, tril]
    col = lambda c: pl.BlockSpec((t, dim), lambda b, i: (cur(b, i), c))
    hcol = lambda c: pl.BlockSpec((8, dim), lambda b, i: (halo(b, i), c))
    return pl.pallas_call(
        _rwkv_kernel,
        grid=(batch, nt),
        in_specs=[col(0), col(1), col(2), pl.BlockSpec((t, lo_w), lambda b, i: (cur(b, i), lo_col)),
                  hcol(0), hcol(1), hcol(2), pl.BlockSpec((8, lo_w), lambda b, i: (halo(b, i), lo_col))]
                 + [full(a) for a in consts],
        out_specs=pl.BlockSpec((t, dim), lambda b, i: (cur(b, i), 0)),
        out_shape=jax.ShapeDtypeStruct((n, dim), BF16),
        scratch_shapes=[pltpu.VMEM((dim // LANES, LANES, LANES), F32)],
        compiler_params=_cparams(("parallel", "arbitrary")),
        name="rwkv7_chunked",
    )(z, z, z, z, z, z, z, z, *consts)


def _lru_kernel(x_ref, gb_ref, xh_ref, cw_ref, cb_ref, wg_ref, ba_ref, bx_ref, lam_ref, o_ref,
                xbuf, carry):
    t, ch = x_ref.shape
    first = pl.program_id(1) == 0

    @pl.when(first)
    def _():
        carry[...] = jnp.zeros_like(carry)

    xbuf[0:8, :] = jnp.where(first, 0.0, xh_ref[...])
    xbuf[8:, :] = x_ref[...]
    off = 8 - (LRU_CONV_WIDTH - 1)
    xc = jnp.broadcast_to(cb_ref[...], (t, ch))
    for j in range(LRU_CONV_WIDTH):
        xc = xc + cw_ref[j:j + 1, :] * xbuf[off + j:off + j + t, :]
    gr, gi = [], []
    for p in range(ch // LANES):
        gp = _bdot(xc[:, p * LANES:(p + 1) * LANES], wg_ref[p])
        gr.append(gp[:, :LANES])
        gi.append(gp[:, LANES:])
    r = _sigmoid(jnp.concatenate(gr, axis=1) + ba_ref[...])
    i = _sigmoid(jnp.concatenate(gi, axis=1) + bx_ref[...])
    log_a = -LRU_C * r * _softplus(-lam_ref[...])
    a = jnp.exp(log_a)
    u = jnp.sqrt(1.0 - jnp.exp(2.0 * log_a)) * (i * xc)
    a = a.reshape(t // 8, 8, ch)
    u = u.reshape(t // 8, 8, ch)
    row = lax.broadcasted_iota(jnp.int32, a.shape, 1)
    d = 1
    while d < 8:
        keep = row >= d
        a_sh = jnp.where(keep, pltpu.roll(a, d, axis=1), 1.0)
        u_sh = jnp.where(keep, pltpu.roll(u, d, axis=1), 0.0)
        u = a * u_sh + u
        a = a * a_sh
        d *= 2
    h_prev = carry[...]
    groups = []
    for g in range(t // 8):
        hg = u[g] + a[g] * h_prev
        groups.append(hg)
        h_prev = hg[7:8, :]
    h = jnp.concatenate(groups, axis=0)
    carry[...] = h_prev
    gb = gb_ref[...]
    gelu = 0.5 * gb * (1.0 + jnp.tanh(0.7978845608028654 * (gb + 0.044715 * (gb * gb * gb))))
    o_ref[...] = (h * gelu).astype(o_ref.dtype)


def _rglru(z, conv_w, conv_b, wa, ba, wx, bx, lam, batch, seq, x_col, g_col, t):
    n = z.shape[0]
    ch = lam.shape[-1]
    nt = seq // t
    hpt = t // 8
    nblk, bd = wa.shape[0], wa.shape[1]
    per = LANES // bd
    def blockdiag(w):
        w = w.reshape(nblk // per, per, bd, bd)
        out = jnp.zeros((nblk // per, LANES, LANES), F32)
        for q in range(per):
            out = out.at[:, q * bd:(q + 1) * bd, q * bd:(q + 1) * bd].set(w[:, q])
        return out
    wg = jnp.concatenate([blockdiag(wa), blockdiag(wx)], axis=-1).astype(BF16)
    cw = jnp.zeros((8, ch), F32).at[:LRU_CONV_WIDTH].set(conv_w.reshape(LRU_CONV_WIDTH, ch))
    cur = lambda b, i: b * nt + i
    halo = lambda b, i: jnp.maximum((b * nt + i) * hpt - 1, 0)
    vec = pl.BlockSpec((1, ch), lambda b, i: (0, 0))
    return pl.pallas_call(
        _lru_kernel,
        grid=(batch, nt),
        in_specs=[pl.BlockSpec((t, ch), lambda b, i: (cur(b, i), x_col)),
                  pl.BlockSpec((t, ch), lambda b, i: (cur(b, i), g_col)),
                  pl.BlockSpec((8, ch), lambda b, i: (halo(b, i), x_col)),
                  pl.BlockSpec((8, ch), lambda b, i: (0, 0)),
                  vec,
                  pl.BlockSpec(wg.shape, lambda b, i: (0, 0, 0)),
                  vec, vec, vec],
        out_specs=pl.BlockSpec((t, ch), lambda b, i: (cur(b, i), 0)),
        out_shape=jax.ShapeDtypeStruct((n, ch), BF16),
        scratch_shapes=[pltpu.VMEM((t + 8, ch), F32), pltpu.VMEM((1, ch), F32)],
        compiler_params=_cparams(("parallel", "arbitrary")),
        name="rglru",
    )(z, z, z, cw, conv_b.reshape(1, ch), wg, ba.reshape(1, ch), bx.reshape(1, ch), lam.reshape(1, ch))


def _moe_kernel(be_ref, tok_ref, dst_ref, h_hbm, w1_ref, w3_ref, w2_ref, o_hbm, x0, x1, o0, o1, gsem, ssem):
    i = pl.program_id(0)
    last = pl.num_programs(0) - 1
    xs, outs = (x0, x1), (o0, o1)

    def gather_row(j, r, s):
        return pltpu.make_async_copy(h_hbm.at[pl.ds(tok_ref[j, r], 1)], xs[s].at[pl.ds(r, 1)], gsem.at[s])

    def scatter_row(j, r, s):
        return pltpu.make_async_copy(outs[s].at[pl.ds(r, 1)], o_hbm.at[pl.ds(dst_ref[j, r], 1)], ssem.at[s])

    def gather_block(s):
        return pltpu.make_async_copy(h_hbm.at[pl.ds(0, MOE_BLOCK)], xs[s], gsem.at[s])

    def scatter_block(s):
        return pltpu.make_async_copy(outs[s], o_hbm.at[pl.ds(0, MOE_BLOCK)], ssem.at[s])

    @pl.when(i == 0)
    def _():
        o1[...] = jnp.zeros_like(o1)
        for r in range(MOE_BLOCK):
            gather_row(0, r, 0).start()

    def step(s):
        t = 1 - s
        gather_block(s).wait()

        @pl.when(i >= 1)
        def _():
            scatter_block(s).wait()

        nxt = jnp.minimum(i + 1, last)
        prv = jnp.maximum(i - 1, 0)
        f = w1_ref.shape[-1]
        n_groups = f // MOE_FCHUNK
        per_group = MOE_BLOCK // n_groups
        for c in range(n_groups):
            @pl.when(i >= 0)
            def _():
                for r in range(c * per_group, (c + 1) * per_group):
                    gather_row(nxt, r, t).start()
                    scatter_row(prv, r, t).start()

            fs = slice(c * MOE_FCHUNK, (c + 1) * MOE_FCHUNK)
            x = xs[s][...].astype(BF16)
            h1 = jnp.dot(x, w1_ref[0, :, fs], preferred_element_type=F32)
            h3 = jnp.dot(x, w3_ref[0, :, fs], preferred_element_type=F32)
            hid = (h1 * _sigmoid(h1) * h3).astype(BF16)
            part = jnp.dot(hid, w2_ref[0, fs, :], preferred_element_type=F32)
            if c == 0:
                outs[s][...] = part
            else:
                outs[s][...] += part

        @pl.when(i == last)
        def _():
            for r in range(MOE_BLOCK):
                scatter_row(i, r, s).start()
            scatter_block(t).wait()
            scatter_block(s).wait()
            gather_block(t).wait()

    for s in range(2):
        pl.when(i % 2 == s)(functools.partial(step, s))


def _moe_experts(h, block_e, row_tok, row_dst, w1, w3, w2):
    n, d = h.shape
    f = w1.shape[-1]
    n_blocks = row_dst.shape[0]
    rows = n_blocks * MOE_BLOCK
    grid_spec = pltpu.PrefetchScalarGridSpec(
        num_scalar_prefetch=3,
        grid=(n_blocks,),
        in_specs=[pl.BlockSpec(memory_space=pl.ANY),
                  pl.BlockSpec((1, d, f), lambda i, be, tok, dst: (be[i], 0, 0)),
                  pl.BlockSpec((1, d, f), lambda i, be, tok, dst: (be[i], 0, 0)),
                  pl.BlockSpec((1, f, d), lambda i, be, tok, dst: (be[i], 0, 0))],
        out_specs=pl.BlockSpec(memory_space=pl.ANY),
        scratch_shapes=[pltpu.VMEM((MOE_BLOCK, d), F32)] * 4
                       + [pltpu.SemaphoreType.DMA((2,)), pltpu.SemaphoreType.DMA((2,))],
    )
    return pl.pallas_call(
        _moe_kernel,
        grid_spec=grid_spec,
        out_shape=jax.ShapeDtypeStruct((rows, d), F32),
        compiler_params=_cparams(("arbitrary",)),
        name="moe_experts",
    )(block_e, row_tok, row_dst, h, w1, w3, w2)


def _moe_plan(ridx, counts, n):
    nk = 2 * n
    e = ridx[0:2]
    rank = ridx[2:4]
    padded = (counts + MOE_BLOCK - 1) // MOE_BLOCK * MOE_BLOCK
    pend = jnp.cumsum(padded)
    pstart = pend - padded
    experts = jnp.arange(N_EXPERTS, dtype=jnp.int32)[:, None, None]
    dest = rank + jnp.sum(jnp.where(e[None] == experts, pstart[:, None, None], 0), axis=0)
    n_blocks = nk // MOE_BLOCK + N_EXPERTS
    rows = n_blocks * MOE_BLOCK
    result_row = jnp.arange(nk, dtype=jnp.int32)
    placed = jnp.full((rows,), -1, jnp.int32).at[dest.reshape(-1)].set(
        result_row, unique_indices=True, mode="promise_in_bounds")
    is_pad = placed < 0
    pad_rank = jnp.cumsum(is_pad.astype(jnp.int32)) - 1
    row_dst = jnp.where(is_pad, nk + pad_rank, placed)
    row_tok = jnp.where(is_pad, 0, jnp.where(placed >= n, placed - n, placed))
    block_start = jnp.arange(n_blocks, dtype=jnp.int32) * MOE_BLOCK
    block_e = jnp.minimum(jnp.sum(block_start[:, None] >= pend[None, :], axis=1), N_EXPERTS - 1).astype(jnp.int32)
    return block_e, row_tok.reshape(n_blocks, MOE_BLOCK), row_dst.reshape(n_blocks, MOE_BLOCK)


def _residual_kernel(x_ref, o1_ref, o2_ref, rg_ref, gate_ref, g_ref, out_ref, *, final):
    g = rg_ref[...].T
    y = g[:, 0:1] * o1_ref[...] + g[:, 1:2] * o2_ref[...]
    x2 = x_ref[...] + gate_ref[0] * y
    if final:
        ms = jnp.mean(x2 * x2, axis=-1, keepdims=True)
        x2 = x2 * lax.rsqrt(ms + EPS) * g_ref[...]
    out_ref[...] = x2


def _residual(x, o_all, rgate, gate, g, seq, tm, final):
    n, d = x.shape
    per_seq = seq // tm
    nt = n // tm
    return pl.pallas_call(
        functools.partial(_residual_kernel, final=final),
        grid=(nt,),
        in_specs=[pl.BlockSpec((tm, d), lambda i: (i, 0)),
                  pl.BlockSpec((tm, d), lambda i: (i, 0)),
                  pl.BlockSpec((tm, d), lambda i: (i + nt, 0)),
                  pl.BlockSpec((8, tm), lambda i: (0, i)),
                  pl.BlockSpec((1, 1, d), lambda i: (i // per_seq, 0, 0)),
                  pl.BlockSpec((1, d), lambda i: (0, 0))],
        out_specs=pl.BlockSpec((tm, d), lambda i: (i, 0)),
        out_shape=jax.ShapeDtypeStruct((n, d), F32),
        compiler_params=_cparams(("parallel",)),
        name="ffn_residual",
    )(x, o_all, o_all, rgate, gate, g.reshape(1, d))


def _pad_cols(w, width):
    return jnp.pad(w, ((0, 0), (0, width - w.shape[1])))


def _pad_rows(w, height):
    return jnp.pad(w, ((0, height - w.shape[0]), (0, 0)))


def kernel(x, c, positions, router_w, router_bias, ada_w, ada_b, norm_mix, norm_ffn, moe_w1, moe_w3, moe_w2, ab_w_in, ab_sinks, ab_conv_w, ab_conv_b, ab_conv_ln_g, ab_conv_ln_b, ab_w_out, cd_w_in, cd_shift_mu, cd_w0, cd_w2, cd_a0, cd_a2, cd_g2, cd_k_k, cd_k_a, cd_r_k, cd_ln_x_g, cd_ln_x_b, cd_lru_conv_w, cd_lru_conv_b, cd_lru_wa, cd_lru_ba, cd_lru_wx, cd_lru_bx, cd_lru_lambda, cd_w_out, final_norm):
    batch, seq, d = x.shape
    n = batch * seq
    depth = ada_w.shape[0]
    xf = x.reshape(n, d)
    mod = _ada_mod(c, ada_w, ada_b).reshape(depth, batch, 6, 1, d)
    cos_t, sin_t = _rope_tables(positions)
    att_q = ATT_HEADS * HEAD_DIM
    att_kv = ATT_KV_HEADS * HEAD_DIM
    tm = 256

    for layer in range(depth):
        j = layer // 2
        shift_m, scale_m, gate_m, shift_f, scale_f, gate_f = (mod[layer, :, q] for q in range(6))
        if layer % 2 == 0:
            w_in = ab_w_in[j]
            w_perm = jnp.concatenate([w_in[:, :att_q], w_in[:, att_q + 2 * att_kv:],
                                      w_in[:, att_q:att_q + 2 * att_kv]], axis=1).astype(BF16)
            z = _in_proj(xf, norm_mix[layer], shift_m, scale_m, w_perm, seq, 256, 512)
            conv_ch = ab_conv_w.shape[-1]
            kv0 = (att_q + 2 * conv_ch) // att_kv
            a1 = _attention(z, cos_t, sin_t, ab_sinks[j], batch, seq, kv0, kv0 + 1)
            a2 = _conformer_conv(z, ab_conv_w[j], ab_conv_b[j], ab_conv_ln_g[j], ab_conv_ln_b[j],
                                 batch, seq, att_q // conv_ch, att_q // conv_ch + 1, 128)
            w_out = ab_w_out[j]
        else:
            w_in = cd_w_in[j]
            dim = cd_w0.shape[-1]
            dl = cd_w2.shape[1]
            al = cd_a2.shape[1]
            gl = cd_g2.shape[1]
            s0 = 3 * dim
            seg_zw = w_in[:, s0:s0 + dl]
            seg_za = w_in[:, s0 + dl:s0 + dl + al]
            seg_zg = w_in[:, s0 + dl + al:s0 + dl + al + gl]
            s1 = s0 + dl + al + gl
            w_perm = jnp.concatenate([w_in[:, :s0], w_in[:, s1:], _pad_cols(seg_zw, LANES),
                                      _pad_cols(seg_za, LANES), seg_zg], axis=1).astype(BF16)
            mu = cd_shift_mu[j]
            mu_lo = jnp.concatenate([jnp.pad(mu[s0:s0 + dl], (0, LANES - dl)),
                                     jnp.pad(mu[s0 + dl:s0 + dl + al], (0, LANES - al)),
                                     mu[s0 + dl + al:s1]]).reshape(1, -1)
            row = lambda a: a.reshape(1, -1)
            p = dict(mu_r=row(mu[:dim]), mu_k=row(mu[dim:2 * dim]), mu_v=row(mu[2 * dim:s0]), mu_lo=mu_lo,
                     w0=row(cd_w0[j]), w2=_pad_rows(cd_w2[j], LANES).astype(BF16),
                     a0=row(cd_a0[j]), a2=_pad_rows(cd_a2[j], LANES).astype(BF16), g2=cd_g2[j].astype(BF16),
                     k_k=row(cd_k_k[j]), k_a=row(cd_k_a[j]), r_k=row(cd_r_k[j]),
                     ln_g=row(cd_ln_x_g[j]), ln_b=row(cd_ln_x_b[j]))
            z = _in_proj(xf, norm_mix[layer], shift_m, scale_m, w_perm, seq, 256, 512)
            lo_col = (5 * dim) // (4 * LANES)
            a1 = _rwkv(z, p, batch, seq, lo_col)
            a2 = _rglru(z, cd_lru_conv_w[j], cd_lru_conv_b[j], cd_lru_wa[j], cd_lru_ba[j],
                        cd_lru_wx[j], cd_lru_bx[j], cd_lru_lambda[j], batch, seq, 3, 4, 256)
            w_out = cd_w_out[j]
        x1, h, ridx, rgate, counts = _out_proj(a1, a2, w_out, xf, gate_m, norm_ffn[layer], shift_f, scale_f,
                                               router_w, router_bias, seq, tm)
        block_e, row_tok, row_dst = _moe_plan(ridx, counts[:, 0], n)
        o_all = _moe_experts(h, block_e, row_tok, row_dst, moe_w1[layer].astype(BF16),
                             moe_w3[layer].astype(BF16), moe_w2[layer].astype(BF16))
        xf = _residual(x1, o_all, rgate, gate_f, final_norm, seq, 512, layer == depth - 1)
    return xf.reshape(batch, seq, d)
```

```python
import functools

import jax
import jax.numpy as jnp
from jax import lax
from jax.experimental import pallas as pl
from jax.experimental.pallas import tpu as pltpu

F32 = jnp.float32
BF16 = jnp.bfloat16

EPS = 1e-6
LN_EPS = 1e-5
RWKV_LN_EPS = 64e-5
HEAD_DIM = 64
LANES = 128
MXU_TILE = 256
ATT_BLOCK = 128
WINDOW = 128
ATT_HEADS = 16
ATT_KV_HEADS = 4
CONV_WIDTH = 31
CONV_HALO = 32
LRU_CONV_WIDTH = 4
LRU_C = 8.0
N_EXPERTS = 16
N_GROUPS = 4
EXPERTS_PER_GROUP = 4
MOE_BLOCK = 256
MOE_FCHUNK = 256
RWKV_CHUNK = 64
RWKV_ROWS = 256
ROPE_THETA = 10000.0
VMEM_LIMIT = 56 * 1024 * 1024
PROJ_ROWS = 256
PROJ_COLS = 512
CONV_ROWS = 128
LRU_ROWS = 256
FINAL_ROWS = 512


def _cparams(sem):
    return pltpu.CompilerParams(dimension_semantics=sem, vmem_limit_bytes=VMEM_LIMIT)


def _bdot(a, b):
    return jnp.dot(a.astype(BF16), b.astype(BF16), preferred_element_type=F32)


def _split2(x):
    hi = x.astype(BF16)
    lo = (x - hi.astype(F32)).astype(BF16)
    return hi, lo


def _dot_left_exact(m_bf16, x):
    hi, lo = _split2(x)
    d = functools.partial(jnp.dot, preferred_element_type=F32)
    return d(m_bf16, hi) + d(m_bf16, lo)


def _dot_right_exact(x, m_bf16):
    hi, lo = _split2(x)
    d = functools.partial(jnp.dot, preferred_element_type=F32)
    return d(hi, m_bf16) + d(lo, m_bf16)


def _dot_3pass(a, b):
    a_hi = a.astype(BF16)
    a_lo = (a - a_hi.astype(F32)).astype(BF16)
    b_hi = b.astype(BF16)
    b_lo = (b - b_hi.astype(F32)).astype(BF16)
    d = functools.partial(jnp.dot, preferred_element_type=F32)
    return d(a_hi, b_hi) + d(a_hi, b_lo) + d(a_lo, b_hi)


def _sigmoid(x):
    return 1.0 / (1.0 + jnp.exp(-x))


def _softplus(x):
    return jnp.maximum(x, 0.0) + jnp.log(1.0 + jnp.exp(-jnp.abs(x)))


def _rms_mod(x, g, shift, scale):
    ms = jnp.mean(x * x, axis=-1, keepdims=True)
    return (x * lax.rsqrt(ms + EPS) * g) * (1.0 + scale) + shift


def _ada_kernel(c_ref, w_ref, b_ref, o_ref):
    c = c_ref[...]
    o_ref[0] = _dot_3pass(c * _sigmoid(c), w_ref[0]) + b_ref[0]


def _ada_mod(c, ada_w, ada_b):
    depth, d, n6 = ada_w.shape
    b = c.shape[0]
    bp = 8
    tn = 768
    c_pad = jnp.zeros((bp, d), F32).at[:b].set(c)
    out = pl.pallas_call(
        _ada_kernel,
        grid=(depth, n6 // tn),
        in_specs=[pl.BlockSpec((bp, d), lambda l, j: (0, 0)),
                  pl.BlockSpec((1, d, tn), lambda l, j: (l, 0, j)),
                  pl.BlockSpec((1, 1, tn), lambda l, j: (l, 0, j))],
        out_specs=pl.BlockSpec((1, bp, tn), lambda l, j: (l, 0, j)),
        out_shape=jax.ShapeDtypeStruct((depth, bp, n6), F32),
        compiler_params=_cparams(("parallel", "parallel")),
        name="ada_mod",
    )(c_pad, ada_w, ada_b.reshape(depth, 1, n6))
    return out[:, :b]


def _rope_kernel(pos_ref, f_ref, cos_ref, sin_ref):
    ang = pos_ref[...].astype(F32) * f_ref[...]
    lane = lax.broadcasted_iota(jnp.int32, ang.shape, 1)
    s = jnp.sin(ang)
    cos_ref[...] = jnp.cos(ang)
    sin_ref[...] = jnp.where((lane & 32) == 0, -s, s)


def _rope_tables(positions):
    n = positions.size
    half = HEAD_DIM // 2
    inv_freq = ROPE_THETA ** (-jnp.arange(half, dtype=F32) / half)
    f_row = jnp.tile(inv_freq, LANES // half).reshape(1, LANES)
    tm = 512
    return pl.pallas_call(
        _rope_kernel,
        grid=(n // tm,),
        in_specs=[pl.BlockSpec((tm, 1), lambda i: (i, 0)),
                  pl.BlockSpec((1, LANES), lambda i: (0, 0))],
        out_specs=[pl.BlockSpec((tm, LANES), lambda i: (i, 0))] * 2,
        out_shape=[jax.ShapeDtypeStruct((n, LANES), F32)] * 2,
        compiler_params=_cparams(("parallel",)),
        name="rope_tables",
    )(positions.reshape(n, 1), f_row)


def _in_proj_kernel(*refs, tn, combine):
    if combine:
        x1_ref, o1_ref, o2_ref, rg_ref, gate_ref, g_ref, sh_ref, sc_ref, w_ref, o_ref, x_out_ref = refs
        gates = rg_ref[...].T
        x = x1_ref[...] + gate_ref[0] * (gates[:, 0:1] * o1_ref[...] + gates[:, 1:2] * o2_ref[...])
        x_out_ref[...] = x
    else:
        x_ref, g_ref, sh_ref, sc_ref, w_ref, o_ref = refs
        x = x_ref[...]
    h = _rms_mod(x, g_ref[...], sh_ref[0], sc_ref[0]).astype(BF16)
    for c in range(o_ref.shape[1] // tn):
        o_ref[:, c * tn:(c + 1) * tn] = jnp.dot(h, w_ref[:, c * tn:(c + 1) * tn], preferred_element_type=F32)


def _in_proj(x, g, shift, scale, w_bf16, seq, tm, tn, combine=None):
    n, d = x.shape
    nout = w_bf16.shape[1]
    per_seq = seq // tm
    nt = n // tm
    row_spec = pl.BlockSpec((tm, d), lambda i: (i, 0))
    mod_spec = pl.BlockSpec((1, 1, d), lambda i: (i // per_seq, 0, 0))
    tail_specs = [pl.BlockSpec((1, d), lambda i: (0, 0)), mod_spec, mod_spec,
                  pl.BlockSpec((d, nout), lambda i: (0, 0), pipeline_mode=pl.Buffered(1))]
    tail_args = (g.reshape(1, d), shift, scale, w_bf16)
    z_spec = pl.BlockSpec((tm, nout), lambda i: (i, 0))
    z_shape = jax.ShapeDtypeStruct((n, nout), F32)
    if combine is None:
        in_specs, args, out_specs, out_shape = [row_spec], (x,), z_spec, z_shape
    else:
        o_all, rgate, gate_f = combine
        in_specs = [row_spec, row_spec, pl.BlockSpec((tm, d), lambda i: (i + nt, 0)),
                    pl.BlockSpec((8, tm), lambda i: (0, i)), mod_spec]
        args = (x, o_all, o_all, rgate, gate_f)
        out_specs = [z_spec, row_spec]
        out_shape = [z_shape, jax.ShapeDtypeStruct((n, d), F32)]
    return pl.pallas_call(
        functools.partial(_in_proj_kernel, tn=tn, combine=combine is not None),
        grid=(nt,),
        in_specs=in_specs + tail_specs,
        out_specs=out_specs,
        out_shape=out_shape,
        compiler_params=_cparams(("parallel",)),
        name="in_proj",
    )(*args, *tail_args)


def _route(logits_t, bias_col):
    m = jnp.max(logits_t, axis=0, keepdims=True)
    e = jnp.exp(logits_t - m)
    probs = e / jnp.sum(e, axis=0, keepdims=True)
    sel = probs + bias_col
    s = [sel[i:i + 1, :] for i in range(N_EXPERTS)]
    pr = [probs[i:i + 1, :] for i in range(N_EXPERTS)]

    def top2_sum(v):
        best = v[0] + v[1]
        for i in range(len(v)):
            for j in range(i + 1, len(v)):
                if (i, j) != (0, 1):
                    best = jnp.maximum(best, v[i] + v[j])
        return best

    gs = [top2_sum(s[4 * g:4 * g + 4]) for g in range(N_GROUPS)]
    best = gs[0]
    gi = jnp.zeros_like(best, dtype=jnp.int32)
    for g in range(1, N_GROUPS):
        upd = gs[g] > best
        gi = jnp.where(upd, g, gi)
        best = jnp.where(upd, gs[g], best)

    def pick_group(rows):
        out = []
        for l in range(EXPERTS_PER_GROUP):
            v = rows[l]
            for g in range(1, N_GROUPS):
                v = jnp.where(gi == g, rows[4 * g + l], v)
            out.append(v)
        return out

    ig = pick_group(s)
    pg = pick_group(pr)

    def argmax4(v):
        bv = v[0]
        bi = jnp.zeros_like(gi)
        for l in range(1, EXPERTS_PER_GROUP):
            upd = v[l] > bv
            bi = jnp.where(upd, l, bi)
            bv = jnp.where(upd, v[l], bv)
        return bi

    i1 = argmax4(ig)
    i2 = argmax4([jnp.where(i1 == l, -jnp.inf, ig[l]) for l in range(EXPERTS_PER_GROUP)])

    def pick(v, idx):
        o = v[0]
        for l in range(1, EXPERTS_PER_GROUP):
            o = jnp.where(idx == l, v[l], o)
        return o

    p1 = pick(pg, i1)
    p2 = pick(pg, i2)
    tot = p1 + p2
    return gi * EXPERTS_PER_GROUP + i1, gi * EXPERTS_PER_GROUP + i2, p1 / tot, p2 / tot


def _out_proj_kernel(a1_ref, a2_ref, w1_ref, w2_ref, x_ref, gate_ref, g_ref, sh_ref, sc_ref,
                     rw_ref, rb_ref, upper_ref, xo_ref, h_ref, ridx_ref, rgate_ref, cnt_ref, carry):
    @pl.when(pl.program_id(0) == 0)
    def _():
        carry[...] = jnp.zeros_like(carry)

    y = (jnp.dot(a1_ref[...], w1_ref[...], preferred_element_type=F32)
         + jnp.dot(a2_ref[...], w2_ref[...], preferred_element_type=F32))
    x1 = x_ref[...] + gate_ref[0] * y
    xo_ref[...] = x1
    h = _rms_mod(x1, g_ref[...], sh_ref[0], sc_ref[0])
    h_ref[...] = h
    logits = _dot_3pass(h, rw_ref[...])
    logits_t = logits.T[:N_EXPERTS, :]
    e1, e2, g1, g2 = _route(logits_t, rb_ref[...])
    erow = lax.broadcasted_iota(jnp.int32, logits_t.shape, 0)
    hit1 = erow == e1
    hit2 = erow == e2
    onehot = jnp.where(hit1 | hit2, 1.0, 0.0)
    before = jnp.dot(onehot.astype(BF16), upper_ref[...], preferred_element_type=F32) + carry[:, 0:1]
    rank1 = jnp.sum(jnp.where(hit1, before, 0.0), axis=0, keepdims=True).astype(jnp.int32)
    rank2 = jnp.sum(jnp.where(hit2, before, 0.0), axis=0, keepdims=True).astype(jnp.int32)
    total = carry[...] + jnp.sum(onehot, axis=1, keepdims=True)
    carry[...] = total
    cnt_ref[...] = total.astype(jnp.int32)
    row = lax.broadcasted_iota(jnp.int32, ridx_ref.shape, 0)
    ridx_ref[...] = jnp.where(row == 0, e1, jnp.where(row == 1, e2,
                              jnp.where(row == 2, rank1, jnp.where(row == 3, rank2, 0))))
    rgate_ref[...] = jnp.where(row == 0, g1, jnp.where(row == 1, g2, 0.0))


def _out_proj(a1, a2, w_out, x, gate, g, shift, scale, router_w, router_bias, seq, tm):
    n, d = x.shape
    ka = a1.shape[1]
    per_seq = seq // tm
    w1 = w_out[:ka].astype(BF16)
    w2 = w_out[ka:].astype(BF16)
    rw = jnp.zeros((d, LANES), F32).at[:, :N_EXPERTS].set(router_w)
    rb = router_bias.astype(F32).reshape(N_EXPERTS, 1)
    upper = (jnp.arange(tm)[:, None] < jnp.arange(tm)[None, :]).astype(BF16)
    mod_spec = pl.BlockSpec((1, 1, d), lambda i: (i // per_seq, 0, 0))
    return pl.pallas_call(
        _out_proj_kernel,
        grid=(n // tm,),
        in_specs=[pl.BlockSpec((tm, ka), lambda i: (i, 0)),
                  pl.BlockSpec((tm, ka), lambda i: (i, 0)),
                  pl.BlockSpec((ka, d), lambda i: (0, 0), pipeline_mode=pl.Buffered(1)),
                  pl.BlockSpec((ka, d), lambda i: (0, 0), pipeline_mode=pl.Buffered(1)),
                  pl.BlockSpec((tm, d), lambda i: (i, 0)),
                  mod_spec,
                  pl.BlockSpec((1, d), lambda i: (0, 0)),
                  mod_spec, mod_spec,
                  pl.BlockSpec((d, LANES), lambda i: (0, 0)),
                  pl.BlockSpec((N_EXPERTS, 1), lambda i: (0, 0)),
                  pl.BlockSpec((tm, tm), lambda i: (0, 0))],
        out_specs=[pl.BlockSpec((tm, d), lambda i: (i, 0)),
                   pl.BlockSpec((tm, d), lambda i: (i, 0)),
                   pl.BlockSpec((8, tm), lambda i: (0, i)),
                   pl.BlockSpec((8, tm), lambda i: (0, i)),
                   pl.BlockSpec((N_EXPERTS, LANES), lambda i: (0, 0))],
        out_shape=[jax.ShapeDtypeStruct((n, d), F32),
                   jax.ShapeDtypeStruct((n, d), F32),
                   jax.ShapeDtypeStruct((8, n), jnp.int32),
                   jax.ShapeDtypeStruct((8, n), F32),
                   jax.ShapeDtypeStruct((N_EXPERTS, LANES), jnp.int32)],
        scratch_shapes=[pltpu.VMEM((N_EXPERTS, LANES), F32)],
        compiler_params=_cparams(("arbitrary",)),
        name="out_proj_route",
    )(a1, a2, w1, w2, x, gate, g.reshape(1, d), shift, scale, rw, rb, upper)


def _rope_apply(x, cos, sin_signed):
    outs = []
    lane = lax.broadcasted_iota(jnp.int32, cos.shape, 1)
    first_half = (lane & 32) == 0
    for c in range(x.shape[1] // LANES):
        xc = x[:, c * LANES:(c + 1) * LANES]
        partner = jnp.where(first_half, pltpu.roll(xc, LANES - 32, axis=1), pltpu.roll(xc, 32, axis=1))
        outs.append(xc * cos + partner * sin_signed)
    return outs


def _attn_kernel(sink_ref, q_ref, kp_ref, kc_ref, vp_ref, vc_ref, cq_ref, sq_ref, cp_ref, sp_ref, o_ref):
    has_prev = pl.program_id(1) > 0
    cq, sq = cq_ref[...], sq_ref[...]
    q_cols = _rope_apply(q_ref[...] * (HEAD_DIM ** -0.5), cq, sq)
    k_cols = [jnp.concatenate([kp, kc], axis=0) for kp, kc in
              zip(_rope_apply(kp_ref[...], cp_ref[...], sp_ref[...]), _rope_apply(kc_ref[...], cq, sq))]
    v_all = jnp.concatenate([vp_ref[...], vc_ref[...]], axis=0).astype(BF16)
    qi = lax.broadcasted_iota(jnp.int32, (ATT_BLOCK, 2 * ATT_BLOCK), 0)
    kj = lax.broadcasted_iota(jnp.int32, (ATT_BLOCK, 2 * ATT_BLOCK), 1)
    valid = ((kj > qi) & (kj < ATT_BLOCK) & has_prev) | ((kj >= ATT_BLOCK) & (kj - ATT_BLOCK <= qi))
    group = ATT_HEADS // ATT_KV_HEADS
    heads = range(ATT_HEADS)
    scores = []
    for hq in heads:
        hk = hq // group
        qh = q_cols[hq // 2][:, (hq % 2) * HEAD_DIM:(hq % 2 + 1) * HEAD_DIM].astype(BF16)
        kh = k_cols[hk // 2][:, (hk % 2) * HEAD_DIM:(hk % 2 + 1) * HEAD_DIM].astype(BF16)
        s = lax.dot_general(qh, kh, (((1,), (1,)), ((), ())), preferred_element_type=F32)
        scores.append(jnp.where(valid, s, -1e30))
    tops = [jnp.maximum(jnp.max(scores[hq], axis=-1, keepdims=True), sink_ref[hq]) for hq in heads]
    probs = [jnp.exp(scores[hq] - tops[hq]) for hq in heads]
    dens = [jnp.sum(probs[hq], axis=-1, keepdims=True) + jnp.exp(sink_ref[hq] - tops[hq]) for hq in heads]
    outs = [jnp.dot(probs[hq].astype(BF16), v_all[:, (hq // group) * HEAD_DIM:(hq // group + 1) * HEAD_DIM],
                    preferred_element_type=F32) / dens[hq] for hq in heads]
    out_cols = [jnp.concatenate([outs[2 * pair], outs[2 * pair + 1]], axis=1) for pair in range(ATT_HEADS // 2)]
    o_ref[...] = jnp.concatenate(out_cols, axis=1).astype(o_ref.dtype)


def _attention(z, cos_t, sin_t, sinks, batch, seq, k_col, v_col):
    n = z.shape[0]
    nb = seq // ATT_BLOCK
    att_q = ATT_HEADS * HEAD_DIM
    att_kv = ATT_KV_HEADS * HEAD_DIM
    cur = lambda b, i: b * nb + i
    prev = lambda b, i: b * nb + jnp.maximum(i - 1, 0)
    return pl.pallas_call(
        _attn_kernel,
        grid=(batch, nb),
        in_specs=[pl.BlockSpec(memory_space=pltpu.SMEM),
                  pl.BlockSpec((ATT_BLOCK, att_q), lambda b, i: (cur(b, i), 0)),
                  pl.BlockSpec((ATT_BLOCK, att_kv), lambda b, i: (prev(b, i), k_col)),
                  pl.BlockSpec((ATT_BLOCK, att_kv), lambda b, i: (cur(b, i), k_col)),
                  pl.BlockSpec((ATT_BLOCK, att_kv), lambda b, i: (prev(b, i), v_col)),
                  pl.BlockSpec((ATT_BLOCK, att_kv), lambda b, i: (cur(b, i), v_col)),
                  pl.BlockSpec((ATT_BLOCK, LANES), lambda b, i: (cur(b, i), 0)),
                  pl.BlockSpec((ATT_BLOCK, LANES), lambda b, i: (cur(b, i), 0)),
                  pl.BlockSpec((ATT_BLOCK, LANES), lambda b, i: (prev(b, i), 0)),
                  pl.BlockSpec((ATT_BLOCK, LANES), lambda b, i: (prev(b, i), 0))],
        out_specs=pl.BlockSpec((ATT_BLOCK, att_q), lambda b, i: (cur(b, i), 0)),
        out_shape=jax.ShapeDtypeStruct((n, att_q), BF16),
        compiler_params=_cparams(("parallel", "parallel")),
        name="swa_attention",
    )(sinks.astype(F32), z, z, z, z, z, cos_t, sin_t, cos_t, sin_t)


def _conv_kernel(val_ref, gate_ref, hval_ref, hgate_ref, w_ref, b_ref, g_ref, beta_ref, o_ref, ybuf, acc, shifted):
    t = val_ref.shape[0]
    ch = val_ref.shape[1]
    first = pl.program_id(1) == 0
    yh = hval_ref[...] * _sigmoid(hgate_ref[...])
    ybuf[0:CONV_HALO, :] = jnp.where(first, 0.0, yh)
    ybuf[CONV_HALO:, :] = val_ref[...] * _sigmoid(gate_ref[...])
    off = CONV_HALO - (CONV_WIDTH - 1)
    span = t + CONV_HALO - 8
    for ph in range(1, 8):
        shifted[ph - 1] = ybuf[ph:ph + span, :]
    for cb in range(ch // LANES):
        cs = slice(cb * LANES, (cb + 1) * LANES)
        a = jnp.broadcast_to(b_ref[:, cs], (t, LANES))
        for j in range(CONV_WIDTH):
            ph, base = (off + j) % 8, (off + j) // 8 * 8
            src = ybuf if ph == 0 else shifted.at[ph - 1]
            wj = jnp.broadcast_to(w_ref[j:j + 1, cs], (t, LANES))
            a = a + wj * src[base:base + t, cs]
        acc[:, cs] = a
    y = acc[...]
    mu = jnp.mean(y, axis=-1, keepdims=True)
    yc = y - mu
    var = jnp.mean(yc * yc, axis=-1, keepdims=True)
    ln = yc * lax.rsqrt(var + LN_EPS) * g_ref[...] + beta_ref[...]
    o_ref[...] = (ln * _sigmoid(ln)).astype(o_ref.dtype)


def _conformer_conv(z, conv_w, conv_b, ln_g, ln_b, batch, seq, val_col, gate_col, t):
    n = z.shape[0]
    ch = conv_w.shape[-1]
    nt = seq // t
    hpt = t // CONV_HALO
    w = jnp.zeros((CONV_HALO, ch), F32).at[:CONV_WIDTH].set(conv_w.reshape(CONV_WIDTH, ch))
    cur = lambda b, i: b * nt + i
    halo = lambda b, i: jnp.maximum((b * nt + i) * hpt - 1, 0)
    vec = pl.BlockSpec((1, ch), lambda b, i: (0, 0))
    return pl.pallas_call(
        _conv_kernel,
        grid=(batch, nt),
        in_specs=[pl.BlockSpec((t, ch), lambda b, i: (cur(b, i), val_col)),
                  pl.BlockSpec((t, ch), lambda b, i: (cur(b, i), gate_col)),
                  pl.BlockSpec((CONV_HALO, ch), lambda b, i: (halo(b, i), val_col)),
                  pl.BlockSpec((CONV_HALO, ch), lambda b, i: (halo(b, i), gate_col)),
                  pl.BlockSpec((CONV_HALO, ch), lambda b, i: (0, 0)),
                  vec, vec, vec],
        out_specs=pl.BlockSpec((t, ch), lambda b, i: (cur(b, i), 0)),
        out_shape=jax.ShapeDtypeStruct((n, ch), BF16),
        scratch_shapes=[pltpu.VMEM((t + CONV_HALO, ch), F32), pltpu.VMEM((t, ch), F32),
                        pltpu.VMEM((7, t + CONV_HALO - 8, ch), F32)],
        compiler_params=_cparams(("parallel", "parallel")),
        name="conformer_conv",
    )(z, z, z, z, w, conv_b.reshape(1, ch), ln_g.reshape(1, ch), ln_b.reshape(1, ch))


def _shift_lerp(z, halo, mu, first):
    prev = pltpu.roll(z, 1, axis=0)
    row = lax.broadcasted_iota(jnp.int32, z.shape, 0)
    hrow = jnp.where(first, 0.0, halo[7:8, :])
    prev = jnp.where(row == 0, hrow, prev)
    return z + mu * (prev - z)


def _rwkv_kernel(r_ref, k_ref, v_ref, lo_ref, rh_ref, kh_ref, vh_ref, loh_ref,
                 mur_ref, muk_ref, muv_ref, mulo_ref, w0_ref, w2_ref, a0_ref, a2_ref, g2_ref,
                 kk_ref, ka_ref, rk_ref, lng_ref, lnb_ref, seg_ref, tril_ref,
                 o_ref, hs_ref):
    first = pl.program_id(1) == 0

    @pl.when(first)
    def _():
        hs_ref[...] = jnp.zeros_like(hs_ref)

    r = _shift_lerp(r_ref[...], rh_ref[...], mur_ref[...], first)
    k = _shift_lerp(k_ref[...], kh_ref[...], muk_ref[...], first)
    v = _shift_lerp(v_ref[...], vh_ref[...], muv_ref[...], first)
    lo = _shift_lerp(lo_ref[...], loh_ref[...], mulo_ref[...], first)
    zw, za, zg = lo[:, 0:LANES], lo[:, LANES:2 * LANES], lo[:, 2 * LANES:]

    seg = seg_ref[...]
    seg_w = seg.shape[0]

    def head_sum(x):
        return jnp.concatenate([_dot_right_exact(x[:, c:c + seg_w], seg) for c in range(0, x.shape[1], seg_w)],
                               axis=1)

    w_raw = w0_ref[...] + _bdot(jnp.tanh(zw), w2_ref[...])
    log_decay = -jnp.exp(-_softplus(-w_raw) - 0.5)
    a_gate = _sigmoid(a0_ref[...] + _bdot(za, a2_ref[...]))
    g = _bdot(_sigmoid(zg), g2_ref[...])
    kk = k * kk_ref[...]
    kk = kk * lax.rsqrt(jnp.maximum(head_sum(kk * kk), 1e-24))
    k_mod = k * (1.0 + (a_gate - 1.0) * ka_ref[...])
    b_vec = kk * a_gate

    n_chunks = r.shape[0] // RWKV_CHUNK
    rows = [slice(c * RWKV_CHUNK, (c + 1) * RWKV_CHUNK) for c in range(n_chunks)]
    cum = _dot_left_exact(tril_ref[...], log_decay)
    cum_last = [cum[rs.stop - 1:rs.stop, :] for rs in rows]
    w_last = [jnp.exp(cl) for cl in cum_last]
    inv_w = jnp.exp(-cum)
    to_end = jnp.concatenate([jnp.exp(cum_last[c] - cum[rows[c], :]) for c in range(n_chunks)], axis=0)
    a_til = -kk * jnp.exp(cum - log_decay)
    r_til = r * jnp.exp(cum)
    b_til = b_vec * inv_w
    k_til = k_mod * inv_w
    b_hat = b_vec * to_end
    k_hat = k_mod * to_end

    lane = lax.broadcasted_iota(jnp.int32, (RWKV_CHUNK, LANES), 1)
    m0 = lane < HEAD_DIM
    ri = lax.broadcasted_iota(jnp.int32, (LANES, LANES), 0)
    ci = lax.broadcasted_iota(jnp.int32, (LANES, LANES), 1)
    same = (ri < HEAD_DIM) == (ci < HEAD_DIM)
    mask_sl = same & (ci < ri)
    mask_li = same & (ci <= ri)
    eye = ri == ci

    def stack(x):
        return jnp.concatenate([jnp.where(m0, x, 0.0), jnp.where(m0, 0.0, x)], axis=0)

    nt_dims = (((1,), (1,)), ((), ()))
    dot = functools.partial(jnp.dot, preferred_element_type=F32)
    n_pairs = r.shape[1] // LANES
    cols = [slice(p * LANES, (p + 1) * LANES) for p in range(n_pairs)]
    units = [(rows[c], cols[p]) for c in range(n_chunks) for p in range(n_pairs)]
    ids = range(len(units))
    xa_f = [stack(a_til[u]) for u in units]
    xr_f = [stack(r_til[u]) for u in units]
    vs = [stack(v[u]).astype(BF16) for u in units]
    m = [lax.dot_general(jnp.concatenate([xa_f[i], xr_f[i]], axis=0).astype(BF16),
                         jnp.concatenate([b_til[units[i]]] * 2 + [k_til[units[i]]] * 2, axis=0).astype(BF16),
                         nt_dims, preferred_element_type=F32) for i in ids]
    power = [jnp.where(mask_sl, m[i][:LANES, :LANES], 0.0).astype(BF16) for i in ids]
    a_ak = [jnp.where(mask_sl, m[i][:LANES, LANES:], 0.0).astype(BF16) for i in ids]
    z = [jnp.concatenate([xa_f[i], dot(a_ak[i], vs[i])], axis=1) for i in ids]
    for level in range(6):
        z = [z[i] + dot(power[i], z[i].astype(BF16)) for i in ids]
        if level < 5:
            power = [dot(power[i], power[i]).astype(BF16) for i in ids]
    zb = [z[i].astype(BF16) for i in ids]
    m_rb = [jnp.where(mask_li, m[i][LANES:, :LANES], 0.0).astype(BF16) for i in ids]
    m_rk = [jnp.where(mask_li, m[i][LANES:, LANES:], 0.0).astype(BF16) for i in ids]
    q = [jnp.concatenate([xr_f[i], dot(m_rk[i], vs[i])], axis=1) + dot(m_rb[i], zb[i]) for i in ids]
    ge = [dot(stack(b_hat[units[i]]).T.astype(BF16), zb[i]) for i in ids]
    e_mat = [ge[i][:, LANES:] + dot(stack(k_hat[units[i]]).T.astype(BF16), vs[i]) for i in ids]
    qg = [jnp.concatenate([q[i][:, :LANES],
                           ge[i][:, :LANES] + jnp.where(eye, w_last[i // n_pairs][:, cols[i % n_pairs]], 0.0)],
                          axis=0).astype(BF16) for i in ids]
    state = [hs_ref[p] for p in range(n_pairs)]
    y_rows = []
    for c in range(n_chunks):
        y_cols = []
        for p in range(n_pairs):
            i = c * n_pairs + p
            yh = dot(qg[i], state[p].astype(BF16))
            state[p] = yh[LANES:] + e_mat[i]
            y2 = yh[:LANES] + q[i][:, LANES:]
            y_cols.append(y2[:RWKV_CHUNK] + y2[RWKV_CHUNK:])
        y_rows.append(jnp.concatenate(y_cols, axis=1))
    for p in range(n_pairs):
        hs_ref[p] = state[p]
    y = jnp.concatenate(y_rows, axis=0)

    inv_hd = 1.0 / HEAD_DIM
    mu = head_sum(y) * inv_hd
    yc = y - mu
    var = head_sum(yc * yc) * inv_hd
    yn = yc * lax.rsqrt(var + RWKV_LN_EPS) * lng_ref[...] + lnb_ref[...]
    bonus = head_sum(r * k_mod * rk_ref[...]) * v
    o_ref[...] = ((yn + bonus) * g).astype(o_ref.dtype)


def _rwkv(z, p, batch, seq, lo_col):
    n = z.shape[0]
    dim = p["w0"].shape[-1]
    t = RWKV_ROWS
    nt = seq // t
    hpt = t // 8
    lo_w = 4 * LANES
    cur = lambda b, i: b * nt + i
    halo = lambda b, i: jnp.maximum((b * nt + i) * hpt - 1, 0)
    si = jnp.arange(MXU_TILE)
    seg = (si[:, None] // HEAD_DIM == si[None, :] // HEAD_DIM).astype(BF16)
    ti = jnp.arange(t)
    tril = ((ti[:, None] >= ti[None, :]) & (ti[:, None] // RWKV_CHUNK == ti[None, :] // RWKV_CHUNK)).astype(BF16)

    def full(a):
        return pl.BlockSpec(a.shape, lambda b, i: (0,) * a.ndim)

    consts = [p["mu_r"], p["mu_k"], p["mu_v"], p["mu_lo"], p["w0"], p["w2"], p["a0"], p["a2"], p["g2"],
              p["k_k"], p["k_a"], p["r_k"], p["ln_g"], p["ln_b"], seg, tril]
    col = lambda c: pl.BlockSpec((t, dim), lambda b, i: (cur(b, i), c))
    hcol = lambda c: pl.BlockSpec((8, dim), lambda b, i: (halo(b, i), c))
    return pl.pallas_call(
        _rwkv_kernel,
        grid=(batch, nt),
        in_specs=[col(0), col(1), col(2), pl.BlockSpec((t, lo_w), lambda b, i: (cur(b, i), lo_col)),
                  hcol(0), hcol(1), hcol(2), pl.BlockSpec((8, lo_w), lambda b, i: (halo(b, i), lo_col))]
                 + [full(a) for a in consts],
        out_specs=pl.BlockSpec((t, dim), lambda b, i: (cur(b, i), 0)),
        out_shape=jax.ShapeDtypeStruct((n, dim), BF16),
        scratch_shapes=[pltpu.VMEM((dim // LANES, LANES, LANES), F32)],
        compiler_params=_cparams(("parallel", "arbitrary")),
        name="rwkv7_chunked",
    )(z, z, z, z, z, z, z, z, *consts)


def _lru_kernel(x_ref, gb_ref, xh_ref, cw_ref, cb_ref, wg_ref, ba_ref, bx_ref, lam_ref, o_ref,
                xbuf, carry):
    t, ch = x_ref.shape
    first = pl.program_id(1) == 0

    @pl.when(first)
    def _():
        carry[...] = jnp.zeros_like(carry)

    xbuf[0:8, :] = jnp.where(first, 0.0, xh_ref[...])
    xbuf[8:, :] = x_ref[...]
    off = 8 - (LRU_CONV_WIDTH - 1)
    xc = jnp.broadcast_to(cb_ref[...], (t, ch))
    for j in range(LRU_CONV_WIDTH):
        xc = xc + cw_ref[j:j + 1, :] * xbuf[off + j:off + j + t, :]
    gr, gi = [], []
    for p in range(ch // LANES):
        gp = _bdot(xc[:, p * LANES:(p + 1) * LANES], wg_ref[p])
        gr.append(gp[:, :LANES])
        gi.append(gp[:, LANES:])
    r = _sigmoid(jnp.concatenate(gr, axis=1) + ba_ref[...])
    i = _sigmoid(jnp.concatenate(gi, axis=1) + bx_ref[...])
    log_a = -LRU_C * r * _softplus(-lam_ref[...])
    a = jnp.exp(log_a)
    u = jnp.sqrt(1.0 - jnp.exp(2.0 * log_a)) * (i * xc)
    a = a.reshape(t // 8, 8, ch)
    u = u.reshape(t // 8, 8, ch)
    row = lax.broadcasted_iota(jnp.int32, a.shape, 1)
    d = 1
    while d < 8:
        keep = row >= d
        a_sh = jnp.where(keep, pltpu.roll(a, d, axis=1), 1.0)
        u_sh = jnp.where(keep, pltpu.roll(u, d, axis=1), 0.0)
        u = a * u_sh + u
        a = a * a_sh
        d *= 2
    h_prev = carry[...]
    groups = []
    for g in range(t // 8):
        hg = u[g] + a[g] * h_prev
        groups.append(hg)
        h_prev = hg[7:8, :]
    h = jnp.concatenate(groups, axis=0)
    carry[...] = h_prev
    gb = gb_ref[...]
    gelu = 0.5 * gb * (1.0 + jnp.tanh(0.7978845608028654 * (gb + 0.044715 * (gb * gb * gb))))
    o_ref[...] = (h * gelu).astype(o_ref.dtype)


def _rglru(z, conv_w, conv_b, wa, ba, wx, bx, lam, batch, seq, x_col, g_col, t):
    n = z.shape[0]
    ch = lam.shape[-1]
    nt = seq // t
    hpt = t // 8
    nblk, bd = wa.shape[0], wa.shape[1]
    per = LANES // bd
    def blockdiag(w):
        w = w.reshape(nblk // per, per, bd, bd)
        out = jnp.zeros((nblk // per, LANES, LANES), F32)
        for q in range(per):
            out = out.at[:, q * bd:(q + 1) * bd, q * bd:(q + 1) * bd].set(w[:, q])
        return out
    wg = jnp.concatenate([blockdiag(wa), blockdiag(wx)], axis=-1).astype(BF16)
    cw = jnp.zeros((8, ch), F32).at[:LRU_CONV_WIDTH].set(conv_w.reshape(LRU_CONV_WIDTH, ch))
    cur = lambda b, i: b * nt + i
    halo = lambda b, i: jnp.maximum((b * nt + i) * hpt - 1, 0)
    vec = pl.BlockSpec((1, ch), lambda b, i: (0, 0))
    return pl.pallas_call(
        _lru_kernel,
        grid=(batch, nt),
        in_specs=[pl.BlockSpec((t, ch), lambda b, i: (cur(b, i), x_col)),
                  pl.BlockSpec((t, ch), lambda b, i: (cur(b, i), g_col)),
                  pl.BlockSpec((8, ch), lambda b, i: (halo(b, i), x_col)),
                  pl.BlockSpec((8, ch), lambda b, i: (0, 0)),
                  vec,
                  pl.BlockSpec(wg.shape, lambda b, i: (0, 0, 0)),
                  vec, vec, vec],
        out_specs=pl.BlockSpec((t, ch), lambda b, i: (cur(b, i), 0)),
        out_shape=jax.ShapeDtypeStruct((n, ch), BF16),
        scratch_shapes=[pltpu.VMEM((t + 8, ch), F32), pltpu.VMEM((1, ch), F32)],
        compiler_params=_cparams(("parallel", "arbitrary")),
        name="rglru",
    )(z, z, z, cw, conv_b.reshape(1, ch), wg, ba.reshape(1, ch), bx.reshape(1, ch), lam.reshape(1, ch))


def _moe_kernel(be_ref, tok_ref, dst_ref, h_hbm, w1_ref, w3_ref, w2_ref, o_hbm, x0, x1, o0, o1, gsem, ssem):
    i = pl.program_id(0)
    last = pl.num_programs(0) - 1
    xs, outs = (x0, x1), (o0, o1)

    def gather_row(j, r, s):
        return pltpu.make_async_copy(h_hbm.at[pl.ds(tok_ref[j, r], 1)], xs[s].at[pl.ds(r, 1)], gsem.at[s])

    def scatter_row(j, r, s):
        return pltpu.make_async_copy(outs[s].at[pl.ds(r, 1)], o_hbm.at[pl.ds(dst_ref[j, r], 1)], ssem.at[s])

    def gather_block(s):
        return pltpu.make_async_copy(h_hbm.at[pl.ds(0, MOE_BLOCK)], xs[s], gsem.at[s])

    def scatter_block(s):
        return pltpu.make_async_copy(outs[s], o_hbm.at[pl.ds(0, MOE_BLOCK)], ssem.at[s])

    @pl.when(i == 0)
    def _():
        o1[...] = jnp.zeros_like(o1)
        for r in range(MOE_BLOCK):
            gather_row(0, r, 0).start()

    def step(s):
        t = 1 - s
        gather_block(s).wait()

        @pl.when(i >= 1)
        def _():
            scatter_block(s).wait()

        nxt = jnp.minimum(i + 1, last)
        prv = jnp.maximum(i - 1, 0)
        f = w1_ref.shape[-1]
        n_groups = f // MOE_FCHUNK
        per_group = MOE_BLOCK // n_groups
        for c in range(n_groups):
            @pl.when(i >= 0)
            def _():
                for r in range(c * per_group, (c + 1) * per_group):
                    gather_row(nxt, r, t).start()
                    scatter_row(prv, r, t).start()

            fs = slice(c * MOE_FCHUNK, (c + 1) * MOE_FCHUNK)
            x = xs[s][...].astype(BF16)
            h1 = jnp.dot(x, w1_ref[0, :, fs], preferred_element_type=F32)
            h3 = jnp.dot(x, w3_ref[0, :, fs], preferred_element_type=F32)
            hid = (h1 * _sigmoid(h1) * h3).astype(BF16)
            part = jnp.dot(hid, w2_ref[0, fs, :], preferred_element_type=F32)
            if c == 0:
                outs[s][...] = part
            else:
                outs[s][...] += part

        @pl.when(i == last)
        def _():
            for r in range(MOE_BLOCK):
                scatter_row(i, r, s).start()
            scatter_block(t).wait()
            scatter_block(s).wait()
            gather_block(t).wait()

    for s in range(2):
        pl.when(i % 2 == s)(functools.partial(step, s))


def _moe_experts(h, block_e, row_tok, row_dst, w1, w3, w2):
    n, d = h.shape
    f = w1.shape[-1]
    n_blocks = row_dst.shape[0]
    rows = n_blocks * MOE_BLOCK
    grid_spec = pltpu.PrefetchScalarGridSpec(
        num_scalar_prefetch=3,
        grid=(n_blocks,),
        in_specs=[pl.BlockSpec(memory_space=pl.ANY),
                  pl.BlockSpec((1, d, f), lambda i, be, tok, dst: (be[i], 0, 0)),
                  pl.BlockSpec((1, d, f), lambda i, be, tok, dst: (be[i], 0, 0)),
                  pl.BlockSpec((1, f, d), lambda i, be, tok, dst: (be[i], 0, 0))],
        out_specs=pl.BlockSpec(memory_space=pl.ANY),
        scratch_shapes=[pltpu.VMEM((MOE_BLOCK, d), F32)] * 4
                       + [pltpu.SemaphoreType.DMA((2,)), pltpu.SemaphoreType.DMA((2,))],
    )
    return pl.pallas_call(
        _moe_kernel,
        grid_spec=grid_spec,
        out_shape=jax.ShapeDtypeStruct((rows, d), F32),
        compiler_params=_cparams(("arbitrary",)),
        name="moe_experts",
    )(block_e, row_tok, row_dst, h, w1, w3, w2)


def _moe_plan(ridx, counts, n):
    nk = 2 * n
    e = ridx[0:2]
    rank = ridx[2:4]
    padded = (counts + MOE_BLOCK - 1) // MOE_BLOCK * MOE_BLOCK
    pend = jnp.cumsum(padded)
    pstart = pend - padded
    experts = jnp.arange(N_EXPERTS, dtype=jnp.int32)[:, None, None]
    dest = rank + jnp.sum(jnp.where(e[None] == experts, pstart[:, None, None], 0), axis=0)
    n_blocks = nk // MOE_BLOCK + N_EXPERTS
    rows = n_blocks * MOE_BLOCK
    result_row = jnp.arange(nk, dtype=jnp.int32)
    placed = jnp.full((rows,), -1, jnp.int32).at[dest.reshape(-1)].set(result_row)
    is_pad = placed < 0
    pad_rank = jnp.cumsum(is_pad.astype(jnp.int32)) - 1
    row_dst = jnp.where(is_pad, nk + pad_rank, placed)
    row_tok = jnp.where(is_pad, 0, jnp.where(placed >= n, placed - n, placed))
    block_start = jnp.arange(n_blocks, dtype=jnp.int32) * MOE_BLOCK
    block_e = jnp.minimum(jnp.sum(block_start[:, None] >= pend[None, :], axis=1), N_EXPERTS - 1).astype(jnp.int32)
    return block_e, row_tok.reshape(n_blocks, MOE_BLOCK), row_dst.reshape(n_blocks, MOE_BLOCK)


def _final_kernel(x_ref, o1_ref, o2_ref, rg_ref, gate_ref, g_ref, out_ref):
    g = rg_ref[...].T
    y = g[:, 0:1] * o1_ref[...] + g[:, 1:2] * o2_ref[...]
    x2 = x_ref[...] + gate_ref[0] * y
    ms = jnp.mean(x2 * x2, axis=-1, keepdims=True)
    out_ref[...] = x2 * lax.rsqrt(ms + EPS) * g_ref[...]


def _final_residual_norm(x, o_all, rgate, gate, g, seq, tm):
    n, d = x.shape
    per_seq = seq // tm
    nt = n // tm
    return pl.pallas_call(
        _final_kernel,
        grid=(nt,),
        in_specs=[pl.BlockSpec((tm, d), lambda i: (i, 0)),
                  pl.BlockSpec((tm, d), lambda i: (i, 0)),
                  pl.BlockSpec((tm, d), lambda i: (i + nt, 0)),
                  pl.BlockSpec((8, tm), lambda i: (0, i)),
                  pl.BlockSpec((1, 1, d), lambda i: (i // per_seq, 0, 0)),
                  pl.BlockSpec((1, d), lambda i: (0, 0))],
        out_specs=pl.BlockSpec((tm, d), lambda i: (i, 0)),
        out_shape=jax.ShapeDtypeStruct((n, d), F32),
        compiler_params=_cparams(("parallel",)),
        name="final_residual_norm",
    )(x, o_all, o_all, rgate, gate, g.reshape(1, d))


def _pad_cols(w, width):
    return jnp.pad(w, ((0, 0), (0, width - w.shape[1])))


def _pad_rows(w, height):
    return jnp.pad(w, ((0, height - w.shape[0]), (0, 0)))


def kernel(x, c, positions, router_w, router_bias, ada_w, ada_b, norm_mix, norm_ffn, moe_w1, moe_w3, moe_w2, ab_w_in, ab_sinks, ab_conv_w, ab_conv_b, ab_conv_ln_g, ab_conv_ln_b, ab_w_out, cd_w_in, cd_shift_mu, cd_w0, cd_w2, cd_a0, cd_a2, cd_g2, cd_k_k, cd_k_a, cd_r_k, cd_ln_x_g, cd_ln_x_b, cd_lru_conv_w, cd_lru_conv_b, cd_lru_wa, cd_lru_ba, cd_lru_wx, cd_lru_bx, cd_lru_lambda, cd_w_out, final_norm):
    batch, seq, d = x.shape
    n = batch * seq
    depth = ada_w.shape[0]
    xf = x.reshape(n, d)
    mod = _ada_mod(c, ada_w, ada_b).reshape(depth, batch, 6, 1, d)
    cos_t, sin_t = _rope_tables(positions)
    att_q = ATT_HEADS * HEAD_DIM
    att_kv = ATT_KV_HEADS * HEAD_DIM

    def in_proj(xf, layer, shift_m, scale_m, w_perm, pending):
        z = _in_proj(xf, norm_mix[layer], shift_m, scale_m, w_perm, seq, PROJ_ROWS, PROJ_COLS, pending)
        return (z, xf) if pending is None else z

    pending = None
    for layer in range(depth):
        j = layer // 2
        shift_m, scale_m, gate_m, shift_f, scale_f, gate_f = (mod[layer, :, q] for q in range(6))
        if layer % 2 == 0:
            w_in = ab_w_in[j]
            w_perm = jnp.concatenate([w_in[:, :att_q], w_in[:, att_q + 2 * att_kv:],
                                      w_in[:, att_q:att_q + 2 * att_kv]], axis=1).astype(BF16)
            z, xf = in_proj(xf, layer, shift_m, scale_m, w_perm, pending)
            conv_ch = ab_conv_w.shape[-1]
            kv0 = (att_q + 2 * conv_ch) // att_kv
            a1 = _attention(z, cos_t, sin_t, ab_sinks[j], batch, seq, kv0, kv0 + 1)
            a2 = _conformer_conv(z, ab_conv_w[j], ab_conv_b[j], ab_conv_ln_g[j], ab_conv_ln_b[j],
                                 batch, seq, att_q // conv_ch, att_q // conv_ch + 1, CONV_ROWS)
            w_out = ab_w_out[j]
        else:
            w_in = cd_w_in[j]
            dim = cd_w0.shape[-1]
            dl = cd_w2.shape[1]
            al = cd_a2.shape[1]
            gl = cd_g2.shape[1]
            s0 = 3 * dim
            seg_zw = w_in[:, s0:s0 + dl]
            seg_za = w_in[:, s0 + dl:s0 + dl + al]
            seg_zg = w_in[:, s0 + dl + al:s0 + dl + al + gl]
            s1 = s0 + dl + al + gl
            w_perm = jnp.concatenate([w_in[:, :s0], w_in[:, s1:], _pad_cols(seg_zw, LANES),
                                      _pad_cols(seg_za, LANES), seg_zg], axis=1).astype(BF16)
            mu = cd_shift_mu[j]
            mu_lo = jnp.concatenate([jnp.pad(mu[s0:s0 + dl], (0, LANES - dl)),
                                     jnp.pad(mu[s0 + dl:s0 + dl + al], (0, LANES - al)),
                                     mu[s0 + dl + al:s1]]).reshape(1, -1)
            row = lambda a: a.reshape(1, -1)
            p = dict(mu_r=row(mu[:dim]), mu_k=row(mu[dim:2 * dim]), mu_v=row(mu[2 * dim:s0]), mu_lo=mu_lo,
                     w0=row(cd_w0[j]), w2=_pad_rows(cd_w2[j], LANES).astype(BF16),
                     a0=row(cd_a0[j]), a2=_pad_rows(cd_a2[j], LANES).astype(BF16), g2=cd_g2[j].astype(BF16),
                     k_k=row(cd_k_k[j]), k_a=row(cd_k_a[j]), r_k=row(cd_r_k[j]),
                     ln_g=row(cd_ln_x_g[j]), ln_b=row(cd_ln_x_b[j]))
            z, xf = in_proj(xf, layer, shift_m, scale_m, w_perm, pending)
            lo_col = (5 * dim) // (4 * LANES)
            a1 = _rwkv(z, p, batch, seq, lo_col)
            a2 = _rglru(z, cd_lru_conv_w[j], cd_lru_conv_b[j], cd_lru_wa[j], cd_lru_ba[j],
                        cd_lru_wx[j], cd_lru_bx[j], cd_lru_lambda[j], batch, seq, 3, 4, LRU_ROWS)
            w_out = cd_w_out[j]
        x1, h, ridx, rgate, counts = _out_proj(a1, a2, w_out, xf, gate_m, norm_ffn[layer], shift_f, scale_f,
                                               router_w, router_bias, seq, PROJ_ROWS)
        block_e, row_tok, row_dst = _moe_plan(ridx, counts[:, 0], n)
        o_all = _moe_experts(h, block_e, row_tok, row_dst, moe_w1[layer].astype(BF16),
                             moe_w3[layer].astype(BF16), moe_w2[layer].astype(BF16))
        xf, pending = x1, (o_all, rgate, gate_f)
    out = _final_residual_norm(xf, *pending, final_norm, seq, FINAL_ROWS)
    return out.reshape(batch, seq, d)
```

```python
import functools

import jax
import jax.numpy as jnp
from jax import lax
from jax.experimental import pallas as pl
from jax.experimental.pallas import tpu as pltpu

F32 = jnp.float32
BF16 = jnp.bfloat16

EPS = 1e-6
LN_EPS = 1e-5
RWKV_LN_EPS = 64e-5
HEAD_DIM = 64
LANES = 128
MXU_TILE = 256
ATT_BLOCK = 128
WINDOW = 128
ATT_HEADS = 16
ATT_KV_HEADS = 4
CONV_WIDTH = 31
CONV_HALO = 32
LRU_CONV_WIDTH = 4
LRU_C = 8.0
N_EXPERTS = 16
N_GROUPS = 4
EXPERTS_PER_GROUP = 4
MOE_BLOCK = 256
MOE_FCHUNK = 256
RWKV_CHUNK = 64
RWKV_ROWS = 256
ROPE_THETA = 10000.0
VMEM_LIMIT = 56 * 1024 * 1024
PROJ_ROWS = 256
PROJ_COLS = 512
CONV_ROWS = 256
LRU_ROWS = 256
FINAL_ROWS = 512


def _cparams(sem):
    return pltpu.CompilerParams(dimension_semantics=sem, vmem_limit_bytes=VMEM_LIMIT)


def _bdot(a, b):
    return jnp.dot(a.astype(BF16), b.astype(BF16), preferred_element_type=F32)


def _split2(x):
    hi = x.astype(BF16)
    lo = (x - hi.astype(F32)).astype(BF16)
    return hi, lo


def _dot_left_exact(m_bf16, x):
    hi, lo = _split2(x)
    d = functools.partial(jnp.dot, preferred_element_type=F32)
    return d(m_bf16, hi) + d(m_bf16, lo)


def _dot_right_exact(x, m_bf16):
    hi, lo = _split2(x)
    d = functools.partial(jnp.dot, preferred_element_type=F32)
    return d(hi, m_bf16) + d(lo, m_bf16)


def _dot_3pass(a, b):
    a_hi = a.astype(BF16)
    a_lo = (a - a_hi.astype(F32)).astype(BF16)
    b_hi = b.astype(BF16)
    b_lo = (b - b_hi.astype(F32)).astype(BF16)
    d = functools.partial(jnp.dot, preferred_element_type=F32)
    return d(a_hi, b_hi) + d(a_hi, b_lo) + d(a_lo, b_hi)


def _sigmoid(x):
    return 1.0 / (1.0 + jnp.exp(-x))


def _softplus(x):
    return jnp.maximum(x, 0.0) + jnp.log(1.0 + jnp.exp(-jnp.abs(x)))


def _rms_mod(x, g, shift, scale):
    ms = jnp.mean(x * x, axis=-1, keepdims=True)
    return (x * lax.rsqrt(ms + EPS) * g) * (1.0 + scale) + shift


def _ada_kernel(c_ref, w_ref, b_ref, o_ref):
    c = c_ref[...]
    o_ref[0] = _dot_3pass(c * _sigmoid(c), w_ref[0]) + b_ref[0]


def _ada_mod(c, ada_w, ada_b):
    depth, d, n6 = ada_w.shape
    b = c.shape[0]
    bp = 8
    tn = 768
    c_pad = jnp.zeros((bp, d), F32).at[:b].set(c)
    out = pl.pallas_call(
        _ada_kernel,
        grid=(depth, n6 // tn),
        in_specs=[pl.BlockSpec((bp, d), lambda l, j: (0, 0)),
                  pl.BlockSpec((1, d, tn), lambda l, j: (l, 0, j)),
                  pl.BlockSpec((1, 1, tn), lambda l, j: (l, 0, j))],
        out_specs=pl.BlockSpec((1, bp, tn), lambda l, j: (l, 0, j)),
        out_shape=jax.ShapeDtypeStruct((depth, bp, n6), F32),
        compiler_params=_cparams(("parallel", "parallel")),
        name="ada_mod",
    )(c_pad, ada_w, ada_b.reshape(depth, 1, n6))
    return out[:, :b]


def _rope_kernel(pos_ref, f_ref, cos_ref, sin_ref):
    ang = pos_ref[...].astype(F32) * f_ref[...]
    lane = lax.broadcasted_iota(jnp.int32, ang.shape, 1)
    s = jnp.sin(ang)
    cos_ref[...] = jnp.cos(ang)
    sin_ref[...] = jnp.where((lane & 32) == 0, -s, s)


def _rope_tables(positions):
    n = positions.size
    half = HEAD_DIM // 2
    inv_freq = ROPE_THETA ** (-jnp.arange(half, dtype=F32) / half)
    f_row = jnp.tile(inv_freq, LANES // half).reshape(1, LANES)
    tm = 512
    return pl.pallas_call(
        _rope_kernel,
        grid=(n // tm,),
        in_specs=[pl.BlockSpec((tm, 1), lambda i: (i, 0)),
                  pl.BlockSpec((1, LANES), lambda i: (0, 0))],
        out_specs=[pl.BlockSpec((tm, LANES), lambda i: (i, 0))] * 2,
        out_shape=[jax.ShapeDtypeStruct((n, LANES), F32)] * 2,
        compiler_params=_cparams(("parallel",)),
        name="rope_tables",
    )(positions.reshape(n, 1), f_row)


def _in_proj_kernel(*refs, tn, combine):
    if combine:
        x1_ref, o1_ref, o2_ref, rg_ref, gate_ref, g_ref, sh_ref, sc_ref, w_ref, o_ref, x_out_ref = refs
        gates = rg_ref[...].T
        x = x1_ref[...] + gate_ref[0] * (gates[:, 0:1] * o1_ref[...] + gates[:, 1:2] * o2_ref[...])
        x_out_ref[...] = x
    else:
        x_ref, g_ref, sh_ref, sc_ref, w_ref, o_ref = refs
        x = x_ref[...]
    h = _rms_mod(x, g_ref[...], sh_ref[0], sc_ref[0]).astype(BF16)
    for c in range(o_ref.shape[1] // tn):
        o_ref[:, c * tn:(c + 1) * tn] = jnp.dot(h, w_ref[:, c * tn:(c + 1) * tn], preferred_element_type=F32)


def _in_proj(x, g, shift, scale, w_bf16, seq, tm, tn, combine=None):
    n, d = x.shape
    nout = w_bf16.shape[1]
    per_seq = seq // tm
    nt = n // tm
    row_spec = pl.BlockSpec((tm, d), lambda i: (i, 0))
    mod_spec = pl.BlockSpec((1, 1, d), lambda i: (i // per_seq, 0, 0))
    tail_specs = [pl.BlockSpec((1, d), lambda i: (0, 0)), mod_spec, mod_spec,
                  pl.BlockSpec((d, nout), lambda i: (0, 0), pipeline_mode=pl.Buffered(1))]
    tail_args = (g.reshape(1, d), shift, scale, w_bf16)
    z_spec = pl.BlockSpec((tm, nout), lambda i: (i, 0))
    z_shape = jax.ShapeDtypeStruct((n, nout), F32)
    if combine is None:
        in_specs, args, out_specs, out_shape = [row_spec], (x,), z_spec, z_shape
    else:
        o_all, rgate, gate_f = combine
        in_specs = [row_spec, row_spec, pl.BlockSpec((tm, d), lambda i: (i + nt, 0)),
                    pl.BlockSpec((8, tm), lambda i: (0, i)), mod_spec]
        args = (x, o_all, o_all, rgate, gate_f)
        out_specs = [z_spec, row_spec]
        out_shape = [z_shape, jax.ShapeDtypeStruct((n, d), F32)]
    return pl.pallas_call(
        functools.partial(_in_proj_kernel, tn=tn, combine=combine is not None),
        grid=(nt,),
        in_specs=in_specs + tail_specs,
        out_specs=out_specs,
        out_shape=out_shape,
        compiler_params=_cparams(("parallel",)),
        name="in_proj",
    )(*args, *tail_args)


def _route(logits_t, bias_col):
    m = jnp.max(logits_t, axis=0, keepdims=True)
    e = jnp.exp(logits_t - m)
    probs = e / jnp.sum(e, axis=0, keepdims=True)
    sel = probs + bias_col
    s = [sel[i:i + 1, :] for i in range(N_EXPERTS)]
    pr = [probs[i:i + 1, :] for i in range(N_EXPERTS)]

    def top2_sum(v):
        best = v[0] + v[1]
        for i in range(len(v)):
            for j in range(i + 1, len(v)):
                if (i, j) != (0, 1):
                    best = jnp.maximum(best, v[i] + v[j])
        return best

    gs = [top2_sum(s[4 * g:4 * g + 4]) for g in range(N_GROUPS)]
    best = gs[0]
    gi = jnp.zeros_like(best, dtype=jnp.int32)
    for g in range(1, N_GROUPS):
        upd = gs[g] > best
        gi = jnp.where(upd, g, gi)
        best = jnp.where(upd, gs[g], best)

    def pick_group(rows):
        out = []
        for l in range(EXPERTS_PER_GROUP):
            v = rows[l]
            for g in range(1, N_GROUPS):
                v = jnp.where(gi == g, rows[4 * g + l], v)
            out.append(v)
        return out

    ig = pick_group(s)
    pg = pick_group(pr)

    def argmax4(v):
        bv = v[0]
        bi = jnp.zeros_like(gi)
        for l in range(1, EXPERTS_PER_GROUP):
            upd = v[l] > bv
            bi = jnp.where(upd, l, bi)
            bv = jnp.where(upd, v[l], bv)
        return bi

    i1 = argmax4(ig)
    i2 = argmax4([jnp.where(i1 == l, -jnp.inf, ig[l]) for l in range(EXPERTS_PER_GROUP)])

    def pick(v, idx):
        o = v[0]
        for l in range(1, EXPERTS_PER_GROUP):
            o = jnp.where(idx == l, v[l], o)
        return o

    p1 = pick(pg, i1)
    p2 = pick(pg, i2)
    tot = p1 + p2
    return gi * EXPERTS_PER_GROUP + i1, gi * EXPERTS_PER_GROUP + i2, p1 / tot, p2 / tot


def _out_proj_kernel(a1_ref, a2_ref, w1_ref, w2_ref, x_ref, gate_ref, g_ref, sh_ref, sc_ref,
                     rw_ref, rb_ref, upper_ref, xo_ref, h_ref, ridx_ref, rgate_ref, cnt_ref, carry):
    @pl.when(pl.program_id(0) == 0)
    def _():
        carry[...] = jnp.zeros_like(carry)

    y = (jnp.dot(a1_ref[...], w1_ref[...], preferred_element_type=F32)
         + jnp.dot(a2_ref[...], w2_ref[...], preferred_element_type=F32))
    x1 = x_ref[...] + gate_ref[0] * y
    xo_ref[...] = x1
    h = _rms_mod(x1, g_ref[...], sh_ref[0], sc_ref[0])
    h_ref[...] = h
    logits = _dot_3pass(h, rw_ref[...])
    logits_t = logits.T[:N_EXPERTS, :]
    e1, e2, g1, g2 = _route(logits_t, rb_ref[...])
    erow = lax.broadcasted_iota(jnp.int32, logits_t.shape, 0)
    hit1 = erow == e1
    hit2 = erow == e2
    onehot = jnp.where(hit1 | hit2, 1.0, 0.0)
    before = jnp.dot(onehot.astype(BF16), upper_ref[...], preferred_element_type=F32) + carry[:, 0:1]
    rank1 = jnp.sum(jnp.where(hit1, before, 0.0), axis=0, keepdims=True).astype(jnp.int32)
    rank2 = jnp.sum(jnp.where(hit2, before, 0.0), axis=0, keepdims=True).astype(jnp.int32)
    total = carry[...] + jnp.sum(onehot, axis=1, keepdims=True)
    carry[...] = total
    cnt_ref[...] = total.astype(jnp.int32)
    row = lax.broadcasted_iota(jnp.int32, ridx_ref.shape, 0)
    ridx_ref[...] = jnp.where(row == 0, e1, jnp.where(row == 1, e2,
                              jnp.where(row == 2, rank1, jnp.where(row == 3, rank2, 0))))
    rgate_ref[...] = jnp.where(row == 0, g1, jnp.where(row == 1, g2, 0.0))


def _out_proj(a1, a2, w_out, x, gate, g, shift, scale, router_w, router_bias, seq, tm):
    n, d = x.shape
    ka = a1.shape[1]
    per_seq = seq // tm
    w1 = w_out[:ka].astype(BF16)
    w2 = w_out[ka:].astype(BF16)
    rw = jnp.zeros((d, LANES), F32).at[:, :N_EXPERTS].set(router_w)
    rb = router_bias.astype(F32).reshape(N_EXPERTS, 1)
    upper = (jnp.arange(tm)[:, None] < jnp.arange(tm)[None, :]).astype(BF16)
    mod_spec = pl.BlockSpec((1, 1, d), lambda i: (i // per_seq, 0, 0))
    return pl.pallas_call(
        _out_proj_kernel,
        grid=(n // tm,),
        in_specs=[pl.BlockSpec((tm, ka), lambda i: (i, 0)),
                  pl.BlockSpec((tm, ka), lambda i: (i, 0)),
                  pl.BlockSpec((ka, d), lambda i: (0, 0), pipeline_mode=pl.Buffered(1)),
                  pl.BlockSpec((ka, d), lambda i: (0, 0), pipeline_mode=pl.Buffered(1)),
                  pl.BlockSpec((tm, d), lambda i: (i, 0)),
                  mod_spec,
                  pl.BlockSpec((1, d), lambda i: (0, 0)),
                  mod_spec, mod_spec,
                  pl.BlockSpec((d, LANES), lambda i: (0, 0)),
                  pl.BlockSpec((N_EXPERTS, 1), lambda i: (0, 0)),
                  pl.BlockSpec((tm, tm), lambda i: (0, 0))],
        out_specs=[pl.BlockSpec((tm, d), lambda i: (i, 0)),
                   pl.BlockSpec((tm, d), lambda i: (i, 0)),
                   pl.BlockSpec((8, tm), lambda i: (0, i)),
                   pl.BlockSpec((8, tm), lambda i: (0, i)),
                   pl.BlockSpec((N_EXPERTS, LANES), lambda i: (0, 0))],
        out_shape=[jax.ShapeDtypeStruct((n, d), F32),
                   jax.ShapeDtypeStruct((n, d), F32),
                   jax.ShapeDtypeStruct((8, n), jnp.int32),
                   jax.ShapeDtypeStruct((8, n), F32),
                   jax.ShapeDtypeStruct((N_EXPERTS, LANES), jnp.int32)],
        scratch_shapes=[pltpu.VMEM((N_EXPERTS, LANES), F32)],
        compiler_params=_cparams(("arbitrary",)),
        name="out_proj_route",
    )(a1, a2, w1, w2, x, gate, g.reshape(1, d), shift, scale, rw, rb, upper)


def _rope_apply(x, cos, sin_signed):
    outs = []
    lane = lax.broadcasted_iota(jnp.int32, cos.shape, 1)
    first_half = (lane & 32) == 0
    for c in range(x.shape[1] // LANES):
        xc = x[:, c * LANES:(c + 1) * LANES]
        partner = jnp.where(first_half, pltpu.roll(xc, LANES - 32, axis=1), pltpu.roll(xc, 32, axis=1))
        outs.append(xc * cos + partner * sin_signed)
    return outs


def _attn_kernel(sink_ref, q_ref, kp_ref, kc_ref, vp_ref, vc_ref, cq_ref, sq_ref, cp_ref, sp_ref, o_ref):
    has_prev = pl.program_id(1) > 0
    cq, sq = cq_ref[...], sq_ref[...]
    q_cols = _rope_apply(q_ref[...] * (HEAD_DIM ** -0.5), cq, sq)
    k_cols = [jnp.concatenate([kp, kc], axis=0) for kp, kc in
              zip(_rope_apply(kp_ref[...], cp_ref[...], sp_ref[...]), _rope_apply(kc_ref[...], cq, sq))]
    v_all = jnp.concatenate([vp_ref[...], vc_ref[...]], axis=0).astype(BF16)
    qi = lax.broadcasted_iota(jnp.int32, (ATT_BLOCK, 2 * ATT_BLOCK), 0)
    kj = lax.broadcasted_iota(jnp.int32, (ATT_BLOCK, 2 * ATT_BLOCK), 1)
    valid = ((kj > qi) & (kj < ATT_BLOCK) & has_prev) | ((kj >= ATT_BLOCK) & (kj - ATT_BLOCK <= qi))
    group = ATT_HEADS // ATT_KV_HEADS
    heads = range(ATT_HEADS)
    scores = []
    for hq in heads:
        hk = hq // group
        qh = q_cols[hq // 2][:, (hq % 2) * HEAD_DIM:(hq % 2 + 1) * HEAD_DIM].astype(BF16)
        kh = k_cols[hk // 2][:, (hk % 2) * HEAD_DIM:(hk % 2 + 1) * HEAD_DIM].astype(BF16)
        s = lax.dot_general(qh, kh, (((1,), (1,)), ((), ())), preferred_element_type=F32)
        scores.append(jnp.where(valid, s, -1e30))
    tops = [jnp.maximum(jnp.max(scores[hq], axis=-1, keepdims=True), sink_ref[hq]) for hq in heads]
    probs = [jnp.exp(scores[hq] - tops[hq]) for hq in heads]
    dens = [jnp.sum(probs[hq], axis=-1, keepdims=True) + jnp.exp(sink_ref[hq] - tops[hq]) for hq in heads]
    outs = [jnp.dot(probs[hq].astype(BF16), v_all[:, (hq // group) * HEAD_DIM:(hq // group + 1) * HEAD_DIM],
                    preferred_element_type=F32) / dens[hq] for hq in heads]
    out_cols = [jnp.concatenate([outs[2 * pair], outs[2 * pair + 1]], axis=1) for pair in range(ATT_HEADS // 2)]
    o_ref[...] = jnp.concatenate(out_cols, axis=1).astype(o_ref.dtype)


def _attention(z, cos_t, sin_t, sinks, batch, seq, k_col, v_col):
    n = z.shape[0]
    nb = seq // ATT_BLOCK
    att_q = ATT_HEADS * HEAD_DIM
    att_kv = ATT_KV_HEADS * HEAD_DIM
    cur = lambda b, i: b * nb + i
    prev = lambda b, i: b * nb + jnp.maximum(i - 1, 0)
    return pl.pallas_call(
        _attn_kernel,
        grid=(batch, nb),
        in_specs=[pl.BlockSpec(memory_space=pltpu.SMEM),
                  pl.BlockSpec((ATT_BLOCK, att_q), lambda b, i: (cur(b, i), 0)),
                  pl.BlockSpec((ATT_BLOCK, att_kv), lambda b, i: (prev(b, i), k_col)),
                  pl.BlockSpec((ATT_BLOCK, att_kv), lambda b, i: (cur(b, i), k_col)),
                  pl.BlockSpec((ATT_BLOCK, att_kv), lambda b, i: (prev(b, i), v_col)),
                  pl.BlockSpec((ATT_BLOCK, att_kv), lambda b, i: (cur(b, i), v_col)),
                  pl.BlockSpec((ATT_BLOCK, LANES), lambda b, i: (cur(b, i), 0)),
                  pl.BlockSpec((ATT_BLOCK, LANES), lambda b, i: (cur(b, i), 0)),
                  pl.BlockSpec((ATT_BLOCK, LANES), lambda b, i: (prev(b, i), 0)),
                  pl.BlockSpec((ATT_BLOCK, LANES), lambda b, i: (prev(b, i), 0))],
        out_specs=pl.BlockSpec((ATT_BLOCK, att_q), lambda b, i: (cur(b, i), 0)),
        out_shape=jax.ShapeDtypeStruct((n, att_q), BF16),
        compiler_params=_cparams(("parallel", "parallel")),
        name="swa_attention",
    )(sinks.astype(F32), z, z, z, z, z, cos_t, sin_t, cos_t, sin_t)


def _conv_kernel(val_ref, gate_ref, hval_ref, hgate_ref, w_ref, b_ref, g_ref, beta_ref, o_ref, ybuf, acc, shifted):
    t = val_ref.shape[0]
    ch = val_ref.shape[1]
    first = pl.program_id(1) == 0
    yh = hval_ref[...] * _sigmoid(hgate_ref[...])
    ybuf[0:CONV_HALO, :] = jnp.where(first, 0.0, yh)
    ybuf[CONV_HALO:, :] = val_ref[...] * _sigmoid(gate_ref[...])
    off = CONV_HALO - (CONV_WIDTH - 1)
    span = t + CONV_HALO - 8
    for ph in range(1, 8):
        shifted[ph - 1] = ybuf[ph:ph + span, :]
    rt = min(t, 128)
    for cb in range(ch // LANES):
        cs = slice(cb * LANES, (cb + 1) * LANES)
        for r0 in range(0, t, rt):
            a = jnp.broadcast_to(b_ref[:, cs], (rt, LANES))
            for j in range(CONV_WIDTH):
                ph, base = (off + j) % 8, (off + j) // 8 * 8
                src = ybuf if ph == 0 else shifted.at[ph - 1]
                a = a + w_ref[j:j + 1, cs] * src[r0 + base:r0 + base + rt, cs]
            acc[r0:r0 + rt, cs] = a
    y = acc[...]
    mu = jnp.mean(y, axis=-1, keepdims=True)
    yc = y - mu
    var = jnp.mean(yc * yc, axis=-1, keepdims=True)
    ln = yc * lax.rsqrt(var + LN_EPS) * g_ref[...] + beta_ref[...]
    o_ref[...] = (ln * _sigmoid(ln)).astype(o_ref.dtype)


def _conformer_conv(z, conv_w, conv_b, ln_g, ln_b, batch, seq, val_col, gate_col, t):
    n = z.shape[0]
    ch = conv_w.shape[-1]
    nt = seq // t
    hpt = t // CONV_HALO
    w = jnp.zeros((CONV_HALO, ch), F32).at[:CONV_WIDTH].set(conv_w.reshape(CONV_WIDTH, ch))
    cur = lambda b, i: b * nt + i
    halo = lambda b, i: jnp.maximum((b * nt + i) * hpt - 1, 0)
    vec = pl.BlockSpec((1, ch), lambda b, i: (0, 0))
    return pl.pallas_call(
        _conv_kernel,
        grid=(batch, nt),
        in_specs=[pl.BlockSpec((t, ch), lambda b, i: (cur(b, i), val_col)),
                  pl.BlockSpec((t, ch), lambda b, i: (cur(b, i), gate_col)),
                  pl.BlockSpec((CONV_HALO, ch), lambda b, i: (halo(b, i), val_col)),
                  pl.BlockSpec((CONV_HALO, ch), lambda b, i: (halo(b, i), gate_col)),
                  pl.BlockSpec((CONV_HALO, ch), lambda b, i: (0, 0)),
                  vec, vec, vec],
        out_specs=pl.BlockSpec((t, ch), lambda b, i: (cur(b, i), 0)),
        out_shape=jax.ShapeDtypeStruct((n, ch), BF16),
        scratch_shapes=[pltpu.VMEM((t + CONV_HALO, ch), F32), pltpu.VMEM((t, ch), F32),
                        pltpu.VMEM((7, t + CONV_HALO - 8, ch), F32)],
        compiler_params=_cparams(("parallel", "parallel")),
        name="conformer_conv",
    )(z, z, z, z, w, conv_b.reshape(1, ch), ln_g.reshape(1, ch), ln_b.reshape(1, ch))


def _shift_lerp(z, halo, mu, first):
    prev = pltpu.roll(z, 1, axis=0)
    row = lax.broadcasted_iota(jnp.int32, z.shape, 0)
    hrow = jnp.where(first, 0.0, halo[7:8, :])
    prev = jnp.where(row == 0, hrow, prev)
    return z + mu * (prev - z)


def _rwkv_kernel(r_ref, k_ref, v_ref, lo_ref, rh_ref, kh_ref, vh_ref, loh_ref,
                 mur_ref, muk_ref, muv_ref, mulo_ref, w0_ref, w2_ref, a0_ref, a2_ref, g2_ref,
                 kk_ref, ka_ref, rk_ref, lng_ref, lnb_ref, seg_ref, tril_ref,
                 o_ref, hs_ref):
    first = pl.program_id(1) == 0

    @pl.when(first)
    def _():
        hs_ref[...] = jnp.zeros_like(hs_ref)

    r = _shift_lerp(r_ref[...], rh_ref[...], mur_ref[...], first)
    k = _shift_lerp(k_ref[...], kh_ref[...], muk_ref[...], first)
    v = _shift_lerp(v_ref[...], vh_ref[...], muv_ref[...], first)
    lo = _shift_lerp(lo_ref[...], loh_ref[...], mulo_ref[...], first)
    zw, za, zg = lo[:, 0:LANES], lo[:, LANES:2 * LANES], lo[:, 2 * LANES:]

    seg = seg_ref[...]
    seg_w = seg.shape[0]

    def head_sum(x):
        return jnp.concatenate([_dot_right_exact(x[:, c:c + seg_w], seg) for c in range(0, x.shape[1], seg_w)],
                               axis=1)

    w_raw = w0_ref[...] + _bdot(jnp.tanh(zw), w2_ref[...])
    log_decay = -jnp.exp(-_softplus(-w_raw) - 0.5)
    a_gate = _sigmoid(a0_ref[...] + _bdot(za, a2_ref[...]))
    g = _bdot(_sigmoid(zg), g2_ref[...])
    kk = k * kk_ref[...]
    kk = kk * lax.rsqrt(jnp.maximum(head_sum(kk * kk), 1e-24))
    k_mod = k * (1.0 + (a_gate - 1.0) * ka_ref[...])
    b_vec = kk * a_gate

    n_chunks = r.shape[0] // RWKV_CHUNK
    rows = [slice(c * RWKV_CHUNK, (c + 1) * RWKV_CHUNK) for c in range(n_chunks)]
    cum = _dot_left_exact(tril_ref[...], log_decay)
    cum_last = [cum[rs.stop - 1:rs.stop, :] for rs in rows]
    w_last = [jnp.exp(cl) for cl in cum_last]
    inv_w = jnp.exp(-cum)
    to_end = jnp.concatenate([jnp.exp(cum_last[c] - cum[rows[c], :]) for c in range(n_chunks)], axis=0)
    a_til = -kk * jnp.exp(cum - log_decay)
    r_til = r * jnp.exp(cum)
    b_til = b_vec * inv_w
    k_til = k_mod * inv_w
    b_hat = b_vec * to_end
    k_hat = k_mod * to_end

    lane = lax.broadcasted_iota(jnp.int32, (RWKV_CHUNK, LANES), 1)
    m0 = lane < HEAD_DIM
    ri = lax.broadcasted_iota(jnp.int32, (LANES, LANES), 0)
    ci = lax.broadcasted_iota(jnp.int32, (LANES, LANES), 1)
    same = (ri < HEAD_DIM) == (ci < HEAD_DIM)
    mask_sl = same & (ci < ri)
    mask_li = same & (ci <= ri)
    eye = ri == ci

    def stack(x):
        return jnp.concatenate([jnp.where(m0, x, 0.0), jnp.where(m0, 0.0, x)], axis=0)

    nt_dims = (((1,), (1,)), ((), ()))
    dot = functools.partial(jnp.dot, preferred_element_type=F32)
    n_pairs = r.shape[1] // LANES
    cols = [slice(p * LANES, (p + 1) * LANES) for p in range(n_pairs)]
    units = [(rows[c], cols[p]) for c in range(n_chunks) for p in range(n_pairs)]
    ids = range(len(units))
    xa_f = [stack(a_til[u]) for u in units]
    xr_f = [stack(r_til[u]) for u in units]
    vs = [stack(v[u]).astype(BF16) for u in units]
    m = [lax.dot_general(jnp.concatenate([xa_f[i], xr_f[i]], axis=0).astype(BF16),
                         jnp.concatenate([b_til[units[i]]] * 2 + [k_til[units[i]]] * 2, axis=0).astype(BF16),
                         nt_dims, preferred_element_type=F32) for i in ids]
    power = [jnp.where(mask_sl, m[i][:LANES, :LANES], 0.0).astype(BF16) for i in ids]
    a_ak = [jnp.where(mask_sl, m[i][:LANES, LANES:], 0.0).astype(BF16) for i in ids]
    z = [jnp.concatenate([xa_f[i], dot(a_ak[i], vs[i])], axis=1) for i in ids]
    for level in range(6):
        z = [z[i] + dot(power[i], z[i].astype(BF16)) for i in ids]
        if level < 5:
            power = [dot(power[i], power[i]).astype(BF16) for i in ids]
    zb = [z[i].astype(BF16) for i in ids]
    m_rb = [jnp.where(mask_li, m[i][LANES:, :LANES], 0.0).astype(BF16) for i in ids]
    m_rk = [jnp.where(mask_li, m[i][LANES:, LANES:], 0.0).astype(BF16) for i in ids]
    q = [jnp.concatenate([xr_f[i], dot(m_rk[i], vs[i])], axis=1) + dot(m_rb[i], zb[i]) for i in ids]
    ge = [dot(stack(b_hat[units[i]]).T.astype(BF16), zb[i]) for i in ids]
    e_mat = [ge[i][:, LANES:] + dot(stack(k_hat[units[i]]).T.astype(BF16), vs[i]) for i in ids]
    qg = [jnp.concatenate([q[i][:, :LANES],
                           ge[i][:, :LANES] + jnp.where(eye, w_last[i // n_pairs][:, cols[i % n_pairs]], 0.0)],
                          axis=0).astype(BF16) for i in ids]
    state = [hs_ref[p] for p in range(n_pairs)]
    y_rows = []
    for c in range(n_chunks):
        y_cols = []
        for p in range(n_pairs):
            i = c * n_pairs + p
            yh = dot(qg[i], state[p].astype(BF16))
            state[p] = yh[LANES:] + e_mat[i]
            y2 = yh[:LANES] + q[i][:, LANES:]
            y_cols.append(y2[:RWKV_CHUNK] + y2[RWKV_CHUNK:])
        y_rows.append(jnp.concatenate(y_cols, axis=1))
    for p in range(n_pairs):
        hs_ref[p] = state[p]
    y = jnp.concatenate(y_rows, axis=0)

    inv_hd = 1.0 / HEAD_DIM
    mu = head_sum(y) * inv_hd
    yc = y - mu
    var = head_sum(yc * yc) * inv_hd
    yn = yc * lax.rsqrt(var + RWKV_LN_EPS) * lng_ref[...] + lnb_ref[...]
    bonus = head_sum(r * k_mod * rk_ref[...]) * v
    o_ref[...] = ((yn + bonus) * g).astype(o_ref.dtype)


def _rwkv(z, p, batch, seq, lo_col):
    n = z.shape[0]
    dim = p["w0"].shape[-1]
    t = RWKV_ROWS
    nt = seq // t
    hpt = t // 8
    lo_w = 4 * LANES
    cur = lambda b, i: b * nt + i
    halo = lambda b, i: jnp.maximum((b * nt + i) * hpt - 1, 0)
    si = jnp.arange(MXU_TILE)
    seg = (si[:, None] // HEAD_DIM == si[None, :] // HEAD_DIM).astype(BF16)
    ti = jnp.arange(t)
    tril = ((ti[:, None] >= ti[None, :]) & (ti[:, None] // RWKV_CHUNK == ti[None, :] // RWKV_CHUNK)).astype(BF16)

    def full(a):
        return pl.BlockSpec(a.shape, lambda b, i: (0,) * a.ndim)

    consts = [p["mu_r"], p["mu_k"], p["mu_v"], p["mu_lo"], p["w0"], p["w2"], p["a0"], p["a2"], p["g2"],
              p["k_k"], p["k_a"], p["r_k"], p["ln_g"], p["ln_b"], seg, tril]
    col = lambda c: pl.BlockSpec((t, dim), lambda b, i: (cur(b, i), c))
    hcol = lambda c: pl.BlockSpec((8, dim), lambda b, i: (halo(b, i), c))
    return pl.pallas_call(
        _rwkv_kernel,
        grid=(batch, nt),
        in_specs=[col(0), col(1), col(2), pl.BlockSpec((t, lo_w), lambda b, i: (cur(b, i), lo_col)),
                  hcol(0), hcol(1), hcol(2), pl.BlockSpec((8, lo_w), lambda b, i: (halo(b, i), lo_col))]
                 + [full(a) for a in consts],
        out_specs=pl.BlockSpec((t, dim), lambda b, i: (cur(b, i), 0)),
        out_shape=jax.ShapeDtypeStruct((n, dim), BF16),
        scratch_shapes=[pltpu.VMEM((dim // LANES, LANES, LANES), F32)],
        compiler_params=_cparams(("parallel", "arbitrary")),
        name="rwkv7_chunked",
    )(z, z, z, z, z, z, z, z, *consts)


def _lru_kernel(x_ref, gb_ref, xh_ref, cw_ref, cb_ref, wg_ref, ba_ref, bx_ref, lam_ref, o_ref,
                xbuf, carry):
    t, ch = x_ref.shape
    first = pl.program_id(1) == 0

    @pl.when(first)
    def _():
        carry[...] = jnp.zeros_like(carry)

    xbuf[0:8, :] = jnp.where(first, 0.0, xh_ref[...])
    xbuf[8:, :] = x_ref[...]
    off = 8 - (LRU_CONV_WIDTH - 1)
    xc = jnp.broadcast_to(cb_ref[...], (t, ch))
    for j in range(LRU_CONV_WIDTH):
        xc = xc + cw_ref[j:j + 1, :] * xbuf[off + j:off + j + t, :]
    gr, gi = [], []
    for p in range(ch // LANES):
        gp = _bdot(xc[:, p * LANES:(p + 1) * LANES], wg_ref[p])
        gr.append(gp[:, :LANES])
        gi.append(gp[:, LANES:])
    r = _sigmoid(jnp.concatenate(gr, axis=1) + ba_ref[...])
    i = _sigmoid(jnp.concatenate(gi, axis=1) + bx_ref[...])
    log_a = -LRU_C * r * _softplus(-lam_ref[...])
    a = jnp.exp(log_a)
    u = jnp.sqrt(1.0 - jnp.exp(2.0 * log_a)) * (i * xc)
    a = a.reshape(t // 8, 8, ch)
    u = u.reshape(t // 8, 8, ch)
    row = lax.broadcasted_iota(jnp.int32, a.shape, 1)
    d = 1
    while d < 8:
        keep = row >= d
        a_sh = jnp.where(keep, pltpu.roll(a, d, axis=1), 1.0)
        u_sh = jnp.where(keep, pltpu.roll(u, d, axis=1), 0.0)
        u = a * u_sh + u
        a = a * a_sh
        d *= 2
    h_prev = carry[...]
    groups = []
    for g in range(t // 8):
        hg = u[g] + a[g] * h_prev
        groups.append(hg)
        h_prev = hg[7:8, :]
    h = jnp.concatenate(groups, axis=0)
    carry[...] = h_prev
    gb = gb_ref[...]
    gelu = 0.5 * gb * (1.0 + jnp.tanh(0.7978845608028654 * (gb + 0.044715 * (gb * gb * gb))))
    o_ref[...] = (h * gelu).astype(o_ref.dtype)


def _rglru(z, conv_w, conv_b, wa, ba, wx, bx, lam, batch, seq, x_col, g_col, t):
    n = z.shape[0]
    ch = lam.shape[-1]
    nt = seq // t
    hpt = t // 8
    nblk, bd = wa.shape[0], wa.shape[1]
    per = LANES // bd
    def blockdiag(w):
        w = w.reshape(nblk // per, per, bd, bd)
        out = jnp.zeros((nblk // per, LANES, LANES), F32)
        for q in range(per):
            out = out.at[:, q * bd:(q + 1) * bd, q * bd:(q + 1) * bd].set(w[:, q])
        return out
    wg = jnp.concatenate([blockdiag(wa), blockdiag(wx)], axis=-1).astype(BF16)
    cw = jnp.zeros((8, ch), F32).at[:LRU_CONV_WIDTH].set(conv_w.reshape(LRU_CONV_WIDTH, ch))
    cur = lambda b, i: b * nt + i
    halo = lambda b, i: jnp.maximum((b * nt + i) * hpt - 1, 0)
    vec = pl.BlockSpec((1, ch), lambda b, i: (0, 0))
    return pl.pallas_call(
        _lru_kernel,
        grid=(batch, nt),
        in_specs=[pl.BlockSpec((t, ch), lambda b, i: (cur(b, i), x_col)),
                  pl.BlockSpec((t, ch), lambda b, i: (cur(b, i), g_col)),
                  pl.BlockSpec((8, ch), lambda b, i: (halo(b, i), x_col)),
                  pl.BlockSpec((8, ch), lambda b, i: (0, 0)),
                  vec,
                  pl.BlockSpec(wg.shape, lambda b, i: (0, 0, 0)),
                  vec, vec, vec],
        out_specs=pl.BlockSpec((t, ch), lambda b, i: (cur(b, i), 0)),
        out_shape=jax.ShapeDtypeStruct((n, ch), BF16),
        scratch_shapes=[pltpu.VMEM((t + 8, ch), F32), pltpu.VMEM((1, ch), F32)],
        compiler_params=_cparams(("parallel", "arbitrary")),
        name="rglru",
    )(z, z, z, cw, conv_b.reshape(1, ch), wg, ba.reshape(1, ch), bx.reshape(1, ch), lam.reshape(1, ch))


def _moe_kernel(be_ref, tok_ref, dst_ref, h_hbm, w1_ref, w3_ref, w2_ref, o_hbm, x0, x1, o0, o1, gsem, ssem):
    i = pl.program_id(0)
    last = pl.num_programs(0) - 1
    xs, outs = (x0, x1), (o0, o1)

    def gather_row(j, r, s):
        return pltpu.make_async_copy(h_hbm.at[pl.ds(tok_ref[j, r], 1)], xs[s].at[pl.ds(r, 1)], gsem.at[s])

    def scatter_row(j, r, s):
        return pltpu.make_async_copy(outs[s].at[pl.ds(r, 1)], o_hbm.at[pl.ds(dst_ref[j, r], 1)], ssem.at[s])

    def gather_block(s):
        return pltpu.make_async_copy(h_hbm.at[pl.ds(0, MOE_BLOCK)], xs[s], gsem.at[s])

    def scatter_block(s):
        return pltpu.make_async_copy(outs[s], o_hbm.at[pl.ds(0, MOE_BLOCK)], ssem.at[s])

    @pl.when(i == 0)
    def _():
        o1[...] = jnp.zeros_like(o1)
        for r in range(MOE_BLOCK):
            gather_row(0, r, 0).start()

    def step(s):
        t = 1 - s
        gather_block(s).wait()

        @pl.when(i >= 1)
        def _():
            scatter_block(s).wait()

        nxt = jnp.minimum(i + 1, last)
        prv = jnp.maximum(i - 1, 0)
        f = w1_ref.shape[-1]
        n_groups = f // MOE_FCHUNK
        per_group = MOE_BLOCK // n_groups
        for c in range(n_groups):
            @pl.when(i >= 0)
            def _():
                for r in range(c * per_group, (c + 1) * per_group):
                    gather_row(nxt, r, t).start()
                    scatter_row(prv, r, t).start()

            fs = slice(c * MOE_FCHUNK, (c + 1) * MOE_FCHUNK)
            x = xs[s][...].astype(BF16)
            h1 = jnp.dot(x, w1_ref[0, :, fs], preferred_element_type=F32)
            h3 = jnp.dot(x, w3_ref[0, :, fs], preferred_element_type=F32)
            hid = (h1 * _sigmoid(h1) * h3).astype(BF16)
            part = jnp.dot(hid, w2_ref[0, fs, :], preferred_element_type=F32)
            if c == 0:
                outs[s][...] = part
            else:
                outs[s][...] += part

        @pl.when(i == last)
        def _():
            for r in range(MOE_BLOCK):
                scatter_row(i, r, s).start()
            scatter_block(t).wait()
            scatter_block(s).wait()
            gather_block(t).wait()

    for s in range(2):
        pl.when(i % 2 == s)(functools.partial(step, s))


def _moe_experts(h, block_e, row_tok, row_dst, w1, w3, w2):
    n, d = h.shape
    f = w1.shape[-1]
    n_blocks = row_dst.shape[0]
    rows = n_blocks * MOE_BLOCK
    grid_spec = pltpu.PrefetchScalarGridSpec(
        num_scalar_prefetch=3,
        grid=(n_blocks,),
        in_specs=[pl.BlockSpec(memory_space=pl.ANY),
                  pl.BlockSpec((1, d, f), lambda i, be, tok, dst: (be[i], 0, 0)),
                  pl.BlockSpec((1, d, f), lambda i, be, tok, dst: (be[i], 0, 0)),
                  pl.BlockSpec((1, f, d), lambda i, be, tok, dst: (be[i], 0, 0))],
        out_specs=pl.BlockSpec(memory_space=pl.ANY),
        scratch_shapes=[pltpu.VMEM((MOE_BLOCK, d), F32)] * 4
                       + [pltpu.SemaphoreType.DMA((2,)), pltpu.SemaphoreType.DMA((2,))],
    )
    return pl.pallas_call(
        _moe_kernel,
        grid_spec=grid_spec,
        out_shape=jax.ShapeDtypeStruct((rows, d), F32),
        compiler_params=_cparams(("arbitrary",)),
        name="moe_experts",
    )(block_e, row_tok, row_dst, h, w1, w3, w2)


INVERT_UNROLL = 16


def _invert_kernel(trips_ref, dest_ref, placed_ref):
    def clear(b, carry):
        for u in range(INVERT_UNROLL):
            placed_ref[b * INVERT_UNROLL + u] = -1
        return carry

    lax.fori_loop(0, trips_ref[0], clear, 0)

    def place(b, carry):
        for u in range(INVERT_UNROLL):
            i = b * INVERT_UNROLL + u
            placed_ref[dest_ref[i]] = i
        return carry

    lax.fori_loop(0, trips_ref[1], place, 0)


def _invert_placement(dest_flat, rows):
    trips = jnp.array([rows // INVERT_UNROLL, dest_flat.shape[0] // INVERT_UNROLL], jnp.int32)
    smem = pl.BlockSpec(memory_space=pltpu.SMEM)
    return pl.pallas_call(
        _invert_kernel,
        in_specs=[smem, smem],
        out_specs=smem,
        out_shape=jax.ShapeDtypeStruct((rows,), jnp.int32),
        name="invert_placement",
    )(trips, dest_flat)


def _moe_plan(ridx, counts, n):
    nk = 2 * n
    e = ridx[0:2]
    rank = ridx[2:4]
    padded = (counts + MOE_BLOCK - 1) // MOE_BLOCK * MOE_BLOCK
    pend = jnp.cumsum(padded)
    pstart = pend - padded
    experts = jnp.arange(N_EXPERTS, dtype=jnp.int32)[:, None, None]
    dest = rank + jnp.sum(jnp.where(e[None] == experts, pstart[:, None, None], 0), axis=0)
    n_blocks = nk // MOE_BLOCK + N_EXPERTS
    rows = n_blocks * MOE_BLOCK
    placed = _invert_placement(dest.reshape(-1), rows)
    is_pad = placed < 0
    pad_rank = jnp.cumsum(is_pad.astype(jnp.int32)) - 1
    row_dst = jnp.where(is_pad, nk + pad_rank, placed)
    row_tok = jnp.where(is_pad, 0, jnp.where(placed >= n, placed - n, placed))
    block_start = jnp.arange(n_blocks, dtype=jnp.int32) * MOE_BLOCK
    block_e = jnp.minimum(jnp.sum(block_start[:, None] >= pend[None, :], axis=1), N_EXPERTS - 1).astype(jnp.int32)
    return block_e, row_tok.reshape(n_blocks, MOE_BLOCK), row_dst.reshape(n_blocks, MOE_BLOCK)


def _final_kernel(x_ref, o1_ref, o2_ref, rg_ref, gate_ref, g_ref, out_ref):
    g = rg_ref[...].T
    y = g[:, 0:1] * o1_ref[...] + g[:, 1:2] * o2_ref[...]
    x2 = x_ref[...] + gate_ref[0] * y
    ms = jnp.mean(x2 * x2, axis=-1, keepdims=True)
    out_ref[...] = x2 * lax.rsqrt(ms + EPS) * g_ref[...]


def _final_residual_norm(x, o_all, rgate, gate, g, seq, tm):
    n, d = x.shape
    per_seq = seq // tm
    nt = n // tm
    return pl.pallas_call(
        _final_kernel,
        grid=(nt,),
        in_specs=[pl.BlockSpec((tm, d), lambda i: (i, 0)),
                  pl.BlockSpec((tm, d), lambda i: (i, 0)),
                  pl.BlockSpec((tm, d), lambda i: (i + nt, 0)),
                  pl.BlockSpec((8, tm), lambda i: (0, i)),
                  pl.BlockSpec((1, 1, d), lambda i: (i // per_seq, 0, 0)),
                  pl.BlockSpec((1, d), lambda i: (0, 0))],
        out_specs=pl.BlockSpec((tm, d), lambda i: (i, 0)),
        out_shape=jax.ShapeDtypeStruct((n, d), F32),
        compiler_params=_cparams(("parallel",)),
        name="final_residual_norm",
    )(x, o_all, o_all, rgate, gate, g.reshape(1, d))


def _pad_cols(w, width):
    return jnp.pad(w, ((0, 0), (0, width - w.shape[1])))


def _pad_rows(w, height):
    return jnp.pad(w, ((0, height - w.shape[0]), (0, 0)))


def kernel(x, c, positions, router_w, router_bias, ada_w, ada_b, norm_mix, norm_ffn, moe_w1, moe_w3, moe_w2, ab_w_in, ab_sinks, ab_conv_w, ab_conv_b, ab_conv_ln_g, ab_conv_ln_b, ab_w_out, cd_w_in, cd_shift_mu, cd_w0, cd_w2, cd_a0, cd_a2, cd_g2, cd_k_k, cd_k_a, cd_r_k, cd_ln_x_g, cd_ln_x_b, cd_lru_conv_w, cd_lru_conv_b, cd_lru_wa, cd_lru_ba, cd_lru_wx, cd_lru_bx, cd_lru_lambda, cd_w_out, final_norm):
    batch, seq, d = x.shape
    n = batch * seq
    depth = ada_w.shape[0]
    xf = x.reshape(n, d)
    mod = _ada_mod(c, ada_w, ada_b).reshape(depth, batch, 6, 1, d)
    cos_t, sin_t = _rope_tables(positions)
    att_q = ATT_HEADS * HEAD_DIM
    att_kv = ATT_KV_HEADS * HEAD_DIM

    def in_proj(xf, layer, shift_m, scale_m, w_perm, pending):
        z = _in_proj(xf, norm_mix[layer], shift_m, scale_m, w_perm, seq, PROJ_ROWS, PROJ_COLS, pending)
        return (z, xf) if pending is None else z

    pending = None
    for layer in range(depth):
        j = layer // 2
        shift_m, scale_m, gate_m, shift_f, scale_f, gate_f = (mod[layer, :, q] for q in range(6))
        if layer % 2 == 0:
            w_in = ab_w_in[j]
            w_perm = jnp.concatenate([w_in[:, :att_q], w_in[:, att_q + 2 * att_kv:],
                                      w_in[:, att_q:att_q + 2 * att_kv]], axis=1).astype(BF16)
            z, xf = in_proj(xf, layer, shift_m, scale_m, w_perm, pending)
            conv_ch = ab_conv_w.shape[-1]
            kv0 = (att_q + 2 * conv_ch) // att_kv
            a1 = _attention(z, cos_t, sin_t, ab_sinks[j], batch, seq, kv0, kv0 + 1)
            a2 = _conformer_conv(z, ab_conv_w[j], ab_conv_b[j], ab_conv_ln_g[j], ab_conv_ln_b[j],
                                 batch, seq, att_q // conv_ch, att_q // conv_ch + 1, CONV_ROWS)
            w_out = ab_w_out[j]
        else:
            w_in = cd_w_in[j]
            dim = cd_w0.shape[-1]
            dl = cd_w2.shape[1]
            al = cd_a2.shape[1]
            gl = cd_g2.shape[1]
            s0 = 3 * dim
            seg_zw = w_in[:, s0:s0 + dl]
            seg_za = w_in[:, s0 + dl:s0 + dl + al]
            seg_zg = w_in[:, s0 + dl + al:s0 + dl + al + gl]
            s1 = s0 + dl + al + gl
            w_perm = jnp.concatenate([w_in[:, :s0], w_in[:, s1:], _pad_cols(seg_zw, LANES),
                                      _pad_cols(seg_za, LANES), seg_zg], axis=1).astype(BF16)
            mu = cd_shift_mu[j]
            mu_lo = jnp.concatenate([jnp.pad(mu[s0:s0 + dl], (0, LANES - dl)),
                                     jnp.pad(mu[s0 + dl:s0 + dl + al], (0, LANES - al)),
                                     mu[s0 + dl + al:s1]]).reshape(1, -1)
            row = lambda a: a.reshape(1, -1)
            p = dict(mu_r=row(mu[:dim]), mu_k=row(mu[dim:2 * dim]), mu_v=row(mu[2 * dim:s0]), mu_lo=mu_lo,
                     w0=row(cd_w0[j]), w2=_pad_rows(cd_w2[j], LANES).astype(BF16),
                     a0=row(cd_a0[j]), a2=_pad_rows(cd_a2[j], LANES).astype(BF16), g2=cd_g2[j].astype(BF16),
                     k_k=row(cd_k_k[j]), k_a=row(cd_k_a[j]), r_k=row(cd_r_k[j]),
                     ln_g=row(cd_ln_x_g[j]), ln_b=row(cd_ln_x_b[j]))
            z, xf = in_proj(xf, layer, shift_m, scale_m, w_perm, pending)
            lo_col = (5 * dim) // (4 * LANES)
            a1 = _rwkv(z, p, batch, seq, lo_col)
            a2 = _rglru(z, cd_lru_conv_w[j], cd_lru_conv_b[j], cd_lru_wa[j], cd_lru_ba[j],
                        cd_lru_wx[j], cd_lru_bx[j], cd_lru_lambda[j], batch, seq, 3, 4, LRU_ROWS)
            w_out = cd_w_out[j]
        x1, h, ridx, rgate, counts = _out_proj(a1, a2, w_out, xf, gate_m, norm_ffn[layer], shift_f, scale_f,
                                               router_w, router_bias, seq, PROJ_ROWS)
        block_e, row_tok, row_dst = _moe_plan(ridx, counts[:, 0], n)
        o_all = _moe_experts(h, block_e, row_tok, row_dst, moe_w1[layer].astype(BF16),
                             moe_w3[layer].astype(BF16), moe_w2[layer].astype(BF16))
        xf, pending = x1, (o_all, rgate, gate_f)
    out = _final_residual_norm(xf, *pending, final_norm, seq, FINAL_ROWS)
    return out.reshape(batch, seq, d)
```

```python
import functools

import jax
import jax.numpy as jnp
from jax import lax
from jax.experimental import pallas as pl
from jax.experimental.pallas import tpu as pltpu

F32 = jnp.float32
BF16 = jnp.bfloat16

EPS = 1e-6
LN_EPS = 1e-5
RWKV_LN_EPS = 64e-5
HEAD_DIM = 64
LANES = 128
MXU_TILE = 256
ATT_BLOCK = 128
ATT_STEP = 256
WINDOW = 128
ATT_HEADS = 16
ATT_KV_HEADS = 4
CONV_WIDTH = 31
CONV_HALO = 32
LRU_CONV_WIDTH = 4
LRU_C = 8.0
N_EXPERTS = 16
N_GROUPS = 4
EXPERTS_PER_GROUP = 4
MOE_BLOCK = 256
MOE_FCHUNK = 256
RWKV_CHUNK = 64
RWKV_ROWS = 256
ROPE_THETA = 10000.0
VMEM_LIMIT = 56 * 1024 * 1024
PROJ_ROWS = 256
PROJ_COLS = 512
CONV_ROWS = 256
LRU_ROWS = 256
FINAL_ROWS = 512


def _cparams(sem):
    return pltpu.CompilerParams(dimension_semantics=sem, vmem_limit_bytes=VMEM_LIMIT)


def _bdot(a, b):
    return jnp.dot(a.astype(BF16), b.astype(BF16), preferred_element_type=F32)


def _split2(x):
    hi = x.astype(BF16)
    lo = (x - hi.astype(F32)).astype(BF16)
    return hi, lo


def _dot_left_exact(m_bf16, x):
    hi, lo = _split2(x)
    d = functools.partial(jnp.dot, preferred_element_type=F32)
    return d(m_bf16, hi) + d(m_bf16, lo)


def _dot_right_exact(x, m_bf16):
    hi, lo = _split2(x)
    d = functools.partial(jnp.dot, preferred_element_type=F32)
    return d(hi, m_bf16) + d(lo, m_bf16)


def _dot_3pass(a, b):
    a_hi = a.astype(BF16)
    a_lo = (a - a_hi.astype(F32)).astype(BF16)
    b_hi = b.astype(BF16)
    b_lo = (b - b_hi.astype(F32)).astype(BF16)
    d = functools.partial(jnp.dot, preferred_element_type=F32)
    return d(a_hi, b_hi) + d(a_hi, b_lo) + d(a_lo, b_hi)


def _sigmoid(x):
    return 1.0 / (1.0 + jnp.exp(-x))


def _softplus(x):
    return jnp.maximum(x, 0.0) + jnp.log(1.0 + jnp.exp(-jnp.abs(x)))


def _rms_mod(x, g, shift, scale):
    ms = jnp.mean(x * x, axis=-1, keepdims=True)
    return (x * lax.rsqrt(ms + EPS) * g) * (1.0 + scale) + shift


def _ada_kernel(c_ref, w_ref, b_ref, o_ref):
    c = c_ref[...]
    o_ref[0] = _dot_3pass(c * _sigmoid(c), w_ref[0]) + b_ref[0]


def _ada_mod(c, ada_w, ada_b):
    depth, d, n6 = ada_w.shape
    b = c.shape[0]
    bp = 8
    tn = 768
    c_pad = jnp.zeros((bp, d), F32).at[:b].set(c)
    out = pl.pallas_call(
        _ada_kernel,
        grid=(depth, n6 // tn),
        in_specs=[pl.BlockSpec((bp, d), lambda l, j: (0, 0)),
                  pl.BlockSpec((1, d, tn), lambda l, j: (l, 0, j)),
                  pl.BlockSpec((1, 1, tn), lambda l, j: (l, 0, j))],
        out_specs=pl.BlockSpec((1, bp, tn), lambda l, j: (l, 0, j)),
        out_shape=jax.ShapeDtypeStruct((depth, bp, n6), F32),
        compiler_params=_cparams(("parallel", "parallel")),
        name="ada_mod",
    )(c_pad, ada_w, ada_b.reshape(depth, 1, n6))
    return out[:, :b]


def _rope_kernel(pos_ref, f_ref, cos_ref, sin_ref):
    ang = pos_ref[...].astype(F32) * f_ref[...]
    lane = lax.broadcasted_iota(jnp.int32, ang.shape, 1)
    s = jnp.sin(ang)
    cos_ref[...] = jnp.cos(ang)
    sin_ref[...] = jnp.where((lane & 32) == 0, -s, s)


def _rope_tables(positions):
    n = positions.size
    half = HEAD_DIM // 2
    inv_freq = ROPE_THETA ** (-jnp.arange(half, dtype=F32) / half)
    f_row = jnp.tile(inv_freq, LANES // half).reshape(1, LANES)
    tm = 512
    return pl.pallas_call(
        _rope_kernel,
        grid=(n // tm,),
        in_specs=[pl.BlockSpec((tm, 1), lambda i: (i, 0)),
                  pl.BlockSpec((1, LANES), lambda i: (0, 0))],
        out_specs=[pl.BlockSpec((tm, LANES), lambda i: (i, 0))] * 2,
        out_shape=[jax.ShapeDtypeStruct((n, LANES), F32)] * 2,
        compiler_params=_cparams(("parallel",)),
        name="rope_tables",
    )(positions.reshape(n, 1), f_row)


def _in_proj_kernel(*refs, tn, combine):
    if combine:
        x1_ref, o1_ref, o2_ref, rg_ref, gate_ref, g_ref, sh_ref, sc_ref, w_ref, o_ref, x_out_ref = refs
        gates = rg_ref[...].T
        x = x1_ref[...] + gate_ref[0] * (gates[:, 0:1] * o1_ref[...] + gates[:, 1:2] * o2_ref[...])
        x_out_ref[...] = x
    else:
        x_ref, g_ref, sh_ref, sc_ref, w_ref, o_ref = refs
        x = x_ref[...]
    h = _rms_mod(x, g_ref[...], sh_ref[0], sc_ref[0]).astype(BF16)
    for c in range(o_ref.shape[1] // tn):
        o_ref[:, c * tn:(c + 1) * tn] = jnp.dot(h, w_ref[:, c * tn:(c + 1) * tn], preferred_element_type=F32)


def _in_proj(x, g, shift, scale, w_bf16, seq, tm, tn, combine=None):
    n, d = x.shape
    nout = w_bf16.shape[1]
    per_seq = seq // tm
    nt = n // tm
    row_spec = pl.BlockSpec((tm, d), lambda i: (i, 0))
    mod_spec = pl.BlockSpec((1, 1, d), lambda i: (i // per_seq, 0, 0))
    tail_specs = [pl.BlockSpec((1, d), lambda i: (0, 0)), mod_spec, mod_spec,
                  pl.BlockSpec((d, nout), lambda i: (0, 0), pipeline_mode=pl.Buffered(1))]
    tail_args = (g.reshape(1, d), shift, scale, w_bf16)
    z_spec = pl.BlockSpec((tm, nout), lambda i: (i, 0))
    z_shape = jax.ShapeDtypeStruct((n, nout), F32)
    if combine is None:
        in_specs, args, out_specs, out_shape = [row_spec], (x,), z_spec, z_shape
    else:
        o_all, rgate, gate_f = combine
        in_specs = [row_spec, row_spec, pl.BlockSpec((tm, d), lambda i: (i + nt, 0)),
                    pl.BlockSpec((8, tm), lambda i: (0, i)), mod_spec]
        args = (x, o_all, o_all, rgate, gate_f)
        out_specs = [z_spec, row_spec]
        out_shape = [z_shape, jax.ShapeDtypeStruct((n, d), F32)]
    return pl.pallas_call(
        functools.partial(_in_proj_kernel, tn=tn, combine=combine is not None),
        grid=(nt,),
        in_specs=in_specs + tail_specs,
        out_specs=out_specs,
        out_shape=out_shape,
        compiler_params=_cparams(("parallel",)),
        name="in_proj",
    )(*args, *tail_args)


def _route(logits_t, bias_col):
    m = jnp.max(logits_t, axis=0, keepdims=True)
    e = jnp.exp(logits_t - m)
    probs = e / jnp.sum(e, axis=0, keepdims=True)
    sel = probs + bias_col
    s = [sel[i:i + 1, :] for i in range(N_EXPERTS)]
    pr = [probs[i:i + 1, :] for i in range(N_EXPERTS)]

    def top2_sum(v):
        best = v[0] + v[1]
        for i in range(len(v)):
            for j in range(i + 1, len(v)):
                if (i, j) != (0, 1):
                    best = jnp.maximum(best, v[i] + v[j])
        return best

    gs = [top2_sum(s[4 * g:4 * g + 4]) for g in range(N_GROUPS)]
    best = gs[0]
    gi = jnp.zeros_like(best, dtype=jnp.int32)
    for g in range(1, N_GROUPS):
        upd = gs[g] > best
        gi = jnp.where(upd, g, gi)
        best = jnp.where(upd, gs[g], best)

    def pick_group(rows):
        out = []
        for l in range(EXPERTS_PER_GROUP):
            v = rows[l]
            for g in range(1, N_GROUPS):
                v = jnp.where(gi == g, rows[4 * g + l], v)
            out.append(v)
        return out

    ig = pick_group(s)
    pg = pick_group(pr)

    def argmax4(v):
        bv = v[0]
        bi = jnp.zeros_like(gi)
        for l in range(1, EXPERTS_PER_GROUP):
            upd = v[l] > bv
            bi = jnp.where(upd, l, bi)
            bv = jnp.where(upd, v[l], bv)
        return bi

    i1 = argmax4(ig)
    i2 = argmax4([jnp.where(i1 == l, -jnp.inf, ig[l]) for l in range(EXPERTS_PER_GROUP)])

    def pick(v, idx):
        o = v[0]
        for l in range(1, EXPERTS_PER_GROUP):
            o = jnp.where(idx == l, v[l], o)
        return o

    p1 = pick(pg, i1)
    p2 = pick(pg, i2)
    tot = p1 + p2
    return gi * EXPERTS_PER_GROUP + i1, gi * EXPERTS_PER_GROUP + i2, p1 / tot, p2 / tot


def _out_proj_kernel(a1_ref, a2_ref, w1_ref, w2_ref, x_ref, gate_ref, g_ref, sh_ref, sc_ref,
                     rw_ref, rb_ref, upper_ref, xo_ref, h_ref, ridx_ref, rgate_ref, cnt_ref, carry):
    @pl.when(pl.program_id(0) == 0)
    def _():
        carry[...] = jnp.zeros_like(carry)

    y = (jnp.dot(a1_ref[...], w1_ref[...], preferred_element_type=F32)
         + jnp.dot(a2_ref[...], w2_ref[...], preferred_element_type=F32))
    x1 = x_ref[...] + gate_ref[0] * y
    xo_ref[...] = x1
    h = _rms_mod(x1, g_ref[...], sh_ref[0], sc_ref[0])
    h_ref[...] = h
    logits = _dot_3pass(h, rw_ref[...])
    logits_t = logits.T[:N_EXPERTS, :]
    e1, e2, g1, g2 = _route(logits_t, rb_ref[...])
    erow = lax.broadcasted_iota(jnp.int32, logits_t.shape, 0)
    hit1 = erow == e1
    hit2 = erow == e2
    onehot = jnp.where(hit1 | hit2, 1.0, 0.0)
    before = jnp.dot(onehot.astype(BF16), upper_ref[...], preferred_element_type=F32) + carry[:, 0:1]
    rank1 = jnp.sum(jnp.where(hit1, before, 0.0), axis=0, keepdims=True).astype(jnp.int32)
    rank2 = jnp.sum(jnp.where(hit2, before, 0.0), axis=0, keepdims=True).astype(jnp.int32)
    total = carry[...] + jnp.sum(onehot, axis=1, keepdims=True)
    carry[...] = total
    cnt_ref[...] = total.astype(jnp.int32)
    row = lax.broadcasted_iota(jnp.int32, ridx_ref.shape, 0)
    ridx_ref[...] = jnp.where(row == 0, e1, jnp.where(row == 1, e2,
                              jnp.where(row == 2, rank1, jnp.where(row == 3, rank2, 0))))
    rgate_ref[...] = jnp.where(row == 0, g1, jnp.where(row == 1, g2, 0.0))


def _out_proj(a1, a2, w_out, x, gate, g, shift, scale, router_w, router_bias, seq, tm):
    n, d = x.shape
    ka = a1.shape[1]
    per_seq = seq // tm
    w1 = w_out[:ka].astype(BF16)
    w2 = w_out[ka:].astype(BF16)
    rw = jnp.zeros((d, LANES), F32).at[:, :N_EXPERTS].set(router_w)
    rb = router_bias.astype(F32).reshape(N_EXPERTS, 1)
    upper = (jnp.arange(tm)[:, None] < jnp.arange(tm)[None, :]).astype(BF16)
    mod_spec = pl.BlockSpec((1, 1, d), lambda i: (i // per_seq, 0, 0))
    return pl.pallas_call(
        _out_proj_kernel,
        grid=(n // tm,),
        in_specs=[pl.BlockSpec((tm, ka), lambda i: (i, 0)),
                  pl.BlockSpec((tm, ka), lambda i: (i, 0)),
                  pl.BlockSpec((ka, d), lambda i: (0, 0), pipeline_mode=pl.Buffered(1)),
                  pl.BlockSpec((ka, d), lambda i: (0, 0), pipeline_mode=pl.Buffered(1)),
                  pl.BlockSpec((tm, d), lambda i: (i, 0)),
                  mod_spec,
                  pl.BlockSpec((1, d), lambda i: (0, 0)),
                  mod_spec, mod_spec,
                  pl.BlockSpec((d, LANES), lambda i: (0, 0)),
                  pl.BlockSpec((N_EXPERTS, 1), lambda i: (0, 0)),
                  pl.BlockSpec((tm, tm), lambda i: (0, 0))],
        out_specs=[pl.BlockSpec((tm, d), lambda i: (i, 0)),
                   pl.BlockSpec((tm, d), lambda i: (i, 0)),
                   pl.BlockSpec((8, tm), lambda i: (0, i)),
                   pl.BlockSpec((8, tm), lambda i: (0, i)),
                   pl.BlockSpec((N_EXPERTS, LANES), lambda i: (0, 0))],
        out_shape=[jax.ShapeDtypeStruct((n, d), F32),
                   jax.ShapeDtypeStruct((n, d), F32),
                   jax.ShapeDtypeStruct((8, n), jnp.int32),
                   jax.ShapeDtypeStruct((8, n), F32),
                   jax.ShapeDtypeStruct((N_EXPERTS, LANES), jnp.int32)],
        scratch_shapes=[pltpu.VMEM((N_EXPERTS, LANES), F32)],
        compiler_params=_cparams(("arbitrary",)),
        name="out_proj_route",
    )(a1, a2, w1, w2, x, gate, g.reshape(1, d), shift, scale, rw, rb, upper)


def _rope_apply(x, cos, sin_signed):
    outs = []
    lane = lax.broadcasted_iota(jnp.int32, cos.shape, 1)
    first_half = (lane & 32) == 0
    for c in range(x.shape[1] // LANES):
        xc = x[:, c * LANES:(c + 1) * LANES]
        partner = jnp.where(first_half, pltpu.roll(xc, LANES - 32, axis=1), pltpu.roll(xc, 32, axis=1))
        outs.append(xc * cos + partner * sin_signed)
    return outs


def _attn_kernel(sink_ref, q_ref, kp_ref, kc_ref, vp_ref, vc_ref, cq_ref, sq_ref, cp_ref, sp_ref, o_ref):
    has_prev = pl.program_id(1) > 0
    n_sub = q_ref.shape[0] // ATT_BLOCK
    cq, sq = cq_ref[...], sq_ref[...]
    q_cols = _rope_apply(q_ref[...] * (HEAD_DIM ** -0.5), cq, sq)
    k_cols = [jnp.concatenate([kp, kc], axis=0) for kp, kc in
              zip(_rope_apply(kp_ref[...], cp_ref[...], sp_ref[...]), _rope_apply(kc_ref[...], cq, sq))]
    v_all = jnp.concatenate([vp_ref[...], vc_ref[...]], axis=0).astype(BF16)
    qi = lax.broadcasted_iota(jnp.int32, (ATT_BLOCK, 2 * ATT_BLOCK), 0)
    kj = lax.broadcasted_iota(jnp.int32, (ATT_BLOCK, 2 * ATT_BLOCK), 1)
    in_window = (kj > qi) & (kj < ATT_BLOCK)
    causal = (kj >= ATT_BLOCK) & (kj - ATT_BLOCK <= qi)
    valid = [(in_window & has_prev) | causal] + [in_window | causal] * (n_sub - 1)
    group = ATT_HEADS // ATT_KV_HEADS
    units = [(u, hq) for u in range(n_sub) for hq in range(ATT_HEADS)]
    scores = []
    for u, hq in units:
        hk = hq // group
        qs = slice(u * ATT_BLOCK, (u + 1) * ATT_BLOCK)
        ks = slice(u * ATT_BLOCK, (u + 2) * ATT_BLOCK)
        qh = q_cols[hq // 2][qs, (hq % 2) * HEAD_DIM:(hq % 2 + 1) * HEAD_DIM].astype(BF16)
        kh = k_cols[hk // 2][ks, (hk % 2) * HEAD_DIM:(hk % 2 + 1) * HEAD_DIM].astype(BF16)
        s = lax.dot_general(qh, kh, (((1,), (1,)), ((), ())), preferred_element_type=F32)
        scores.append(jnp.where(valid[u], s, -1e30))
    ids = range(len(units))
    tops = [jnp.maximum(jnp.max(scores[i], axis=-1, keepdims=True), sink_ref[units[i][1]]) for i in ids]
    probs = [jnp.exp(scores[i] - tops[i]) for i in ids]
    dens = [jnp.sum(probs[i], axis=-1, keepdims=True) + jnp.exp(sink_ref[units[i][1]] - tops[i]) for i in ids]
    outs = []
    for i in ids:
        u, hq = units[i]
        hk = hq // group
        vh = v_all[u * ATT_BLOCK:(u + 2) * ATT_BLOCK, hk * HEAD_DIM:(hk + 1) * HEAD_DIM]
        outs.append(jnp.dot(probs[i].astype(BF16), vh, preferred_element_type=F32) / dens[i])
    for u in range(n_sub):
        base = u * ATT_HEADS
        cols = [jnp.concatenate([outs[base + 2 * p], outs[base + 2 * p + 1]], axis=1) for p in range(ATT_HEADS // 2)]
        o_ref[u * ATT_BLOCK:(u + 1) * ATT_BLOCK, :] = jnp.concatenate(cols, axis=1).astype(o_ref.dtype)


def _attention(z, cos_t, sin_t, sinks, batch, seq, k_col, v_col):
    n = z.shape[0]
    nb = seq // ATT_STEP
    sub = ATT_STEP // ATT_BLOCK
    att_q = ATT_HEADS * HEAD_DIM
    att_kv = ATT_KV_HEADS * HEAD_DIM
    cur = lambda b, i: b * nb + i
    prev = lambda b, i: jnp.maximum((b * nb + i) * sub - 1, 0)
    return pl.pallas_call(
        _attn_kernel,
        grid=(batch, nb),
        in_specs=[pl.BlockSpec(memory_space=pltpu.SMEM),
                  pl.BlockSpec((ATT_STEP, att_q), lambda b, i: (cur(b, i), 0)),
                  pl.BlockSpec((ATT_BLOCK, att_kv), lambda b, i: (prev(b, i), k_col)),
                  pl.BlockSpec((ATT_STEP, att_kv), lambda b, i: (cur(b, i), k_col)),
                  pl.BlockSpec((ATT_BLOCK, att_kv), lambda b, i: (prev(b, i), v_col)),
                  pl.BlockSpec((ATT_STEP, att_kv), lambda b, i: (cur(b, i), v_col)),
                  pl.BlockSpec((ATT_STEP, LANES), lambda b, i: (cur(b, i), 0)),
                  pl.BlockSpec((ATT_STEP, LANES), lambda b, i: (cur(b, i), 0)),
                  pl.BlockSpec((ATT_BLOCK, LANES), lambda b, i: (prev(b, i), 0)),
                  pl.BlockSpec((ATT_BLOCK, LANES), lambda b, i: (prev(b, i), 0))],
        out_specs=pl.BlockSpec((ATT_STEP, att_q), lambda b, i: (cur(b, i), 0)),
        out_shape=jax.ShapeDtypeStruct((n, att_q), BF16),
        compiler_params=_cparams(("parallel", "parallel")),
        name="swa_attention",
    )(sinks.astype(F32), z, z, z, z, z, cos_t, sin_t, cos_t, sin_t)


def _conv_kernel(val_ref, gate_ref, hval_ref, hgate_ref, w_ref, b_ref, g_ref, beta_ref, o_ref, ybuf, acc, shifted):
    t = val_ref.shape[0]
    ch = val_ref.shape[1]
    first = pl.program_id(1) == 0
    yh = hval_ref[...] * _sigmoid(hgate_ref[...])
    ybuf[0:CONV_HALO, :] = jnp.where(first, 0.0, yh)
    ybuf[CONV_HALO:, :] = val_ref[...] * _sigmoid(gate_ref[...])
    off = CONV_HALO - (CONV_WIDTH - 1)
    span = t + CONV_HALO - 8
    for ph in range(1, 8):
        shifted[ph - 1] = ybuf[ph:ph + span, :]
    rt = min(t, 128)
    for cb in range(ch // LANES):
        cs = slice(cb * LANES, (cb + 1) * LANES)
        for r0 in range(0, t, rt):
            a = jnp.broadcast_to(b_ref[:, cs], (rt, LANES))
            for j in range(CONV_WIDTH):
                ph, base = (off + j) % 8, (off + j) // 8 * 8
                src = ybuf if ph == 0 else shifted.at[ph - 1]
                a = a + w_ref[j:j + 1, cs] * src[r0 + base:r0 + base + rt, cs]
            acc[r0:r0 + rt, cs] = a
    y = acc[...]
    mu = jnp.mean(y, axis=-1, keepdims=True)
    yc = y - mu
    var = jnp.mean(yc * yc, axis=-1, keepdims=True)
    ln = yc * lax.rsqrt(var + LN_EPS) * g_ref[...] + beta_ref[...]
    o_ref[...] = (ln * _sigmoid(ln)).astype(o_ref.dtype)


def _conformer_conv(z, conv_w, conv_b, ln_g, ln_b, batch, seq, val_col, gate_col, t):
    n = z.shape[0]
    ch = conv_w.shape[-1]
    nt = seq // t
    hpt = t // CONV_HALO
    w = jnp.zeros((CONV_HALO, ch), F32).at[:CONV_WIDTH].set(conv_w.reshape(CONV_WIDTH, ch))
    cur = lambda b, i: b * nt + i
    halo = lambda b, i: jnp.maximum((b * nt + i) * hpt - 1, 0)
    vec = pl.BlockSpec((1, ch), lambda b, i: (0, 0))
    return pl.pallas_call(
        _conv_kernel,
        grid=(batch, nt),
        in_specs=[pl.BlockSpec((t, ch), lambda b, i: (cur(b, i), val_col)),
                  pl.BlockSpec((t, ch), lambda b, i: (cur(b, i), gate_col)),
                  pl.BlockSpec((CONV_HALO, ch), lambda b, i: (halo(b, i), val_col)),
                  pl.BlockSpec((CONV_HALO, ch), lambda b, i: (halo(b, i), gate_col)),
                  pl.BlockSpec((CONV_HALO, ch), lambda b, i: (0, 0)),
                  vec, vec, vec],
        out_specs=pl.BlockSpec((t, ch), lambda b, i: (cur(b, i), 0)),
        out_shape=jax.ShapeDtypeStruct((n, ch), BF16),
        scratch_shapes=[pltpu.VMEM((t + CONV_HALO, ch), F32), pltpu.VMEM((t, ch), F32),
                        pltpu.VMEM((7, t + CONV_HALO - 8, ch), F32)],
        compiler_params=_cparams(("parallel", "parallel")),
        name="conformer_conv",
    )(z, z, z, z, w, conv_b.reshape(1, ch), ln_g.reshape(1, ch), ln_b.reshape(1, ch))


def _shift_lerp(z, halo, mu, first):
    prev = pltpu.roll(z, 1, axis=0)
    row = lax.broadcasted_iota(jnp.int32, z.shape, 0)
    hrow = jnp.where(first, 0.0, halo[7:8, :])
    prev = jnp.where(row == 0, hrow, prev)
    return z + mu * (prev - z)


def _rwkv_kernel(r_ref, k_ref, v_ref, lo_ref, rh_ref, kh_ref, vh_ref, loh_ref,
                 mur_ref, muk_ref, muv_ref, mulo_ref, w0_ref, w2_ref, a0_ref, a2_ref, g2_ref,
                 kk_ref, ka_ref, rk_ref, lng_ref, lnb_ref, seg_ref, tril_ref,
                 o_ref, hs_ref):
    first = pl.program_id(1) == 0

    @pl.when(first)
    def _():
        hs_ref[...] = jnp.zeros_like(hs_ref)

    r = _shift_lerp(r_ref[...], rh_ref[...], mur_ref[...], first)
    k = _shift_lerp(k_ref[...], kh_ref[...], muk_ref[...], first)
    v = _shift_lerp(v_ref[...], vh_ref[...], muv_ref[...], first)
    lo = _shift_lerp(lo_ref[...], loh_ref[...], mulo_ref[...], first)
    zw, za, zg = lo[:, 0:LANES], lo[:, LANES:2 * LANES], lo[:, 2 * LANES:]

    seg = seg_ref[...]
    seg_w = seg.shape[0]

    def head_sum(x):
        return jnp.concatenate([_dot_right_exact(x[:, c:c + seg_w], seg) for c in range(0, x.shape[1], seg_w)],
                               axis=1)

    w_raw = w0_ref[...] + _bdot(jnp.tanh(zw), w2_ref[...])
    log_decay = -jnp.exp(-_softplus(-w_raw) - 0.5)
    a_gate = _sigmoid(a0_ref[...] + _bdot(za, a2_ref[...]))
    g = _bdot(_sigmoid(zg), g2_ref[...])
    kk = k * kk_ref[...]
    kk = kk * lax.rsqrt(jnp.maximum(head_sum(kk * kk), 1e-24))
    k_mod = k * (1.0 + (a_gate - 1.0) * ka_ref[...])
    b_vec = kk * a_gate

    n_chunks = r.shape[0] // RWKV_CHUNK
    rows = [slice(c * RWKV_CHUNK, (c + 1) * RWKV_CHUNK) for c in range(n_chunks)]
    cum = _dot_left_exact(tril_ref[...], log_decay)
    cum_last = [cum[rs.stop - 1:rs.stop, :] for rs in rows]
    w_last = [jnp.exp(cl) for cl in cum_last]
    inv_w = jnp.exp(-cum)
    to_end = jnp.concatenate([jnp.exp(cum_last[c] - cum[rows[c], :]) for c in range(n_chunks)], axis=0)
    a_til = -kk * jnp.exp(cum - log_decay)
    r_til = r * jnp.exp(cum)
    b_til = b_vec * inv_w
    k_til = k_mod * inv_w
    b_hat = b_vec * to_end
    k_hat = k_mod * to_end

    lane = lax.broadcasted_iota(jnp.int32, (RWKV_CHUNK, LANES), 1)
    m0 = lane < HEAD_DIM
    ri = lax.broadcasted_iota(jnp.int32, (LANES, LANES), 0)
    ci = lax.broadcasted_iota(jnp.int32, (LANES, LANES), 1)
    same = (ri < HEAD_DIM) == (ci < HEAD_DIM)
    mask_sl = same & (ci < ri)
    mask_li = same & (ci <= ri)
    eye = ri == ci

    def stack(x):
        return jnp.concatenate([jnp.where(m0, x, 0.0), jnp.where(m0, 0.0, x)], axis=0)

    nt_dims = (((1,), (1,)), ((), ()))
    dot = functools.partial(jnp.dot, preferred_element_type=F32)
    n_pairs = r.shape[1] // LANES
    cols = [slice(p * LANES, (p + 1) * LANES) for p in range(n_pairs)]
    units = [(rows[c], cols[p]) for c in range(n_chunks) for p in range(n_pairs)]
    ids = range(len(units))
    xa_f = [stack(a_til[u]) for u in units]
    xr_f = [stack(r_til[u]) for u in units]
    vs = [stack(v[u]).astype(BF16) for u in units]
    m = [lax.dot_general(jnp.concatenate([xa_f[i], xr_f[i]], axis=0).astype(BF16),
                         jnp.concatenate([b_til[units[i]]] * 2 + [k_til[units[i]]] * 2, axis=0).astype(BF16),
                         nt_dims, preferred_element_type=F32) for i in ids]
    power = [jnp.where(mask_sl, m[i][:LANES, :LANES], 0.0).astype(BF16) for i in ids]
    a_ak = [jnp.where(mask_sl, m[i][:LANES, LANES:], 0.0).astype(BF16) for i in ids]
    z = [jnp.concatenate([xa_f[i], dot(a_ak[i], vs[i])], axis=1) for i in ids]
    for level in range(6):
        z = [z[i] + dot(power[i], z[i].astype(BF16)) for i in ids]
        if level < 5:
            power = [dot(power[i], power[i]).astype(BF16) for i in ids]
    zb = [z[i].astype(BF16) for i in ids]
    m_rb = [jnp.where(mask_li, m[i][LANES:, :LANES], 0.0).astype(BF16) for i in ids]
    m_rk = [jnp.where(mask_li, m[i][LANES:, LANES:], 0.0).astype(BF16) for i in ids]
    q = [jnp.concatenate([xr_f[i], dot(m_rk[i], vs[i])], axis=1) + dot(m_rb[i], zb[i]) for i in ids]
    ge = [dot(stack(b_hat[units[i]]).T.astype(BF16), zb[i]) for i in ids]
    e_mat = [ge[i][:, LANES:] + dot(stack(k_hat[units[i]]).T.astype(BF16), vs[i]) for i in ids]
    qg = [jnp.concatenate([q[i][:, :LANES],
                           ge[i][:, :LANES] + jnp.where(eye, w_last[i // n_pairs][:, cols[i % n_pairs]], 0.0)],
                          axis=0).astype(BF16) for i in ids]
    state = [hs_ref[p] for p in range(n_pairs)]
    y_rows = []
    for c in range(n_chunks):
        y_cols = []
        for p in range(n_pairs):
            i = c * n_pairs + p
            yh = dot(qg[i], state[p].astype(BF16))
            state[p] = yh[LANES:] + e_mat[i]
            y2 = yh[:LANES] + q[i][:, LANES:]
            y_cols.append(y2[:RWKV_CHUNK] + y2[RWKV_CHUNK:])
        y_rows.append(jnp.concatenate(y_cols, axis=1))
    for p in range(n_pairs):
        hs_ref[p] = state[p]
    y = jnp.concatenate(y_rows, axis=0)

    inv_hd = 1.0 / HEAD_DIM
    mu = head_sum(y) * inv_hd
    yc = y - mu
    var = head_sum(yc * yc) * inv_hd
    yn = yc * lax.rsqrt(var + RWKV_LN_EPS) * lng_ref[...] + lnb_ref[...]
    bonus = head_sum(r * k_mod * rk_ref[...]) * v
    o_ref[...] = ((yn + bonus) * g).astype(o_ref.dtype)


def _rwkv(z, p, batch, seq, lo_col):
    n = z.shape[0]
    dim = p["w0"].shape[-1]
    t = RWKV_ROWS
    nt = seq // t
    hpt = t // 8
    lo_w = 4 * LANES
    cur = lambda b, i: b * nt + i
    halo = lambda b, i: jnp.maximum((b * nt + i) * hpt - 1, 0)
    si = jnp.arange(MXU_TILE)
    seg = (si[:, None] // HEAD_DIM == si[None, :] // HEAD_DIM).astype(BF16)
    ti = jnp.arange(t)
    tril = ((ti[:, None] >= ti[None, :]) & (ti[:, None] // RWKV_CHUNK == ti[None, :] // RWKV_CHUNK)).astype(BF16)

    def full(a):
        return pl.BlockSpec(a.shape, lambda b, i: (0,) * a.ndim)

    consts = [p["mu_r"], p["mu_k"], p["mu_v"], p["mu_lo"], p["w0"], p["w2"], p["a0"], p["a2"], p["g2"],
              p["k_k"], p["k_a"], p["r_k"], p["ln_g"], p["ln_b"], seg, tril]
    col = lambda c: pl.BlockSpec((t, dim), lambda b, i: (cur(b, i), c))
    hcol = lambda c: pl.BlockSpec((8, dim), lambda b, i: (halo(b, i), c))
    return pl.pallas_call(
        _rwkv_kernel,
        grid=(batch, nt),
        in_specs=[col(0), col(1), col(2), pl.BlockSpec((t, lo_w), lambda b, i: (cur(b, i), lo_col)),
                  hcol(0), hcol(1), hcol(2), pl.BlockSpec((8, lo_w), lambda b, i: (halo(b, i), lo_col))]
                 + [full(a) for a in consts],
        out_specs=pl.BlockSpec((t, dim), lambda b, i: (cur(b, i), 0)),
        out_shape=jax.ShapeDtypeStruct((n, dim), BF16),
        scratch_shapes=[pltpu.VMEM((dim // LANES, LANES, LANES), F32)],
        compiler_params=_cparams(("parallel", "arbitrary")),
        name="rwkv7_chunked",
    )(z, z, z, z, z, z, z, z, *consts)


def _lru_kernel(x_ref, gb_ref, xh_ref, cw_ref, cb_ref, wg_ref, ba_ref, bx_ref, lam_ref, o_ref,
                xbuf, carry):
    t, ch = x_ref.shape
    first = pl.program_id(1) == 0

    @pl.when(first)
    def _():
        carry[...] = jnp.zeros_like(carry)

    xbuf[0:8, :] = jnp.where(first, 0.0, xh_ref[...])
    xbuf[8:, :] = x_ref[...]
    off = 8 - (LRU_CONV_WIDTH - 1)
    xc = jnp.broadcast_to(cb_ref[...], (t, ch))
    for j in range(LRU_CONV_WIDTH):
        xc = xc + cw_ref[j:j + 1, :] * xbuf[off + j:off + j + t, :]
    gr, gi = [], []
    for p in range(ch // LANES):
        gp = _bdot(xc[:, p * LANES:(p + 1) * LANES], wg_ref[p])
        gr.append(gp[:, :LANES])
        gi.append(gp[:, LANES:])
    r = _sigmoid(jnp.concatenate(gr, axis=1) + ba_ref[...])
    i = _sigmoid(jnp.concatenate(gi, axis=1) + bx_ref[...])
    log_a = -LRU_C * r * _softplus(-lam_ref[...])
    a = jnp.exp(log_a)
    u = jnp.sqrt(1.0 - jnp.exp(2.0 * log_a)) * (i * xc)
    a = a.reshape(t // 8, 8, ch)
    u = u.reshape(t // 8, 8, ch)
    row = lax.broadcasted_iota(jnp.int32, a.shape, 1)
    d = 1
    while d < 8:
        keep = row >= d
        a_sh = jnp.where(keep, pltpu.roll(a, d, axis=1), 1.0)
        u_sh = jnp.where(keep, pltpu.roll(u, d, axis=1), 0.0)
        u = a * u_sh + u
        a = a * a_sh
        d *= 2
    h_prev = carry[...]
    groups = []
    for g in range(t // 8):
        hg = u[g] + a[g] * h_prev
        groups.append(hg)
        h_prev = hg[7:8, :]
    h = jnp.concatenate(groups, axis=0)
    carry[...] = h_prev
    gb = gb_ref[...]
    gelu = 0.5 * gb * (1.0 + jnp.tanh(0.7978845608028654 * (gb + 0.044715 * (gb * gb * gb))))
    o_ref[...] = (h * gelu).astype(o_ref.dtype)


def _rglru(z, conv_w, conv_b, wa, ba, wx, bx, lam, batch, seq, x_col, g_col, t):
    n = z.shape[0]
    ch = lam.shape[-1]
    nt = seq // t
    hpt = t // 8
    nblk, bd = wa.shape[0], wa.shape[1]
    per = LANES // bd
    def blockdiag(w):
        w = w.reshape(nblk // per, per, bd, bd)
        out = jnp.zeros((nblk // per, LANES, LANES), F32)
        for q in range(per):
            out = out.at[:, q * bd:(q + 1) * bd, q * bd:(q + 1) * bd].set(w[:, q])
        return out
    wg = jnp.concatenate([blockdiag(wa), blockdiag(wx)], axis=-1).astype(BF16)
    cw = jnp.zeros((8, ch), F32).at[:LRU_CONV_WIDTH].set(conv_w.reshape(LRU_CONV_WIDTH, ch))
    cur = lambda b, i: b * nt + i
    halo = lambda b, i: jnp.maximum((b * nt + i) * hpt - 1, 0)
    vec = pl.BlockSpec((1, ch), lambda b, i: (0, 0))
    return pl.pallas_call(
        _lru_kernel,
        grid=(batch, nt),
        in_specs=[pl.BlockSpec((t, ch), lambda b, i: (cur(b, i), x_col)),
                  pl.BlockSpec((t, ch), lambda b, i: (cur(b, i), g_col)),
                  pl.BlockSpec((8, ch), lambda b, i: (halo(b, i), x_col)),
                  pl.BlockSpec((8, ch), lambda b, i: (0, 0)),
                  vec,
                  pl.BlockSpec(wg.shape, lambda b, i: (0, 0, 0)),
                  vec, vec, vec],
        out_specs=pl.BlockSpec((t, ch), lambda b, i: (cur(b, i), 0)),
        out_shape=jax.ShapeDtypeStruct((n, ch), BF16),
        scratch_shapes=[pltpu.VMEM((t + 8, ch), F32), pltpu.VMEM((1, ch), F32)],
        compiler_params=_cparams(("parallel", "arbitrary")),
        name="rglru",
    )(z, z, z, cw, conv_b.reshape(1, ch), wg, ba.reshape(1, ch), bx.reshape(1, ch), lam.reshape(1, ch))


def _moe_kernel(be_ref, tok_ref, dst_ref, h_hbm, w1_ref, w3_ref, w2_ref, o_hbm, x0, x1, o0, o1, gsem, ssem):
    i = pl.program_id(0)
    last = pl.num_programs(0) - 1
    xs, outs = (x0, x1), (o0, o1)

    def gather_row(j, r, s):
        return pltpu.make_async_copy(h_hbm.at[pl.ds(tok_ref[j, r], 1)], xs[s].at[pl.ds(r, 1)], gsem.at[s])

    def scatter_row(j, r, s):
        return pltpu.make_async_copy(outs[s].at[pl.ds(r, 1)], o_hbm.at[pl.ds(dst_ref[j, r], 1)], ssem.at[s])

    def gather_block(s):
        return pltpu.make_async_copy(h_hbm.at[pl.ds(0, MOE_BLOCK)], xs[s], gsem.at[s])

    def scatter_block(s):
        return pltpu.make_async_copy(outs[s], o_hbm.at[pl.ds(0, MOE_BLOCK)], ssem.at[s])

    @pl.when(i == 0)
    def _():
        o1[...] = jnp.zeros_like(o1)
        for r in range(MOE_BLOCK):
            gather_row(0, r, 0).start()

    def step(s):
        t = 1 - s
        gather_block(s).wait()

        @pl.when(i >= 1)
        def _():
            scatter_block(s).wait()

        nxt = jnp.minimum(i + 1, last)
        prv = jnp.maximum(i - 1, 0)
        f = w1_ref.shape[-1]
        n_groups = f // MOE_FCHUNK
        per_group = MOE_BLOCK // n_groups
        for c in range(n_groups):
            @pl.when(i >= 0)
            def _():
                for r in range(c * per_group, (c + 1) * per_group):
                    gather_row(nxt, r, t).start()
                    scatter_row(prv, r, t).start()

            fs = slice(c * MOE_FCHUNK, (c + 1) * MOE_FCHUNK)
            x = xs[s][...].astype(BF16)
            h1 = jnp.dot(x, w1_ref[0, :, fs], preferred_element_type=F32)
            h3 = jnp.dot(x, w3_ref[0, :, fs], preferred_element_type=F32)
            hid = (h1 * _sigmoid(h1) * h3).astype(BF16)
            part = jnp.dot(hid, w2_ref[0, fs, :], preferred_element_type=F32)
            if c == 0:
                outs[s][...] = part
            else:
                outs[s][...] += part

        @pl.when(i == last)
        def _():
            for r in range(MOE_BLOCK):
                scatter_row(i, r, s).start()
            scatter_block(t).wait()
            scatter_block(s).wait()
            gather_block(t).wait()

    for s in range(2):
        pl.when(i % 2 == s)(functools.partial(step, s))


def _moe_experts(h, block_e, row_tok, row_dst, w1, w3, w2):
    n, d = h.shape
    f = w1.shape[-1]
    n_blocks = row_dst.shape[0]
    rows = n_blocks * MOE_BLOCK
    grid_spec = pltpu.PrefetchScalarGridSpec(
        num_scalar_prefetch=3,
        grid=(n_blocks,),
        in_specs=[pl.BlockSpec(memory_space=pl.ANY),
                  pl.BlockSpec((1, d, f), lambda i, be, tok, dst: (be[i], 0, 0)),
                  pl.BlockSpec((1, d, f), lambda i, be, tok, dst: (be[i], 0, 0)),
                  pl.BlockSpec((1, f, d), lambda i, be, tok, dst: (be[i], 0, 0))],
        out_specs=pl.BlockSpec(memory_space=pl.ANY),
        scratch_shapes=[pltpu.VMEM((MOE_BLOCK, d), F32)] * 4
                       + [pltpu.SemaphoreType.DMA((2,)), pltpu.SemaphoreType.DMA((2,))],
    )
    return pl.pallas_call(
        _moe_kernel,
        grid_spec=grid_spec,
        out_shape=jax.ShapeDtypeStruct((rows, d), F32),
        compiler_params=_cparams(("arbitrary",)),
        name="moe_experts",
    )(block_e, row_tok, row_dst, h, w1, w3, w2)


INVERT_UNROLL = 16


def _invert_kernel(trips_ref, dest_ref, placed_ref):
    def clear(b, carry):
        for u in range(INVERT_UNROLL):
            placed_ref[b * INVERT_UNROLL + u] = -1
        return carry

    lax.fori_loop(0, trips_ref[0], clear, 0)

    def place(b, carry):
        for u in range(INVERT_UNROLL):
            i = b * INVERT_UNROLL + u
            placed_ref[dest_ref[i]] = i
        return carry

    lax.fori_loop(0, trips_ref[1], place, 0)


def _invert_placement(dest_flat, rows):
    trips = jnp.array([rows // INVERT_UNROLL, dest_flat.shape[0] // INVERT_UNROLL], jnp.int32)
    smem = pl.BlockSpec(memory_space=pltpu.SMEM)
    return pl.pallas_call(
        _invert_kernel,
        in_specs=[smem, smem],
        out_specs=smem,
        out_shape=jax.ShapeDtypeStruct((rows,), jnp.int32),
        name="invert_placement",
    )(trips, dest_flat)


def _moe_plan(ridx, counts, n):
    nk = 2 * n
    e = ridx[0:2]
    rank = ridx[2:4]
    padded = (counts + MOE_BLOCK - 1) // MOE_BLOCK * MOE_BLOCK
    pend = jnp.cumsum(padded)
    pstart = pend - padded
    experts = jnp.arange(N_EXPERTS, dtype=jnp.int32)[:, None, None]
    dest = rank + jnp.sum(jnp.where(e[None] == experts, pstart[:, None, None], 0), axis=0)
    n_blocks = nk // MOE_BLOCK + N_EXPERTS
    rows = n_blocks * MOE_BLOCK
    placed = _invert_placement(dest.reshape(-1), rows)
    is_pad = placed < 0
    pad_rank = jnp.cumsum(is_pad.astype(jnp.int32)) - 1
    row_dst = jnp.where(is_pad, nk + pad_rank, placed)
    row_tok = jnp.where(is_pad, 0, jnp.where(placed >= n, placed - n, placed))
    block_start = jnp.arange(n_blocks, dtype=jnp.int32) * MOE_BLOCK
    block_e = jnp.minimum(jnp.sum(block_start[:, None] >= pend[None, :], axis=1), N_EXPERTS - 1).astype(jnp.int32)
    return block_e, row_tok.reshape(n_blocks, MOE_BLOCK), row_dst.reshape(n_blocks, MOE_BLOCK)


def _final_kernel(x_ref, o1_ref, o2_ref, rg_ref, gate_ref, g_ref, out_ref):
    g = rg_ref[...].T
    y = g[:, 0:1] * o1_ref[...] + g[:, 1:2] * o2_ref[...]
    x2 = x_ref[...] + gate_ref[0] * y
    ms = jnp.mean(x2 * x2, axis=-1, keepdims=True)
    out_ref[...] = x2 * lax.rsqrt(ms + EPS) * g_ref[...]


def _final_residual_norm(x, o_all, rgate, gate, g, seq, tm):
    n, d = x.shape
    per_seq = seq // tm
    nt = n // tm
    return pl.pallas_call(
        _final_kernel,
        grid=(nt,),
        in_specs=[pl.BlockSpec((tm, d), lambda i: (i, 0)),
                  pl.BlockSpec((tm, d), lambda i: (i, 0)),
                  pl.BlockSpec((tm, d), lambda i: (i + nt, 0)),
                  pl.BlockSpec((8, tm), lambda i: (0, i)),
                  pl.BlockSpec((1, 1, d), lambda i: (i // per_seq, 0, 0)),
                  pl.BlockSpec((1, d), lambda i: (0, 0))],
        out_specs=pl.BlockSpec((tm, d), lambda i: (i, 0)),
        out_shape=jax.ShapeDtypeStruct((n, d), F32),
        compiler_params=_cparams(("parallel",)),
        name="final_residual_norm",
    )(x, o_all, o_all, rgate, gate, g.reshape(1, d))


def _pad_cols(w, width):
    return jnp.pad(w, ((0, 0), (0, width - w.shape[1])))


def _pad_rows(w, height):
    return jnp.pad(w, ((0, height - w.shape[0]), (0, 0)))


def kernel(x, c, positions, router_w, router_bias, ada_w, ada_b, norm_mix, norm_ffn, moe_w1, moe_w3, moe_w2, ab_w_in, ab_sinks, ab_conv_w, ab_conv_b, ab_conv_ln_g, ab_conv_ln_b, ab_w_out, cd_w_in, cd_shift_mu, cd_w0, cd_w2, cd_a0, cd_a2, cd_g2, cd_k_k, cd_k_a, cd_r_k, cd_ln_x_g, cd_ln_x_b, cd_lru_conv_w, cd_lru_conv_b, cd_lru_wa, cd_lru_ba, cd_lru_wx, cd_lru_bx, cd_lru_lambda, cd_w_out, final_norm):
    batch, seq, d = x.shape
    n = batch * seq
    depth = ada_w.shape[0]
    xf = x.reshape(n, d)
    mod = _ada_mod(c, ada_w, ada_b).reshape(depth, batch, 6, 1, d)
    cos_t, sin_t = _rope_tables(positions)
    att_q = ATT_HEADS * HEAD_DIM
    att_kv = ATT_KV_HEADS * HEAD_DIM

    def in_proj(xf, layer, shift_m, scale_m, w_perm, pending):
        z = _in_proj(xf, norm_mix[layer], shift_m, scale_m, w_perm, seq, PROJ_ROWS, PROJ_COLS, pending)
        return (z, xf) if pending is None else z

    pending = None
    for layer in range(depth):
        j = layer // 2
        shift_m, scale_m, gate_m, shift_f, scale_f, gate_f = (mod[layer, :, q] for q in range(6))
        if layer % 2 == 0:
            w_in = ab_w_in[j]
            w_perm = jnp.concatenate([w_in[:, :att_q], w_in[:, att_q + 2 * att_kv:],
                                      w_in[:, att_q:att_q + 2 * att_kv]], axis=1).astype(BF16)
            z, xf = in_proj(xf, layer, shift_m, scale_m, w_perm, pending)
            conv_ch = ab_conv_w.shape[-1]
            kv0 = (att_q + 2 * conv_ch) // att_kv
            a1 = _attention(z, cos_t, sin_t, ab_sinks[j], batch, seq, kv0, kv0 + 1)
            a2 = _conformer_conv(z, ab_conv_w[j], ab_conv_b[j], ab_conv_ln_g[j], ab_conv_ln_b[j],
                                 batch, seq, att_q // conv_ch, att_q // conv_ch + 1, CONV_ROWS)
            w_out = ab_w_out[j]
        else:
            w_in = cd_w_in[j]
            dim = cd_w0.shape[-1]
            dl = cd_w2.shape[1]
            al = cd_a2.shape[1]
            gl = cd_g2.shape[1]
            s0 = 3 * dim
            seg_zw = w_in[:, s0:s0 + dl]
            seg_za = w_in[:, s0 + dl:s0 + dl + al]
            seg_zg = w_in[:, s0 + dl + al:s0 + dl + al + gl]
            s1 = s0 + dl + al + gl
            w_perm = jnp.concatenate([w_in[:, :s0], w_in[:, s1:], _pad_cols(seg_zw, LANES),
                                      _pad_cols(seg_za, LANES), seg_zg], axis=1).astype(BF16)
            mu = cd_shift_mu[j]
            mu_lo = jnp.concatenate([jnp.pad(mu[s0:s0 + dl], (0, LANES - dl)),
                                     jnp.pad(mu[s0 + dl:s0 + dl + al], (0, LANES - al)),
                                     mu[s0 + dl + al:s1]]).reshape(1, -1)
            row = lambda a: a.reshape(1, -1)
            p = dict(mu_r=row(mu[:dim]), mu_k=row(mu[dim:2 * dim]), mu_v=row(mu[2 * dim:s0]), mu_lo=mu_lo,
                     w0=row(cd_w0[j]), w2=_pad_rows(cd_w2[j], LANES).astype(BF16),
                     a0=row(cd_a0[j]), a2=_pad_rows(cd_a2[j], LANES).astype(BF16), g2=cd_g2[j].astype(BF16),
                     k_k=row(cd_k_k[j]), k_a=row(cd_k_a[j]), r_k=row(cd_r_k[j]),
                     ln_g=row(cd_ln_x_g[j]), ln_b=row(cd_ln_x_b[j]))
            z, xf = in_proj(xf, layer, shift_m, scale_m, w_perm, pending)
            lo_col = (5 * dim) // (4 * LANES)
            a1 = _rwkv(z, p, batch, seq, lo_col)
            a2 = _rglru(z, cd_lru_conv_w[j], cd_lru_conv_b[j], cd_lru_wa[j], cd_lru_ba[j],
                        cd_lru_wx[j], cd_lru_bx[j], cd_lru_lambda[j], batch, seq, 3, 4, LRU_ROWS)
            w_out = cd_w_out[j]
        x1, h, ridx, rgate, counts = _out_proj(a1, a2, w_out, xf, gate_m, norm_ffn[layer], shift_f, scale_f,
                                               router_w, router_bias, seq, PROJ_ROWS)
        block_e, row_tok, row_dst = _moe_plan(ridx, counts[:, 0], n)
        o_all = _moe_experts(h, block_e, row_tok, row_dst, moe_w1[layer].astype(BF16),
                             moe_w3[layer].astype(BF16), moe_w2[layer].astype(BF16))
        xf, pending = x1, (o_all, rgate, gate_f)
    out = _final_residual_norm(xf, *pending, final_norm, seq, FINAL_ROWS)
    return out.reshape(batch, seq, d)
```

```python
import functools

import jax
import jax.numpy as jnp
from jax import lax
from jax.experimental import pallas as pl
from jax.experimental.pallas import tpu as pltpu

F32 = jnp.float32
BF16 = jnp.bfloat16

EPS = 1e-6
LN_EPS = 1e-5
RWKV_LN_EPS = 64e-5
HEAD_DIM = 64
LANES = 128
MXU_TILE = 256
ATT_BLOCK = 128
WINDOW = 128
ATT_HEADS = 16
ATT_KV_HEADS = 4
CONV_WIDTH = 31
CONV_HALO = 32
LRU_CONV_WIDTH = 4
LRU_C = 8.0
N_EXPERTS = 16
N_GROUPS = 4
EXPERTS_PER_GROUP = 4
MOE_BLOCK = 256
MOE_FCHUNK = 256
RWKV_CHUNK = 64
RWKV_ROWS = 256
ROPE_THETA = 10000.0
VMEM_LIMIT = 56 * 1024 * 1024
PROJ_ROWS = 256
PROJ_COLS = 512
CONV_ROWS = 256
LRU_ROWS = 256
FINAL_ROWS = 512


def _cparams(sem):
    return pltpu.CompilerParams(dimension_semantics=sem, vmem_limit_bytes=VMEM_LIMIT)


def _bdot(a, b):
    return jnp.dot(a.astype(BF16), b.astype(BF16), preferred_element_type=F32)


def _split2(x):
    hi = x.astype(BF16)
    lo = (x - hi.astype(F32)).astype(BF16)
    return hi, lo


def _dot_left_exact(m_bf16, x):
    hi, lo = _split2(x)
    d = functools.partial(jnp.dot, preferred_element_type=F32)
    return d(m_bf16, hi) + d(m_bf16, lo)


def _dot_right_exact(x, m_bf16):
    hi, lo = _split2(x)
    d = functools.partial(jnp.dot, preferred_element_type=F32)
    return d(hi, m_bf16) + d(lo, m_bf16)


def _dot_3pass(a, b):
    a_hi = a.astype(BF16)
    a_lo = (a - a_hi.astype(F32)).astype(BF16)
    b_hi = b.astype(BF16)
    b_lo = (b - b_hi.astype(F32)).astype(BF16)
    d = functools.partial(jnp.dot, preferred_element_type=F32)
    return d(a_hi, b_hi) + d(a_hi, b_lo) + d(a_lo, b_hi)


def _sigmoid(x):
    return 1.0 / (1.0 + jnp.exp(-x))


def _softplus(x):
    return jnp.maximum(x, 0.0) + jnp.log(1.0 + jnp.exp(-jnp.abs(x)))


def _rms_mod(x, g, shift, scale):
    ms = jnp.mean(x * x, axis=-1, keepdims=True)
    return (x * lax.rsqrt(ms + EPS) * g) * (1.0 + scale) + shift


def _ada_kernel(c_ref, w_ref, b_ref, o_ref):
    c = c_ref[...]
    o_ref[0] = _dot_3pass(c * _sigmoid(c), w_ref[0]) + b_ref[0]


def _ada_mod(c, ada_w, ada_b):
    depth, d, n6 = ada_w.shape
    b = c.shape[0]
    bp = 8
    tn = 768
    c_pad = jnp.zeros((bp, d), F32).at[:b].set(c)
    out = pl.pallas_call(
        _ada_kernel,
        grid=(depth, n6 // tn),
        in_specs=[pl.BlockSpec((bp, d), lambda l, j: (0, 0)),
                  pl.BlockSpec((1, d, tn), lambda l, j: (l, 0, j)),
                  pl.BlockSpec((1, 1, tn), lambda l, j: (l, 0, j))],
        out_specs=pl.BlockSpec((1, bp, tn), lambda l, j: (l, 0, j)),
        out_shape=jax.ShapeDtypeStruct((depth, bp, n6), F32),
        compiler_params=_cparams(("parallel", "parallel")),
        name="ada_mod",
    )(c_pad, ada_w, ada_b.reshape(depth, 1, n6))
    return out[:, :b]


def _rope_kernel(pos_ref, f_ref, cos_ref, sin_ref):
    ang = pos_ref[...].astype(F32) * f_ref[...]
    lane = lax.broadcasted_iota(jnp.int32, ang.shape, 1)
    s = jnp.sin(ang)
    cos_ref[...] = jnp.cos(ang)
    sin_ref[...] = jnp.where((lane & 32) == 0, -s, s)


def _rope_tables(positions):
    n = positions.size
    half = HEAD_DIM // 2
    inv_freq = ROPE_THETA ** (-jnp.arange(half, dtype=F32) / half)
    f_row = jnp.tile(inv_freq, LANES // half).reshape(1, LANES)
    tm = 512
    return pl.pallas_call(
        _rope_kernel,
        grid=(n // tm,),
        in_specs=[pl.BlockSpec((tm, 1), lambda i: (i, 0)),
                  pl.BlockSpec((1, LANES), lambda i: (0, 0))],
        out_specs=[pl.BlockSpec((tm, LANES), lambda i: (i, 0))] * 2,
        out_shape=[jax.ShapeDtypeStruct((n, LANES), F32)] * 2,
        compiler_params=_cparams(("parallel",)),
        name="rope_tables",
    )(positions.reshape(n, 1), f_row)


def _in_proj_kernel(*refs, tn, combine):
    if combine:
        x1_ref, o1_ref, o2_ref, rg_ref, gate_ref, g_ref, sh_ref, sc_ref, w_ref, o_ref, x_out_ref = refs
        gates = rg_ref[...].T
        x = x1_ref[...] + gate_ref[0] * (gates[:, 0:1] * o1_ref[...] + gates[:, 1:2] * o2_ref[...])
        x_out_ref[...] = x
    else:
        x_ref, g_ref, sh_ref, sc_ref, w_ref, o_ref = refs
        x = x_ref[...]
    h = _rms_mod(x, g_ref[...], sh_ref[0], sc_ref[0]).astype(BF16)
    for c in range(o_ref.shape[1] // tn):
        o_ref[:, c * tn:(c + 1) * tn] = jnp.dot(h, w_ref[:, c * tn:(c + 1) * tn], preferred_element_type=F32)


def _in_proj(x, g, shift, scale, w_bf16, seq, tm, tn, combine=None):
    n, d = x.shape
    nout = w_bf16.shape[1]
    per_seq = seq // tm
    nt = n // tm
    row_spec = pl.BlockSpec((tm, d), lambda i: (i, 0))
    mod_spec = pl.BlockSpec((1, 1, d), lambda i: (i // per_seq, 0, 0))
    tail_specs = [pl.BlockSpec((1, d), lambda i: (0, 0)), mod_spec, mod_spec,
                  pl.BlockSpec((d, nout), lambda i: (0, 0), pipeline_mode=pl.Buffered(1))]
    tail_args = (g.reshape(1, d), shift, scale, w_bf16)
    z_spec = pl.BlockSpec((tm, nout), lambda i: (i, 0))
    z_shape = jax.ShapeDtypeStruct((n, nout), F32)
    if combine is None:
        in_specs, args, out_specs, out_shape = [row_spec], (x,), z_spec, z_shape
    else:
        o_all, rgate, gate_f = combine
        in_specs = [row_spec, row_spec, pl.BlockSpec((tm, d), lambda i: (i + nt, 0)),
                    pl.BlockSpec((8, tm), lambda i: (0, i)), mod_spec]
        args = (x, o_all, o_all, rgate, gate_f)
        out_specs = [z_spec, row_spec]
        out_shape = [z_shape, jax.ShapeDtypeStruct((n, d), F32)]
    return pl.pallas_call(
        functools.partial(_in_proj_kernel, tn=tn, combine=combine is not None),
        grid=(nt,),
        in_specs=in_specs + tail_specs,
        out_specs=out_specs,
        out_shape=out_shape,
        compiler_params=_cparams(("parallel",)),
        name="in_proj",
    )(*args, *tail_args)


def _route(logits_t, bias_col):
    m = jnp.max(logits_t, axis=0, keepdims=True)
    e = jnp.exp(logits_t - m)
    probs = e / jnp.sum(e, axis=0, keepdims=True)
    sel = probs + bias_col
    s = [sel[i:i + 1, :] for i in range(N_EXPERTS)]
    pr = [probs[i:i + 1, :] for i in range(N_EXPERTS)]

    def top2_sum(v):
        best = v[0] + v[1]
        for i in range(len(v)):
            for j in range(i + 1, len(v)):
                if (i, j) != (0, 1):
                    best = jnp.maximum(best, v[i] + v[j])
        return best

    gs = [top2_sum(s[4 * g:4 * g + 4]) for g in range(N_GROUPS)]
    best = gs[0]
    gi = jnp.zeros_like(best, dtype=jnp.int32)
    for g in range(1, N_GROUPS):
        upd = gs[g] > best
        gi = jnp.where(upd, g, gi)
        best = jnp.where(upd, gs[g], best)

    def pick_group(rows):
        out = []
        for l in range(EXPERTS_PER_GROUP):
            v = rows[l]
            for g in range(1, N_GROUPS):
                v = jnp.where(gi == g, rows[4 * g + l], v)
            out.append(v)
        return out

    ig = pick_group(s)
    pg = pick_group(pr)

    def argmax4(v):
        bv = v[0]
        bi = jnp.zeros_like(gi)
        for l in range(1, EXPERTS_PER_GROUP):
            upd = v[l] > bv
            bi = jnp.where(upd, l, bi)
            bv = jnp.where(upd, v[l], bv)
        return bi

    i1 = argmax4(ig)
    i2 = argmax4([jnp.where(i1 == l, -jnp.inf, ig[l]) for l in range(EXPERTS_PER_GROUP)])

    def pick(v, idx):
        o = v[0]
        for l in range(1, EXPERTS_PER_GROUP):
            o = jnp.where(idx == l, v[l], o)
        return o

    p1 = pick(pg, i1)
    p2 = pick(pg, i2)
    tot = p1 + p2
    return gi * EXPERTS_PER_GROUP + i1, gi * EXPERTS_PER_GROUP + i2, p1 / tot, p2 / tot


def _out_proj_kernel(a1_ref, a2_ref, w1_ref, w2_ref, x_ref, gate_ref, g_ref, sh_ref, sc_ref,
                     rw_ref, rb_ref, upper_ref, xo_ref, h_ref, ridx_ref, rgate_ref, cnt_ref, carry):
    @pl.when(pl.program_id(0) == 0)
    def _():
        carry[...] = jnp.zeros_like(carry)

    y = (jnp.dot(a1_ref[...], w1_ref[...], preferred_element_type=F32)
         + jnp.dot(a2_ref[...], w2_ref[...], preferred_element_type=F32))
    x1 = x_ref[...] + gate_ref[0] * y
    xo_ref[...] = x1
    h = _rms_mod(x1, g_ref[...], sh_ref[0], sc_ref[0])
    h_ref[...] = h
    logits = _dot_3pass(h, rw_ref[...])
    logits_t = logits.T[:N_EXPERTS, :]
    e1, e2, g1, g2 = _route(logits_t, rb_ref[...])
    erow = lax.broadcasted_iota(jnp.int32, logits_t.shape, 0)
    hit1 = erow == e1
    hit2 = erow == e2
    onehot = jnp.where(hit1 | hit2, 1.0, 0.0)
    before = jnp.dot(onehot.astype(BF16), upper_ref[...], preferred_element_type=F32) + carry[:, 0:1]
    rank1 = jnp.sum(jnp.where(hit1, before, 0.0), axis=0, keepdims=True).astype(jnp.int32)
    rank2 = jnp.sum(jnp.where(hit2, before, 0.0), axis=0, keepdims=True).astype(jnp.int32)
    total = carry[...] + jnp.sum(onehot, axis=1, keepdims=True)
    carry[...] = total
    cnt_ref[...] = total.astype(jnp.int32)
    row = lax.broadcasted_iota(jnp.int32, ridx_ref.shape, 0)
    ridx_ref[...] = jnp.where(row == 0, e1, jnp.where(row == 1, e2,
                              jnp.where(row == 2, rank1, jnp.where(row == 3, rank2, 0))))
    rgate_ref[...] = jnp.where(row == 0, g1, jnp.where(row == 1, g2, 0.0))


def _out_proj(a1, a2, w_out, x, gate, g, shift, scale, router_w, router_bias, seq, tm):
    n, d = x.shape
    ka = a1.shape[1]
    per_seq = seq // tm
    w1 = w_out[:ka].astype(BF16)
    w2 = w_out[ka:].astype(BF16)
    rw = jnp.zeros((d, LANES), F32).at[:, :N_EXPERTS].set(router_w)
    rb = router_bias.astype(F32).reshape(N_EXPERTS, 1)
    upper = (jnp.arange(tm)[:, None] < jnp.arange(tm)[None, :]).astype(BF16)
    mod_spec = pl.BlockSpec((1, 1, d), lambda i: (i // per_seq, 0, 0))
    return pl.pallas_call(
        _out_proj_kernel,
        grid=(n // tm,),
        in_specs=[pl.BlockSpec((tm, ka), lambda i: (i, 0)),
                  pl.BlockSpec((tm, ka), lambda i: (i, 0)),
                  pl.BlockSpec((ka, d), lambda i: (0, 0), pipeline_mode=pl.Buffered(1)),
                  pl.BlockSpec((ka, d), lambda i: (0, 0), pipeline_mode=pl.Buffered(1)),
                  pl.BlockSpec((tm, d), lambda i: (i, 0)),
                  mod_spec,
                  pl.BlockSpec((1, d), lambda i: (0, 0)),
                  mod_spec, mod_spec,
                  pl.BlockSpec((d, LANES), lambda i: (0, 0)),
                  pl.BlockSpec((N_EXPERTS, 1), lambda i: (0, 0)),
                  pl.BlockSpec((tm, tm), lambda i: (0, 0))],
        out_specs=[pl.BlockSpec((tm, d), lambda i: (i, 0)),
                   pl.BlockSpec((tm, d), lambda i: (i, 0)),
                   pl.BlockSpec((8, tm), lambda i: (0, i)),
                   pl.BlockSpec((8, tm), lambda i: (0, i)),
                   pl.BlockSpec((N_EXPERTS, LANES), lambda i: (0, 0))],
        out_shape=[jax.ShapeDtypeStruct((n, d), F32),
                   jax.ShapeDtypeStruct((n, d), F32),
                   jax.ShapeDtypeStruct((8, n), jnp.int32),
                   jax.ShapeDtypeStruct((8, n), F32),
                   jax.ShapeDtypeStruct((N_EXPERTS, LANES), jnp.int32)],
        scratch_shapes=[pltpu.VMEM((N_EXPERTS, LANES), F32)],
        compiler_params=_cparams(("arbitrary",)),
        name="out_proj_route",
    )(a1, a2, w1, w2, x, gate, g.reshape(1, d), shift, scale, rw, rb, upper)


def _rope_apply(x, cos, sin_signed):
    outs = []
    lane = lax.broadcasted_iota(jnp.int32, cos.shape, 1)
    first_half = (lane & 32) == 0
    for c in range(x.shape[1] // LANES):
        xc = x[:, c * LANES:(c + 1) * LANES]
        partner = jnp.where(first_half, pltpu.roll(xc, LANES - 32, axis=1), pltpu.roll(xc, 32, axis=1))
        outs.append(xc * cos + partner * sin_signed)
    return outs


def _attn_kernel(sink_ref, q_ref, kp_ref, kc_ref, vp_ref, vc_ref, cq_ref, sq_ref, cp_ref, sp_ref, o_ref):
    has_prev = pl.program_id(1) > 0
    cq, sq = cq_ref[...], sq_ref[...]
    q_cols = _rope_apply(q_ref[...] * (HEAD_DIM ** -0.5), cq, sq)
    k_cols = [jnp.concatenate([kp, kc], axis=0) for kp, kc in
              zip(_rope_apply(kp_ref[...], cp_ref[...], sp_ref[...]), _rope_apply(kc_ref[...], cq, sq))]
    v_all = jnp.concatenate([vp_ref[...], vc_ref[...]], axis=0).astype(BF16)
    qi = lax.broadcasted_iota(jnp.int32, (ATT_BLOCK, 2 * ATT_BLOCK), 0)
    kj = lax.broadcasted_iota(jnp.int32, (ATT_BLOCK, 2 * ATT_BLOCK), 1)
    valid = ((kj > qi) & (kj < ATT_BLOCK) & has_prev) | ((kj >= ATT_BLOCK) & (kj - ATT_BLOCK <= qi))
    group = ATT_HEADS // ATT_KV_HEADS
    heads = range(ATT_HEADS)
    scores = []
    for hq in heads:
        hk = hq // group
        qh = q_cols[hq // 2][:, (hq % 2) * HEAD_DIM:(hq % 2 + 1) * HEAD_DIM].astype(BF16)
        kh = k_cols[hk // 2][:, (hk % 2) * HEAD_DIM:(hk % 2 + 1) * HEAD_DIM].astype(BF16)
        s = lax.dot_general(qh, kh, (((1,), (1,)), ((), ())), preferred_element_type=F32)
        scores.append(jnp.where(valid, s, -1e30))
    tops = [jnp.maximum(jnp.max(scores[hq], axis=-1, keepdims=True), sink_ref[hq]) for hq in heads]
    probs = [jnp.exp(scores[hq] - tops[hq]) for hq in heads]
    dens = [jnp.sum(probs[hq], axis=-1, keepdims=True) + jnp.exp(sink_ref[hq] - tops[hq]) for hq in heads]
    outs = [jnp.dot(probs[hq].astype(BF16), v_all[:, (hq // group) * HEAD_DIM:(hq // group + 1) * HEAD_DIM],
                    preferred_element_type=F32) / dens[hq] for hq in heads]
    out_cols = [jnp.concatenate([outs[2 * pair], outs[2 * pair + 1]], axis=1) for pair in range(ATT_HEADS // 2)]
    o_ref[...] = jnp.concatenate(out_cols, axis=1).astype(o_ref.dtype)


def _attention(z, cos_t, sin_t, sinks, batch, seq, k_col, v_col):
    n = z.shape[0]
    nb = seq // ATT_BLOCK
    att_q = ATT_HEADS * HEAD_DIM
    att_kv = ATT_KV_HEADS * HEAD_DIM
    cur = lambda b, i: b * nb + i
    prev = lambda b, i: b * nb + jnp.maximum(i - 1, 0)
    return pl.pallas_call(
        _attn_kernel,
        grid=(batch, nb),
        in_specs=[pl.BlockSpec(memory_space=pltpu.SMEM),
                  pl.BlockSpec((ATT_BLOCK, att_q), lambda b, i: (cur(b, i), 0)),
                  pl.BlockSpec((ATT_BLOCK, att_kv), lambda b, i: (prev(b, i), k_col)),
                  pl.BlockSpec((ATT_BLOCK, att_kv), lambda b, i: (cur(b, i), k_col)),
                  pl.BlockSpec((ATT_BLOCK, att_kv), lambda b, i: (prev(b, i), v_col)),
                  pl.BlockSpec((ATT_BLOCK, att_kv), lambda b, i: (cur(b, i), v_col)),
                  pl.BlockSpec((ATT_BLOCK, LANES), lambda b, i: (cur(b, i), 0)),
                  pl.BlockSpec((ATT_BLOCK, LANES), lambda b, i: (cur(b, i), 0)),
                  pl.BlockSpec((ATT_BLOCK, LANES), lambda b, i: (prev(b, i), 0)),
                  pl.BlockSpec((ATT_BLOCK, LANES), lambda b, i: (prev(b, i), 0))],
        out_specs=pl.BlockSpec((ATT_BLOCK, att_q), lambda b, i: (cur(b, i), 0)),
        out_shape=jax.ShapeDtypeStruct((n, att_q), BF16),
        compiler_params=_cparams(("parallel", "parallel")),
        name="swa_attention",
    )(sinks.astype(F32), z, z, z, z, z, cos_t, sin_t, cos_t, sin_t)


def _conv_kernel(val_ref, gate_ref, hval_ref, hgate_ref, w_ref, b_ref, g_ref, beta_ref, o_ref, ybuf, acc, shifted):
    t = val_ref.shape[0]
    ch = val_ref.shape[1]
    first = pl.program_id(1) == 0
    yh = hval_ref[...] * _sigmoid(hgate_ref[...])
    ybuf[0:CONV_HALO, :] = jnp.where(first, 0.0, yh)
    ybuf[CONV_HALO:, :] = val_ref[...] * _sigmoid(gate_ref[...])
    off = CONV_HALO - (CONV_WIDTH - 1)
    span = t + CONV_HALO - 8
    for ph in range(1, 8):
        shifted[ph - 1] = ybuf[ph:ph + span, :]
    rt = min(t, 128)
    for cb in range(ch // LANES):
        cs = slice(cb * LANES, (cb + 1) * LANES)
        for r0 in range(0, t, rt):
            a = jnp.broadcast_to(b_ref[:, cs], (rt, LANES))
            for j in range(CONV_WIDTH):
                ph, base = (off + j) % 8, (off + j) // 8 * 8
                src = ybuf if ph == 0 else shifted.at[ph - 1]
                a = a + w_ref[j:j + 1, cs] * src[r0 + base:r0 + base + rt, cs]
            acc[r0:r0 + rt, cs] = a
    y = acc[...]
    mu = jnp.mean(y, axis=-1, keepdims=True)
    yc = y - mu
    var = jnp.mean(yc * yc, axis=-1, keepdims=True)
    ln = yc * lax.rsqrt(var + LN_EPS) * g_ref[...] + beta_ref[...]
    o_ref[...] = (ln * _sigmoid(ln)).astype(o_ref.dtype)


def _conformer_conv(z, conv_w, conv_b, ln_g, ln_b, batch, seq, val_col, gate_col, t):
    n = z.shape[0]
    ch = conv_w.shape[-1]
    nt = seq // t
    hpt = t // CONV_HALO
    w = jnp.zeros((CONV_HALO, ch), F32).at[:CONV_WIDTH].set(conv_w.reshape(CONV_WIDTH, ch))
    cur = lambda b, i: b * nt + i
    halo = lambda b, i: jnp.maximum((b * nt + i) * hpt - 1, 0)
    vec = pl.BlockSpec((1, ch), lambda b, i: (0, 0))
    return pl.pallas_call(
        _conv_kernel,
        grid=(batch, nt),
        in_specs=[pl.BlockSpec((t, ch), lambda b, i: (cur(b, i), val_col)),
                  pl.BlockSpec((t, ch), lambda b, i: (cur(b, i), gate_col)),
                  pl.BlockSpec((CONV_HALO, ch), lambda b, i: (halo(b, i), val_col)),
                  pl.BlockSpec((CONV_HALO, ch), lambda b, i: (halo(b, i), gate_col)),
                  pl.BlockSpec((CONV_HALO, ch), lambda b, i: (0, 0)),
                  vec, vec, vec],
        out_specs=pl.BlockSpec((t, ch), lambda b, i: (cur(b, i), 0)),
        out_shape=jax.ShapeDtypeStruct((n, ch), BF16),
        scratch_shapes=[pltpu.VMEM((t + CONV_HALO, ch), F32), pltpu.VMEM((t, ch), F32),
                        pltpu.VMEM((7, t + CONV_HALO - 8, ch), F32)],
        compiler_params=_cparams(("parallel", "parallel")),
        name="conformer_conv",
    )(z, z, z, z, w, conv_b.reshape(1, ch), ln_g.reshape(1, ch), ln_b.reshape(1, ch))


def _shift_lerp(z, halo, mu, first):
    prev = pltpu.roll(z, 1, axis=0)
    row = lax.broadcasted_iota(jnp.int32, z.shape, 0)
    hrow = jnp.where(first, 0.0, halo[7:8, :])
    prev = jnp.where(row == 0, hrow, prev)
    return z + mu * (prev - z)


def _rwkv_kernel(r_ref, k_ref, v_ref, lo_ref, rh_ref, kh_ref, vh_ref, loh_ref,
                 mur_ref, muk_ref, muv_ref, mulo_ref, w0_ref, w2_ref, a0_ref, a2_ref, g2_ref,
                 kk_ref, ka_ref, rk_ref, lng_ref, lnb_ref, seg_ref, tril_ref,
                 o_ref, hs_ref):
    first = pl.program_id(1) == 0

    @pl.when(first)
    def _():
        hs_ref[...] = jnp.zeros_like(hs_ref)

    r = _shift_lerp(r_ref[...], rh_ref[...], mur_ref[...], first)
    k = _shift_lerp(k_ref[...], kh_ref[...], muk_ref[...], first)
    v = _shift_lerp(v_ref[...], vh_ref[...], muv_ref[...], first)
    lo = _shift_lerp(lo_ref[...], loh_ref[...], mulo_ref[...], first)
    zw, za, zg = lo[:, 0:LANES], lo[:, LANES:2 * LANES], lo[:, 2 * LANES:]

    seg = seg_ref[...]
    seg_w = seg.shape[0]

    def head_sum(x):
        return jnp.concatenate([_dot_right_exact(x[:, c:c + seg_w], seg) for c in range(0, x.shape[1], seg_w)],
                               axis=1)

    w_raw = w0_ref[...] + _bdot(jnp.tanh(zw), w2_ref[...])
    log_decay = -jnp.exp(-_softplus(-w_raw) - 0.5)
    a_gate = _sigmoid(a0_ref[...] + _bdot(za, a2_ref[...]))
    g = _bdot(_sigmoid(zg), g2_ref[...])
    kk = k * kk_ref[...]
    kk = kk * lax.rsqrt(jnp.maximum(head_sum(kk * kk), 1e-24))
    k_mod = k * (1.0 + (a_gate - 1.0) * ka_ref[...])
    b_vec = kk * a_gate

    n_chunks = r.shape[0] // RWKV_CHUNK
    rows = [slice(c * RWKV_CHUNK, (c + 1) * RWKV_CHUNK) for c in range(n_chunks)]
    cum = _dot_left_exact(tril_ref[...], log_decay)
    cum_last = [cum[rs.stop - 1:rs.stop, :] for rs in rows]
    w_last = [jnp.exp(cl) for cl in cum_last]
    inv_w = jnp.exp(-cum)
    to_end = jnp.concatenate([jnp.exp(cum_last[c] - cum[rows[c], :]) for c in range(n_chunks)], axis=0)
    a_til = -kk * jnp.exp(cum - log_decay)
    r_til = r * jnp.exp(cum)
    b_til = b_vec * inv_w
    k_til = k_mod * inv_w
    b_hat = b_vec * to_end
    k_hat = k_mod * to_end

    lane = lax.broadcasted_iota(jnp.int32, (RWKV_CHUNK, LANES), 1)
    m0 = lane < HEAD_DIM
    ri = lax.broadcasted_iota(jnp.int32, (LANES, LANES), 0)
    ci = lax.broadcasted_iota(jnp.int32, (LANES, LANES), 1)
    same = (ri < HEAD_DIM) == (ci < HEAD_DIM)
    mask_sl = same & (ci < ri)
    mask_li = same & (ci <= ri)
    eye = ri == ci

    def stack(x):
        return jnp.concatenate([jnp.where(m0, x, 0.0), jnp.where(m0, 0.0, x)], axis=0)

    nt_dims = (((1,), (1,)), ((), ()))
    dot = functools.partial(jnp.dot, preferred_element_type=F32)
    n_pairs = r.shape[1] // LANES
    cols = [slice(p * LANES, (p + 1) * LANES) for p in range(n_pairs)]
    units = [(rows[c], cols[p]) for c in range(n_chunks) for p in range(n_pairs)]
    ids = range(len(units))
    xa_f = [stack(a_til[u]) for u in units]
    xr_f = [stack(r_til[u]) for u in units]
    vs = [stack(v[u]).astype(BF16) for u in units]
    m = [lax.dot_general(jnp.concatenate([xa_f[i], xr_f[i]], axis=0).astype(BF16),
                         jnp.concatenate([b_til[units[i]]] * 2 + [k_til[units[i]]] * 2, axis=0).astype(BF16),
                         nt_dims, preferred_element_type=F32) for i in ids]
    power = [jnp.where(mask_sl, m[i][:LANES, :LANES], 0.0).astype(BF16) for i in ids]
    a_ak = [jnp.where(mask_sl, m[i][:LANES, LANES:], 0.0).astype(BF16) for i in ids]
    z = [jnp.concatenate([xa_f[i], dot(a_ak[i], vs[i])], axis=1) for i in ids]
    for level in range(6):
        z = [z[i] + dot(power[i], z[i].astype(BF16)) for i in ids]
        if level < 5:
            power = [dot(power[i], power[i]).astype(BF16) for i in ids]
    zb = [z[i].astype(BF16) for i in ids]
    m_rb = [jnp.where(mask_li, m[i][LANES:, :LANES], 0.0).astype(BF16) for i in ids]
    m_rk = [jnp.where(mask_li, m[i][LANES:, LANES:], 0.0).astype(BF16) for i in ids]
    q = [jnp.concatenate([xr_f[i], dot(m_rk[i], vs[i])], axis=1) + dot(m_rb[i], zb[i]) for i in ids]
    ge = [dot(stack(b_hat[units[i]]).T.astype(BF16), zb[i]) for i in ids]
    e_mat = [ge[i][:, LANES:] + dot(stack(k_hat[units[i]]).T.astype(BF16), vs[i]) for i in ids]
    qg = [jnp.concatenate([q[i][:, :LANES],
                           ge[i][:, :LANES] + jnp.where(eye, w_last[i // n_pairs][:, cols[i % n_pairs]], 0.0)],
                          axis=0).astype(BF16) for i in ids]
    state = [hs_ref[p] for p in range(n_pairs)]
    y_rows = []
    for c in range(n_chunks):
        y_cols = []
        for p in range(n_pairs):
            i = c * n_pairs + p
            yh = dot(qg[i], state[p].astype(BF16))
            state[p] = yh[LANES:] + e_mat[i]
            y2 = yh[:LANES] + q[i][:, LANES:]
            y_cols.append(y2[:RWKV_CHUNK] + y2[RWKV_CHUNK:])
        y_rows.append(jnp.concatenate(y_cols, axis=1))
    for p in range(n_pairs):
        hs_ref[p] = state[p]
    y = jnp.concatenate(y_rows, axis=0)

    inv_hd = 1.0 / HEAD_DIM
    mu = head_sum(y) * inv_hd
    yc = y - mu
    var = head_sum(yc * yc) * inv_hd
    yn = yc * lax.rsqrt(var + RWKV_LN_EPS) * lng_ref[...] + lnb_ref[...]
    bonus = head_sum(r * k_mod * rk_ref[...]) * v
    o_ref[...] = ((yn + bonus) * g).astype(o_ref.dtype)


def _rwkv(z, p, batch, seq, lo_col):
    n = z.shape[0]
    dim = p["w0"].shape[-1]
    t = RWKV_ROWS
    nt = seq // t
    hpt = t // 8
    lo_w = 4 * LANES
    cur = lambda b, i: b * nt + i
    halo = lambda b, i: jnp.maximum((b * nt + i) * hpt - 1, 0)
    si = jnp.arange(MXU_TILE)
    seg = (si[:, None] // HEAD_DIM == si[None, :] // HEAD_DIM).astype(BF16)
    ti = jnp.arange(t)
    tril = ((ti[:, None] >= ti[None, :]) & (ti[:, None] // RWKV_CHUNK == ti[None, :] // RWKV_CHUNK)).astype(BF16)

    def full(a):
        return pl.BlockSpec(a.shape, lambda b, i: (0,) * a.ndim)

    consts = [p["mu_r"], p["mu_k"], p["mu_v"], p["mu_lo"], p["w0"], p["w2"], p["a0"], p["a2"], p["g2"],
              p["k_k"], p["k_a"], p["r_k"], p["ln_g"], p["ln_b"], seg, tril]
    col = lambda c: pl.BlockSpec((t, dim), lambda b, i: (cur(b, i), c))
    hcol = lambda c: pl.BlockSpec((8, dim), lambda b, i: (halo(b, i), c))
    return pl.pallas_call(
        _rwkv_kernel,
        grid=(batch, nt),
        in_specs=[col(0), col(1), col(2), pl.BlockSpec((t, lo_w), lambda b, i: (cur(b, i), lo_col)),
                  hcol(0), hcol(1), hcol(2), pl.BlockSpec((8, lo_w), lambda b, i: (halo(b, i), lo_col))]
                 + [full(a) for a in consts],
        out_specs=pl.BlockSpec((t, dim), lambda b, i: (cur(b, i), 0)),
        out_shape=jax.ShapeDtypeStruct((n, dim), BF16),
        scratch_shapes=[pltpu.VMEM((dim // LANES, LANES, LANES), F32)],
        compiler_params=_cparams(("parallel", "arbitrary")),
        name="rwkv7_chunked",
    )(z, z, z, z, z, z, z, z, *consts)


def _lru_kernel(x_ref, gb_ref, xh_ref, cw_ref, cb_ref, wg_ref, ba_ref, bx_ref, lam_ref, o_ref,
                xbuf, carry):
    t, ch = x_ref.shape
    first = pl.program_id(1) == 0

    @pl.when(first)
    def _():
        carry[...] = jnp.zeros_like(carry)

    xbuf[0:8, :] = jnp.where(first, 0.0, xh_ref[...])
    xbuf[8:, :] = x_ref[...]
    off = 8 - (LRU_CONV_WIDTH - 1)
    xc = jnp.broadcast_to(cb_ref[...], (t, ch))
    for j in range(LRU_CONV_WIDTH):
        xc = xc + cw_ref[j:j + 1, :] * xbuf[off + j:off + j + t, :]
    gr, gi = [], []
    for p in range(ch // LANES):
        gp = _bdot(xc[:, p * LANES:(p + 1) * LANES], wg_ref[p])
        gr.append(gp[:, :LANES])
        gi.append(gp[:, LANES:])
    r = _sigmoid(jnp.concatenate(gr, axis=1) + ba_ref[...])
    i = _sigmoid(jnp.concatenate(gi, axis=1) + bx_ref[...])
    log_a = -LRU_C * r * _softplus(-lam_ref[...])
    a = jnp.exp(log_a)
    u = jnp.sqrt(1.0 - jnp.exp(2.0 * log_a)) * (i * xc)
    a = a.reshape(t // 8, 8, ch)
    u = u.reshape(t // 8, 8, ch)
    row = lax.broadcasted_iota(jnp.int32, a.shape, 1)
    d = 1
    while d < 8:
        keep = row >= d
        a_sh = jnp.where(keep, pltpu.roll(a, d, axis=1), 1.0)
        u_sh = jnp.where(keep, pltpu.roll(u, d, axis=1), 0.0)
        u = a * u_sh + u
        a = a * a_sh
        d *= 2
    h_prev = carry[...]
    groups = []
    for g in range(t // 8):
        hg = u[g] + a[g] * h_prev
        groups.append(hg)
        h_prev = hg[7:8, :]
    h = jnp.concatenate(groups, axis=0)
    carry[...] = h_prev
    gb = gb_ref[...]
    gelu = 0.5 * gb * (1.0 + jnp.tanh(0.7978845608028654 * (gb + 0.044715 * (gb * gb * gb))))
    o_ref[...] = (h * gelu).astype(o_ref.dtype)


def _rglru(z, conv_w, conv_b, wa, ba, wx, bx, lam, batch, seq, x_col, g_col, t):
    n = z.shape[0]
    ch = lam.shape[-1]
    nt = seq // t
    hpt = t // 8
    nblk, bd = wa.shape[0], wa.shape[1]
    per = LANES // bd
    def blockdiag(w):
        w = w.reshape(nblk // per, per, bd, bd)
        out = jnp.zeros((nblk // per, LANES, LANES), F32)
        for q in range(per):
            out = out.at[:, q * bd:(q + 1) * bd, q * bd:(q + 1) * bd].set(w[:, q])
        return out
    wg = jnp.concatenate([blockdiag(wa), blockdiag(wx)], axis=-1).astype(BF16)
    cw = jnp.zeros((8, ch), F32).at[:LRU_CONV_WIDTH].set(conv_w.reshape(LRU_CONV_WIDTH, ch))
    cur = lambda b, i: b * nt + i
    halo = lambda b, i: jnp.maximum((b * nt + i) * hpt - 1, 0)
    vec = pl.BlockSpec((1, ch), lambda b, i: (0, 0))
    return pl.pallas_call(
        _lru_kernel,
        grid=(batch, nt),
        in_specs=[pl.BlockSpec((t, ch), lambda b, i: (cur(b, i), x_col)),
                  pl.BlockSpec((t, ch), lambda b, i: (cur(b, i), g_col)),
                  pl.BlockSpec((8, ch), lambda b, i: (halo(b, i), x_col)),
                  pl.BlockSpec((8, ch), lambda b, i: (0, 0)),
                  vec,
                  pl.BlockSpec(wg.shape, lambda b, i: (0, 0, 0)),
                  vec, vec, vec],
        out_specs=pl.BlockSpec((t, ch), lambda b, i: (cur(b, i), 0)),
        out_shape=jax.ShapeDtypeStruct((n, ch), BF16),
        scratch_shapes=[pltpu.VMEM((t + 8, ch), F32), pltpu.VMEM((1, ch), F32)],
        compiler_params=_cparams(("parallel", "arbitrary")),
        name="rglru",
    )(z, z, z, cw, conv_b.reshape(1, ch), wg, ba.reshape(1, ch), bx.reshape(1, ch), lam.reshape(1, ch))


def _moe_kernel(be_ref, tok_ref, dst_ref, h_hbm, w1_ref, w3_ref, w2_ref, o_hbm, x0, x1, o0, o1, gsem, ssem):
    i = pl.program_id(0)
    last = pl.num_programs(0) - 1
    xs, outs = (x0, x1), (o0, o1)

    def gather_row(j, r, s):
        return pltpu.make_async_copy(h_hbm.at[pl.ds(tok_ref[j, r], 1)], xs[s].at[pl.ds(r, 1)], gsem.at[s])

    def scatter_row(j, r, s):
        return pltpu.make_async_copy(outs[s].at[pl.ds(r, 1)], o_hbm.at[pl.ds(dst_ref[j, r], 1)], ssem.at[s])

    def gather_block(s):
        return pltpu.make_async_copy(h_hbm.at[pl.ds(0, MOE_BLOCK)], xs[s], gsem.at[s])

    def scatter_block(s):
        return pltpu.make_async_copy(outs[s], o_hbm.at[pl.ds(0, MOE_BLOCK)], ssem.at[s])

    @pl.when(i == 0)
    def _():
        o1[...] = jnp.zeros_like(o1)
        for r in range(MOE_BLOCK):
            gather_row(0, r, 0).start()

    def step(s):
        t = 1 - s
        gather_block(s).wait()

        @pl.when(i >= 1)
        def _():
            scatter_block(s).wait()

        nxt = jnp.minimum(i + 1, last)
        prv = jnp.maximum(i - 1, 0)
        f = w1_ref.shape[-1]
        n_groups = f // MOE_FCHUNK
        per_group = MOE_BLOCK // n_groups
        for c in range(n_groups):
            @pl.when(i >= 0)
            def _():
                for r in range(c * per_group, (c + 1) * per_group):
                    gather_row(nxt, r, t).start(priority=r % 2)
                    scatter_row(prv, r, t).start(priority=r % 2)

            fs = slice(c * MOE_FCHUNK, (c + 1) * MOE_FCHUNK)
            x = xs[s][...].astype(BF16)
            h1 = jnp.dot(x, w1_ref[0, :, fs], preferred_element_type=F32)
            h3 = jnp.dot(x, w3_ref[0, :, fs], preferred_element_type=F32)
            hid = (h1 * _sigmoid(h1) * h3).astype(BF16)
            part = jnp.dot(hid, w2_ref[0, fs, :], preferred_element_type=F32)
            if c == 0:
                outs[s][...] = part
            else:
                outs[s][...] += part

        @pl.when(i == last)
        def _():
            for r in range(MOE_BLOCK):
                scatter_row(i, r, s).start()
            scatter_block(t).wait()
            scatter_block(s).wait()
            gather_block(t).wait()

    for s in range(2):
        pl.when(i % 2 == s)(functools.partial(step, s))


def _moe_experts(h, block_e, row_tok, row_dst, w1, w3, w2):
    n, d = h.shape
    f = w1.shape[-1]
    n_blocks = row_dst.shape[0]
    rows = n_blocks * MOE_BLOCK
    grid_spec = pltpu.PrefetchScalarGridSpec(
        num_scalar_prefetch=3,
        grid=(n_blocks,),
        in_specs=[pl.BlockSpec(memory_space=pl.ANY),
                  pl.BlockSpec((1, d, f), lambda i, be, tok, dst: (be[i], 0, 0)),
                  pl.BlockSpec((1, d, f), lambda i, be, tok, dst: (be[i], 0, 0)),
                  pl.BlockSpec((1, f, d), lambda i, be, tok, dst: (be[i], 0, 0))],
        out_specs=pl.BlockSpec(memory_space=pl.ANY),
        scratch_shapes=[pltpu.VMEM((MOE_BLOCK, d), F32)] * 4
                       + [pltpu.SemaphoreType.DMA((2,)), pltpu.SemaphoreType.DMA((2,))],
    )
    return pl.pallas_call(
        _moe_kernel,
        grid_spec=grid_spec,
        out_shape=jax.ShapeDtypeStruct((rows, d), F32),
        compiler_params=_cparams(("arbitrary",)),
        name="moe_experts",
    )(block_e, row_tok, row_dst, h, w1, w3, w2)


INVERT_UNROLL = 16


def _invert_kernel(trips_ref, dest_ref, placed_ref):
    def clear(b, carry):
        for u in range(INVERT_UNROLL):
            placed_ref[b * INVERT_UNROLL + u] = -1
        return carry

    lax.fori_loop(0, trips_ref[0], clear, 0)

    def place(b, carry):
        for u in range(INVERT_UNROLL):
            i = b * INVERT_UNROLL + u
            placed_ref[dest_ref[i]] = i
        return carry

    lax.fori_loop(0, trips_ref[1], place, 0)


def _invert_placement(dest_flat, rows):
    trips = jnp.array([rows // INVERT_UNROLL, dest_flat.shape[0] // INVERT_UNROLL], jnp.int32)
    smem = pl.BlockSpec(memory_space=pltpu.SMEM)
    return pl.pallas_call(
        _invert_kernel,
        in_specs=[smem, smem],
        out_specs=smem,
        out_shape=jax.ShapeDtypeStruct((rows,), jnp.int32),
        name="invert_placement",
    )(trips, dest_flat)


def _moe_plan(ridx, counts, n):
    nk = 2 * n
    e = ridx[0:2]
    rank = ridx[2:4]
    padded = (counts + MOE_BLOCK - 1) // MOE_BLOCK * MOE_BLOCK
    pend = jnp.cumsum(padded)
    pstart = pend - padded
    experts = jnp.arange(N_EXPERTS, dtype=jnp.int32)[:, None, None]
    dest = rank + jnp.sum(jnp.where(e[None] == experts, pstart[:, None, None], 0), axis=0)
    n_blocks = nk // MOE_BLOCK + N_EXPERTS
    rows = n_blocks * MOE_BLOCK
    placed = _invert_placement(dest.reshape(-1), rows)
    is_pad = placed < 0
    pad_rank = jnp.cumsum(is_pad.astype(jnp.int32)) - 1
    row_dst = jnp.where(is_pad, nk + pad_rank, placed)
    row_tok = jnp.where(is_pad, 0, jnp.where(placed >= n, placed - n, placed))
    block_start = jnp.arange(n_blocks, dtype=jnp.int32) * MOE_BLOCK
    block_e = jnp.minimum(jnp.sum(block_start[:, None] >= pend[None, :], axis=1), N_EXPERTS - 1).astype(jnp.int32)
    return block_e, row_tok.reshape(n_blocks, MOE_BLOCK), row_dst.reshape(n_blocks, MOE_BLOCK)


def _final_kernel(x_ref, o1_ref, o2_ref, rg_ref, gate_ref, g_ref, out_ref):
    g = rg_ref[...].T
    y = g[:, 0:1] * o1_ref[...] + g[:, 1:2] * o2_ref[...]
    x2 = x_ref[...] + gate_ref[0] * y
    ms = jnp.mean(x2 * x2, axis=-1, keepdims=True)
    out_ref[...] = x2 * lax.rsqrt(ms + EPS) * g_ref[...]


def _final_residual_norm(x, o_all, rgate, gate, g, seq, tm):
    n, d = x.shape
    per_seq = seq // tm
    nt = n // tm
    return pl.pallas_call(
        _final_kernel,
        grid=(nt,),
        in_specs=[pl.BlockSpec((tm, d), lambda i: (i, 0)),
                  pl.BlockSpec((tm, d), lambda i: (i, 0)),
                  pl.BlockSpec((tm, d), lambda i: (i + nt, 0)),
                  pl.BlockSpec((8, tm), lambda i: (0, i)),
                  pl.BlockSpec((1, 1, d), lambda i: (i // per_seq, 0, 0)),
                  pl.BlockSpec((1, d), lambda i: (0, 0))],
        out_specs=pl.BlockSpec((tm, d), lambda i: (i, 0)),
        out_shape=jax.ShapeDtypeStruct((n, d), F32),
        compiler_params=_cparams(("parallel",)),
        name="final_residual_norm",
    )(x, o_all, o_all, rgate, gate, g.reshape(1, d))


def _pad_cols(w, width):
    return jnp.pad(w, ((0, 0), (0, width - w.shape[1])))


def _pad_rows(w, height):
    return jnp.pad(w, ((0, height - w.shape[0]), (0, 0)))


def kernel(x, c, positions, router_w, router_bias, ada_w, ada_b, norm_mix, norm_ffn, moe_w1, moe_w3, moe_w2, ab_w_in, ab_sinks, ab_conv_w, ab_conv_b, ab_conv_ln_g, ab_conv_ln_b, ab_w_out, cd_w_in, cd_shift_mu, cd_w0, cd_w2, cd_a0, cd_a2, cd_g2, cd_k_k, cd_k_a, cd_r_k, cd_ln_x_g, cd_ln_x_b, cd_lru_conv_w, cd_lru_conv_b, cd_lru_wa, cd_lru_ba, cd_lru_wx, cd_lru_bx, cd_lru_lambda, cd_w_out, final_norm):
    batch, seq, d = x.shape
    n = batch * seq
    depth = ada_w.shape[0]
    xf = x.reshape(n, d)
    mod = _ada_mod(c, ada_w, ada_b).reshape(depth, batch, 6, 1, d)
    cos_t, sin_t = _rope_tables(positions)
    att_q = ATT_HEADS * HEAD_DIM
    att_kv = ATT_KV_HEADS * HEAD_DIM

    def in_proj(xf, layer, shift_m, scale_m, w_perm, pending):
        z = _in_proj(xf, norm_mix[layer], shift_m, scale_m, w_perm, seq, PROJ_ROWS, PROJ_COLS, pending)
        return (z, xf) if pending is None else z

    pending = None
    for layer in range(depth):
        j = layer // 2
        shift_m, scale_m, gate_m, shift_f, scale_f, gate_f = (mod[layer, :, q] for q in range(6))
        if layer % 2 == 0:
            w_in = ab_w_in[j]
            w_perm = jnp.concatenate([w_in[:, :att_q], w_in[:, att_q + 2 * att_kv:],
                                      w_in[:, att_q:att_q + 2 * att_kv]], axis=1).astype(BF16)
            z, xf = in_proj(xf, layer, shift_m, scale_m, w_perm, pending)
            conv_ch = ab_conv_w.shape[-1]
            kv0 = (att_q + 2 * conv_ch) // att_kv
            a1 = _attention(z, cos_t, sin_t, ab_sinks[j], batch, seq, kv0, kv0 + 1)
            a2 = _conformer_conv(z, ab_conv_w[j], ab_conv_b[j], ab_conv_ln_g[j], ab_conv_ln_b[j],
                                 batch, seq, att_q // conv_ch, att_q // conv_ch + 1, CONV_ROWS)
            w_out = ab_w_out[j]
        else:
            w_in = cd_w_in[j]
            dim = cd_w0.shape[-1]
            dl = cd_w2.shape[1]
            al = cd_a2.shape[1]
            gl = cd_g2.shape[1]
            s0 = 3 * dim
            seg_zw = w_in[:, s0:s0 + dl]
            seg_za = w_in[:, s0 + dl:s0 + dl + al]
            seg_zg = w_in[:, s0 + dl + al:s0 + dl + al + gl]
            s1 = s0 + dl + al + gl
            w_perm = jnp.concatenate([w_in[:, :s0], w_in[:, s1:], _pad_cols(seg_zw, LANES),
                                      _pad_cols(seg_za, LANES), seg_zg], axis=1).astype(BF16)
            mu = cd_shift_mu[j]
            mu_lo = jnp.concatenate([jnp.pad(mu[s0:s0 + dl], (0, LANES - dl)),
                                     jnp.pad(mu[s0 + dl:s0 + dl + al], (0, LANES - al)),
                                     mu[s0 + dl + al:s1]]).reshape(1, -1)
            row = lambda a: a.reshape(1, -1)
            p = dict(mu_r=row(mu[:dim]), mu_k=row(mu[dim:2 * dim]), mu_v=row(mu[2 * dim:s0]), mu_lo=mu_lo,
                     w0=row(cd_w0[j]), w2=_pad_rows(cd_w2[j], LANES).astype(BF16),
                     a0=row(cd_a0[j]), a2=_pad_rows(cd_a2[j], LANES).astype(BF16), g2=cd_g2[j].astype(BF16),
                     k_k=row(cd_k_k[j]), k_a=row(cd_k_a[j]), r_k=row(cd_r_k[j]),
                     ln_g=row(cd_ln_x_g[j]), ln_b=row(cd_ln_x_b[j]))
            z, xf = in_proj(xf, layer, shift_m, scale_m, w_perm, pending)
            lo_col = (5 * dim) // (4 * LANES)
            a1 = _rwkv(z, p, batch, seq, lo_col)
            a2 = _rglru(z, cd_lru_conv_w[j], cd_lru_conv_b[j], cd_lru_wa[j], cd_lru_ba[j],
                        cd_lru_wx[j], cd_lru_bx[j], cd_lru_lambda[j], batch, seq, 3, 4, LRU_ROWS)
            w_out = cd_w_out[j]
        x1, h, ridx, rgate, counts = _out_proj(a1, a2, w_out, xf, gate_m, norm_ffn[layer], shift_f, scale_f,
                                               router_w, router_bias, seq, PROJ_ROWS)
        block_e, row_tok, row_dst = _moe_plan(ridx, counts[:, 0], n)
        o_all = _moe_experts(h, block_e, row_tok, row_dst, moe_w1[layer].astype(BF16),
                             moe_w3[layer].astype(BF16), moe_w2[layer].astype(BF16))
        xf, pending = x1, (o_all, rgate, gate_f)
    out = _final_residual_norm(xf, *pending, final_norm, seq, FINAL_ROWS)
    return out.reshape(batch, seq, d)
```
